```python
import jax, jax.numpy as jnp
from jax import lax
import numpy as np

D_MODEL = 1024
BATCH = 8
SEQ = 2048
DEPTH = 1
DEC_BATCH = 128
DEC_SEQ = 4
PAST_LEN = 16384
PAGE_SIZE = 128

PLE_DIM = 256
D_SG = D_MODEL
SG_HEADS = 4
SG_HEAD_DIM = D_SG // SG_HEADS
CHUNK = 128
D_LRU = D_MODEL
LRU_BLOCKS = 8
LRU_BLOCK_DIM = D_LRU // LRU_BLOCKS
CONV_WIDTH = 4
LRU_C = 8.0
EPS = 1e-6
D_IN = 3 * D_SG + 2 * D_LRU

kernel_name = 'hybrid_sgmlp_rglru_decoder_step'


def rms_norm(x, g):
    xf = x.astype(jnp.float32)
    var = jnp.mean(xf * xf, axis=-1, keepdims=True)
    return (xf * lax.rsqrt(var + EPS) * g.astype(jnp.float32)).astype(x.dtype)


def chunk_spatial_mix(z, w_s, b_s):
    B, T = z.shape[0], z.shape[1]
    n_chunks = -(-T // CHUNK)
    pad = n_chunks * CHUNK - T
    zp = jnp.pad(z, ((0, 0), (0, pad), (0, 0), (0, 0)))
    zc = zp.reshape(B, n_chunks, CHUNK, SG_HEADS, SG_HEAD_DIM)
    mask = jnp.tril(jnp.ones((CHUNK, CHUNK), dtype=bool))
    w = jnp.where(mask[None], w_s, jnp.zeros_like(w_s)).astype(z.dtype)
    s = jnp.einsum('hts,bnshd->bnthd', w, zc) + b_s.T.astype(z.dtype)[None, None, :, :, None]
    return s.reshape(B, n_chunks * CHUNK, SG_HEADS, SG_HEAD_DIM)[:, :T]


def causal_conv(xb, buf, w, b):
    T = xb.shape[1]
    xp = jnp.concatenate([buf.astype(xb.dtype), xb], axis=1)
    y = b.astype(xb.dtype)
    for k in range(CONV_WIDTH):
        y = y + w[k].astype(xb.dtype) * xp[:, k:k + T]
    return y, xp[:, -(CONV_WIDTH - 1):]


def rg_lru(xc, h0, wa, ba, wx, bx, lam):
    B, T = xc.shape[0], xc.shape[1]
    xb = xc.reshape(B, T, LRU_BLOCKS, LRU_BLOCK_DIM)
    r = jax.nn.sigmoid(jnp.einsum('btnc,ncd->btnd', xb, wa).reshape(B, T, D_LRU) + ba)
    i = jax.nn.sigmoid(jnp.einsum('btnc,ncd->btnd', xb, wx).reshape(B, T, D_LRU) + bx)
    log_a = (-LRU_C * jax.nn.softplus(-lam.astype(jnp.float32))) * r.astype(jnp.float32)
    a = jnp.exp(log_a)
    mult = jnp.sqrt(-jnp.expm1(2.0 * log_a))
    bterm = mult * (i * xc).astype(jnp.float32)

    def combine(c1, c2):
        a1, b1 = c1
        a2, b2 = c2
        return a1 * a2, a2 * b1 + b2

    a_cum, h_zero = lax.associative_scan(combine, (a, bterm), axis=1)
    h = h_zero + a_cum * h0.astype(jnp.float32)[:, None]
    return h.astype(xc.dtype), h[:, -1].astype(h0.dtype)


def mixer_layer(x, p_l, conv_buf, h0, norm_pre, w_in, sg_norm, sg_w, sg_b, conv_w, conv_b,
                lru_wa, lru_ba, lru_wx, lru_bx, lru_lambda, w_branch_sg, w_branch_lru,
                w_merge, b_merge, w_out, norm_post, w_ple, w_ple_gate, b_ple_gate):
    B, T = x.shape[0], x.shape[1]
    h = rms_norm(x, norm_pre)
    proj = h @ w_in
    u, v, g_sg, x_lru, g_lru = jnp.split(
        proj, [D_SG, 2 * D_SG, 3 * D_SG, 3 * D_SG + D_LRU], axis=-1)
    z = rms_norm(v.reshape(B, T, SG_HEADS, SG_HEAD_DIM), sg_norm.reshape(SG_HEADS, SG_HEAD_DIM))
    s = chunk_spatial_mix(z, sg_w, sg_b).reshape(B, T, D_SG)
    y_sg = u * s * jax.nn.silu(g_sg)
    xc, new_conv = causal_conv(x_lru, conv_buf, conv_w, conv_b)
    hseq, new_h = rg_lru(xc, h0, lru_wa, lru_ba, lru_wx, lru_bx, lru_lambda)
    y_lru = hseq * jax.nn.silu(g_lru)
    g_a, g_b = jnp.split(jax.nn.sigmoid(h @ w_merge + b_merge), 2, axis=-1)
    merged = g_a * (y_sg @ w_branch_sg) + g_b * (y_lru @ w_branch_lru)
    x = x + rms_norm(merged @ w_out, norm_post)
    x = x + jax.nn.sigmoid(x @ w_ple_gate + b_ple_gate) * (p_l @ w_ple)
    return x, new_conv, new_h, z.reshape(B, T, D_SG)


def setup_inputs(seed: int = 0) -> dict:
    key = jax.random.key(seed)
    ks = jax.random.split(key, 32)
    f32 = jnp.float32

    def nrm(k, shape, scale):
        return jax.random.normal(k, shape, f32) * scale

    a0 = jax.random.uniform(ks[20], (DEPTH, D_LRU), f32, 0.9, 0.999)
    s0 = a0 ** (1.0 / LRU_C)
    lru_lambda = jnp.log(s0) - jnp.log1p(-s0)
    return {
        'x_prompt': nrm(ks[0], (BATCH, SEQ, D_MODEL), 1.0),
        'x_sample': nrm(ks[1], (DEC_BATCH, DEC_SEQ, D_MODEL), 1.0),
        'p_prompt': nrm(ks[2], (DEPTH, BATCH, SEQ, PLE_DIM), 1.0),
        'p_sample': nrm(ks[3], (DEPTH, DEC_BATCH, DEC_SEQ, PLE_DIM), 1.0),
        'state_conv': nrm(ks[4], (DEPTH, DEC_BATCH, CONV_WIDTH - 1, D_LRU), 1.0),
        'state_lru': nrm(ks[5], (DEPTH, DEC_BATCH, D_LRU), 0.5),
        'norm_pre': 1.0 + nrm(ks[6], (DEPTH, D_MODEL), 0.05),
        'w_in': nrm(ks[7], (DEPTH, D_MODEL, D_IN), D_MODEL ** -0.5),
        'sg_norm': 1.0 + nrm(ks[8], (DEPTH, D_SG), 0.05),
        'sg_w': nrm(ks[9], (DEPTH, SG_HEADS, CHUNK, CHUNK), 0.5 * CHUNK ** -0.5),
        'sg_b': 1.0 + nrm(ks[10], (DEPTH, SG_HEADS, CHUNK), 0.05),
        'conv_w': nrm(ks[11], (DEPTH, CONV_WIDTH, D_LRU), CONV_WIDTH ** -0.5),
        'conv_b': nrm(ks[12], (DEPTH, D_LRU), 0.02),
        'lru_wa': nrm(ks[13], (DEPTH, LRU_BLOCKS, LRU_BLOCK_DIM, LRU_BLOCK_DIM), LRU_BLOCK_DIM ** -0.5),
        'lru_ba': nrm(ks[14], (DEPTH, D_LRU), 0.02),
        'lru_wx': nrm(ks[15], (DEPTH, LRU_BLOCKS, LRU_BLOCK_DIM, LRU_BLOCK_DIM), LRU_BLOCK_DIM ** -0.5),
        'lru_bx': nrm(ks[16], (DEPTH, D_LRU), 0.02),
        'lru_lambda': lru_lambda,
        'w_branch_sg': nrm(ks[17], (DEPTH, D_SG, D_MODEL), D_SG ** -0.5),
        'w_branch_lru': nrm(ks[18], (DEPTH, D_LRU, D_MODEL), D_LRU ** -0.5),
        'w_merge': nrm(ks[19], (DEPTH, D_MODEL, 2 * D_MODEL), D_MODEL ** -0.5),
        'b_merge': nrm(ks[21], (DEPTH, 2 * D_MODEL), 0.02),
        'w_out': nrm(ks[22], (DEPTH, D_MODEL, D_MODEL), D_MODEL ** -0.5),
        'norm_post': 1.0 + nrm(ks[23], (DEPTH, D_MODEL), 0.05),
        'w_ple': nrm(ks[24], (DEPTH, PLE_DIM, D_MODEL), PLE_DIM ** -0.5),
        'w_ple_gate': nrm(ks[25], (DEPTH, D_MODEL, D_MODEL), D_MODEL ** -0.5),
        'b_ple_gate': nrm(ks[26], (DEPTH, D_MODEL), 0.02),
    }


def reference(x_prompt, x_sample, p_prompt, p_sample, state_conv, state_lru,
              norm_pre, w_in, sg_norm, sg_w, sg_b, conv_w, conv_b,
              lru_wa, lru_ba, lru_wx, lru_bx, lru_lambda, w_branch_sg, w_branch_lru,
              w_merge, b_merge, w_out, norm_post, w_ple, w_ple_gate, b_ple_gate):
    xp = x_prompt
    xs = x_sample
    conv_p, lru_p, conv_s, lru_s, chunk_s = [], [], [], [], []
    for l in range(DEPTH):
        lw = (norm_pre[l], w_in[l], sg_norm[l], sg_w[l], sg_b[l], conv_w[l], conv_b[l],
              lru_wa[l], lru_ba[l], lru_wx[l], lru_bx[l], lru_lambda[l],
              w_branch_sg[l], w_branch_lru[l], w_merge[l], b_merge[l], w_out[l],
              norm_post[l], w_ple[l], w_ple_gate[l], b_ple_gate[l])
        buf0 = jnp.zeros((xp.shape[0], CONV_WIDTH - 1, D_LRU), xp.dtype)
        h00 = jnp.zeros((xp.shape[0], D_LRU), state_lru.dtype)
        xp, cp, hp, _ = mixer_layer(xp, p_prompt[l], buf0, h00, *lw)
        xs, cs, hs, zs = mixer_layer(xs, p_sample[l], state_conv[l], state_lru[l], *lw)
        conv_p.append(cp)
        lru_p.append(hp)
        conv_s.append(cs)
        lru_s.append(hs)
        chunk_s.append(zs)
    conv_prompt = jnp.stack(conv_p)
    lru_prompt = jnp.stack(lru_p)
    conv_sample = jnp.stack(conv_s)
    lru_sample = jnp.stack(lru_s)
    chunk_v_sample = jnp.stack(chunk_s)
    return (xp, xs, conv_prompt, lru_prompt, conv_sample, lru_sample, chunk_v_sample)
```

```python
import functools

import jax
import jax.numpy as jnp
from jax import lax
from jax.experimental import pallas as pl
from jax.experimental.pallas import tpu as pltpu

D_MODEL = 1024
PLE_DIM = 256
SG_HEADS = 4
SG_HEAD_DIM = D_MODEL // SG_HEADS
CHUNK = 128
LRU_BLOCKS = 8
LRU_BLOCK_DIM = D_MODEL // LRU_BLOCKS
CONV_WIDTH = 4
LRU_C = 8.0
EPS = 1e-6

SUBLANES = 8
PROMPT_TILE = 256
VMEM_LIMIT_BYTES = 56 * 1024 * 1024

_BF16 = jnp.bfloat16
_F32 = jnp.float32


def _dot(a, b):
    return jnp.dot(a, b, preferred_element_type=_F32)


def _rms_scale(x):
    var = jnp.mean(x * x, axis=-1, keepdims=True)
    return x * lax.rsqrt(var + EPS)


def _silu(x):
    return x * jax.nn.sigmoid(x)


def _head_norm(v, sgn):
    zs = []
    for h in range(SG_HEADS):
        sl = slice(h * SG_HEAD_DIM, (h + 1) * SG_HEAD_DIM)
        zs.append(_rms_scale(v[:, sl]) * sgn[:, sl])
    return jnp.concatenate(zs, axis=-1)


def _lru_coeffs(xc, wg_ref, ba, bx, lam):
    xcb = xc.astype(_BF16)
    rs, is_ = [], []
    for n in range(LRU_BLOCKS):
        sl = slice(n * LRU_BLOCK_DIM, (n + 1) * LRU_BLOCK_DIM)
        g = _dot(xcb[:, sl], wg_ref[n])
        rs.append(g[:, :LRU_BLOCK_DIM])
        is_.append(g[:, LRU_BLOCK_DIM:])
    r = jax.nn.sigmoid(jnp.concatenate(rs, axis=-1) + ba)
    i = jax.nn.sigmoid(jnp.concatenate(is_, axis=-1) + bx)
    log_a = (-LRU_C * jax.nn.softplus(-lam)) * r
    a = jnp.exp(log_a)
    mult = jnp.sqrt(1.0 - a * a)
    return a, mult * (i * xc)


def _post_mix(x, hb, y_sg, y_lru, p, w):
    g_a = jax.nn.sigmoid(_dot(hb, w["w_merge"][:, :D_MODEL]) + w["b_merge"][:, :D_MODEL])
    merged = g_a * _dot(y_sg.astype(_BF16), w["w_branch_sg"][...])
    g_b = jax.nn.sigmoid(_dot(hb, w["w_merge"][:, D_MODEL:]) + w["b_merge"][:, D_MODEL:])
    merged = merged + g_b * _dot(y_lru.astype(_BF16), w["w_branch_lru"][...])
    o = _dot(merged.astype(_BF16), w["w_out"][...])
    x1 = x + _rms_scale(o) * w["norm_post"][...]
    gate = jax.nn.sigmoid(_dot(x1.astype(_BF16), w["w_ple_gate"][...]) + w["b_ple_gate"][...])
    return x1 + gate * _dot(p.astype(_BF16), w["w_ple"][...])


_WEIGHT_NAMES = (
    "norm_pre", "w_in", "sg_norm", "conv_w", "conv_b", "w_gate", "lru_ba", "lru_bx",
    "lru_lambda", "w_branch_sg", "w_branch_lru", "w_merge", "b_merge", "w_out",
    "norm_post", "w_ple", "w_ple_gate", "b_ple_gate")


def _in_proj(hb, w, seg):
    return _dot(hb, w["w_in"][:, seg * D_MODEL:(seg + 1) * D_MODEL])


def _group_scan(a, b, h_prev):
    rows, d = a.shape
    groups = rows // SUBLANES
    a3 = a.reshape(groups, SUBLANES, d)
    b3 = b.reshape(groups, SUBLANES, d)
    row = lax.broadcasted_iota(jnp.int32, a3.shape, 1)
    shift = 1
    while shift < SUBLANES:
        keep = row >= shift
        a_sh = jnp.where(keep, pltpu.roll(a3, shift, 1), 1.0)
        b_sh = jnp.where(keep, pltpu.roll(b3, shift, 1), 0.0)
        b3 = b3 + a3 * b_sh
        a3 = a3 * a_sh
        shift *= 2
    hs = []
    h = h_prev
    for g in range(groups):
        hg = b3[g] + a3[g] * h
        hs.append(hg)
        h = hg[SUBLANES - 1:SUBLANES, :]
    return jnp.concatenate(hs, axis=0), h


def _prompt_kernel(*refs):
    n_w = len(_WEIGHT_NAMES)
    x_ref, p_ref = refs[0], refs[1]
    w = dict(zip(_WEIGHT_NAMES, refs[2:2 + n_w]))
    sgw_ref, sgb_ref = refs[2 + n_w], refs[3 + n_w]
    y_ref, conv_out_ref, lru_out_ref = refs[4 + n_w:7 + n_w]
    xl_buf, conv_state, h_state = refs[7 + n_w:]

    t = pl.program_id(1)
    tile = x_ref.shape[0]

    @pl.when(t == 0)
    def _():
        conv_state[...] = jnp.zeros_like(conv_state)
        h_state[...] = jnp.zeros_like(h_state)

    x = x_ref[...]
    hb = (_rms_scale(x) * w["norm_pre"][...]).astype(_BF16)

    z = _head_norm(_in_proj(hb, w, 1), w["sg_norm"][...]).astype(_BF16)
    tri = (lax.broadcasted_iota(jnp.int32, (CHUNK, CHUNK), 0)
           >= lax.broadcasted_iota(jnp.int32, (CHUNK, CHUNK), 1))
    s_rows = []
    for c in range(tile // CHUNK):
        s_heads = []
        for h in range(SG_HEADS):
            wm = jnp.where(tri, sgw_ref[h], 0.0).astype(_BF16)
            zc = z[c * CHUNK:(c + 1) * CHUNK, h * SG_HEAD_DIM:(h + 1) * SG_HEAD_DIM]
            s_heads.append(_dot(wm, zc) + sgb_ref[:, h:h + 1])
        s_rows.append(jnp.concatenate(s_heads, axis=-1))
    s = jnp.concatenate(s_rows, axis=0)
    y_sg = _in_proj(hb, w, 0) * s * _silu(_in_proj(hb, w, 2))

    x_lru = _in_proj(hb, w, 3)
    xl_buf[0:SUBLANES, :] = conv_state[...]
    xl_buf[SUBLANES:, :] = x_lru
    conv_state[...] = xl_buf[tile:tile + SUBLANES, :]
    xc = w["conv_b"][...] + w["conv_w"][CONV_WIDTH - 1:CONV_WIDTH, :] * x_lru
    for j in range(1, CONV_WIDTH):
        k = CONV_WIDTH - 1 - j
        xc = xc + w["conv_w"][k:k + 1, :] * xl_buf[pl.ds(SUBLANES - j, tile), :]
    a, bterm = _lru_coeffs(xc, w["w_gate"], w["lru_ba"][...], w["lru_bx"][...],
                           w["lru_lambda"][...])
    hseq, h_last = _group_scan(a, bterm, h_state[...])
    h_state[...] = h_last
    y_lru = hseq * _silu(_in_proj(hb, w, 4))

    y_ref[...] = _post_mix(x, hb, y_sg, y_lru, p_ref[...], w)

    @pl.when(t == pl.num_programs(1) - 1)
    def _():
        conv_out_ref[...] = xl_buf[pl.ds(tile + SUBLANES - (CONV_WIDTH - 1), CONV_WIDTH - 1), :]
        lru_out_ref[...] = h_last


def _resident(shape):
    return pl.BlockSpec(shape, lambda *_: (0,) * len(shape), pipeline_mode=pl.Buffered(1))


def _prompt_call(x, p, weights, sgw, sgb_t):
    batch, seq, d = x.shape
    tile = PROMPT_TILE
    grid = (batch, seq // tile)
    in_specs = [
        pl.BlockSpec((None, tile, d), lambda b, t: (b, t, 0)),
        pl.BlockSpec((None, tile, PLE_DIM), lambda b, t: (b, t, 0)),
    ]
    in_specs += [_resident(wt.shape) for wt in weights]
    in_specs += [_resident(sgw.shape), _resident(sgb_t.shape)]
    out_shape = (
        jax.ShapeDtypeStruct((batch, seq, d), _F32),
        jax.ShapeDtypeStruct((batch, CONV_WIDTH - 1, d), _F32),
        jax.ShapeDtypeStruct((batch, 1, d), _F32),
    )
    out_specs = (
        pl.BlockSpec((None, tile, d), lambda b, t: (b, t, 0)),
        pl.BlockSpec((None, CONV_WIDTH - 1, d), lambda b, t: (b, 0, 0)),
        pl.BlockSpec((None, 1, d), lambda b, t: (b, 0, 0)),
    )
    return pl.pallas_call(
        _prompt_kernel,
        grid=grid,
        in_specs=in_specs,
        out_specs=out_specs,
        out_shape=out_shape,
        scratch_shapes=[
            pltpu.VMEM((tile + SUBLANES, d), _F32),
            pltpu.VMEM((SUBLANES, d), _F32),
            pltpu.VMEM((1, d), _F32),
        ],
        compiler_params=pltpu.CompilerParams(
            dimension_semantics=("arbitrary", "arbitrary"),
            vmem_limit_bytes=VMEM_LIMIT_BYTES),
        name="prompt_layer",
    )(x, p, *weights, sgw, sgb_t)


def _sample_kernel(*refs, steps):
    n_w = len(_WEIGHT_NAMES)
    x_ref, p_ref, conv_in_ref, h0_ref = refs[:4]
    w = dict(zip(_WEIGHT_NAMES, refs[4:4 + n_w]))
    sgw_ref, sgb_ref = refs[4 + n_w], refs[5 + n_w]
    y_ref, conv_out_ref, lru_out_ref, z_out_ref = refs[6 + n_w:]
    d = D_MODEL
    nb = x_ref.shape[0]

    def slab(v, t):
        return v[t * nb:(t + 1) * nb]

    x = jnp.concatenate([x_ref[:, t * d:(t + 1) * d] for t in range(steps)], axis=0)
    p = jnp.concatenate([p_ref[:, t * PLE_DIM:(t + 1) * PLE_DIM] for t in range(steps)], axis=0)
    hb = (_rms_scale(x) * w["norm_pre"][...]).astype(_BF16)

    z = _head_norm(_in_proj(hb, w, 1), w["sg_norm"][...])
    for t in range(steps):
        z_out_ref[:, t * d:(t + 1) * d] = slab(z, t)
    s_slabs = []
    for t in range(steps):
        acc = jnp.broadcast_to(sgb_ref[t:t + 1, :], (nb, d))
        for u in range(t + 1):
            acc = acc + sgw_ref[t * steps + u:t * steps + u + 1, :] * slab(z, u)
        s_slabs.append(acc)
    s = jnp.concatenate(s_slabs, axis=0)
    y_sg = _in_proj(hb, w, 0) * s * _silu(_in_proj(hb, w, 2))

    x_lru = _in_proj(hb, w, 3)
    hist = [conv_in_ref[:, k * d:(k + 1) * d] for k in range(CONV_WIDTH - 1)]
    hist += [slab(x_lru, t) for t in range(steps)]
    for k in range(CONV_WIDTH - 1):
        conv_out_ref[:, k * d:(k + 1) * d] = hist[steps + k]
    xc_slabs = []
    for t in range(steps):
        acc = w["conv_b"][...] + w["conv_w"][0:1, :] * hist[t]
        for k in range(1, CONV_WIDTH):
            acc = acc + w["conv_w"][k:k + 1, :] * hist[t + k]
        xc_slabs.append(acc)
    xc = jnp.concatenate(xc_slabs, axis=0)
    a, bterm = _lru_coeffs(xc, w["w_gate"], w["lru_ba"][...], w["lru_bx"][...],
                           w["lru_lambda"][...])
    h = h0_ref[...]
    h_slabs = []
    for t in range(steps):
        h = slab(a, t) * h + slab(bterm, t)
        h_slabs.append(h)
    lru_out_ref[...] = h
    y_lru = jnp.concatenate(h_slabs, axis=0) * _silu(_in_proj(hb, w, 4))

    y = _post_mix(x, hb, y_sg, y_lru, p, w)
    for t in range(steps):
        y_ref[:, t * d:(t + 1) * d] = slab(y, t)


def _sample_call(x2, p2, conv2, h0, weights, sgw_rows, sgb_rows, steps):
    nb = x2.shape[0]
    d = D_MODEL
    out_shape = (
        jax.ShapeDtypeStruct((nb, steps * d), _F32),
        jax.ShapeDtypeStruct((nb, (CONV_WIDTH - 1) * d), _F32),
        jax.ShapeDtypeStruct((nb, d), _F32),
        jax.ShapeDtypeStruct((nb, steps * d), _F32),
    )
    return pl.pallas_call(
        functools.partial(_sample_kernel, steps=steps),
        out_shape=out_shape,
        compiler_params=pltpu.CompilerParams(vmem_limit_bytes=VMEM_LIMIT_BYTES),
        name="sample_layer",
    )(x2, p2, conv2, h0, *weights, sgw_rows, sgb_rows)


def _layer_weights(l, norm_pre, w_in, sg_norm, conv_w, conv_b, lru_wa, lru_ba, lru_wx, lru_bx,
                   lru_lambda, w_branch_sg, w_branch_lru, w_merge, b_merge, w_out, norm_post,
                   w_ple, w_ple_gate, b_ple_gate):
    row = lambda v: v[l].reshape(1, -1)
    w_gate = jnp.concatenate([lru_wa[l], lru_wx[l]], axis=-1).astype(_BF16)
    by_name = dict(
        norm_pre=row(norm_pre), w_in=w_in[l].astype(_BF16), sg_norm=row(sg_norm),
        conv_w=conv_w[l], conv_b=row(conv_b), w_gate=w_gate, lru_ba=row(lru_ba),
        lru_bx=row(lru_bx), lru_lambda=row(lru_lambda),
        w_branch_sg=w_branch_sg[l].astype(_BF16), w_branch_lru=w_branch_lru[l].astype(_BF16),
        w_merge=w_merge[l].astype(_BF16), b_merge=row(b_merge), w_out=w_out[l].astype(_BF16),
        norm_post=row(norm_post), w_ple=w_ple[l].astype(_BF16),
        w_ple_gate=w_ple_gate[l].astype(_BF16), b_ple_gate=row(b_ple_gate))
    return [by_name[n] for n in _WEIGHT_NAMES]


def kernel(x_prompt, x_sample, p_prompt, p_sample, state_conv, state_lru, norm_pre, w_in, sg_norm,
           sg_w, sg_b, conv_w, conv_b, lru_wa, lru_ba, lru_wx, lru_bx, lru_lambda, w_branch_sg,
           w_branch_lru, w_merge, b_merge, w_out, norm_post, w_ple, w_ple_gate, b_ple_gate):
    depth = norm_pre.shape[0]
    nb, steps, d = x_sample.shape
    xp, xs = x_prompt, x_sample.reshape(nb, steps * d)
    conv_p, lru_p, conv_s, lru_s, chunk_s = [], [], [], [], []
    for l in range(depth):
        weights = _layer_weights(
            l, norm_pre, w_in, sg_norm, conv_w, conv_b, lru_wa, lru_ba, lru_wx, lru_bx,
            lru_lambda, w_branch_sg, w_branch_lru, w_merge, b_merge, w_out, norm_post, w_ple,
            w_ple_gate, b_ple_gate)
        xp, cp, hp = _prompt_call(xp, p_prompt[l], weights, sg_w[l], sg_b[l].T)
        conv_p.append(cp)
        lru_p.append(hp.reshape(-1, d))
        sgw_rows = jnp.repeat(sg_w[l][:, :steps, :steps].reshape(SG_HEADS, steps * steps).T,
                              SG_HEAD_DIM, axis=1)
        sgb_rows = jnp.repeat(sg_b[l][:, :steps].T, SG_HEAD_DIM, axis=1)
        xs, cs, hs, zs = _sample_call(
            xs, p_sample[l].reshape(nb, steps * PLE_DIM),
            state_conv[l].reshape(nb, (CONV_WIDTH - 1) * d), state_lru[l],
            weights, sgw_rows, sgb_rows, steps)
        conv_s.append(cs.reshape(nb, CONV_WIDTH - 1, d))
        lru_s.append(hs)
        chunk_s.append(zs.reshape(nb, steps, d))
    return (xp, xs.reshape(nb, steps, d), jnp.stack(conv_p), jnp.stack(lru_p),
            jnp.stack(conv_s), jnp.stack(lru_s), jnp.stack(chunk_s))
```

```python
import functools

import jax
import jax.numpy as jnp
from jax import lax
from jax.experimental import pallas as pl
from jax.experimental.pallas import tpu as pltpu

D_MODEL = 1024
PLE_DIM = 256
SG_HEADS = 4
SG_HEAD_DIM = D_MODEL // SG_HEADS
CHUNK = 128
LRU_BLOCKS = 8
LRU_BLOCK_DIM = D_MODEL // LRU_BLOCKS
CONV_WIDTH = 4
LRU_C = 8.0
EPS = 1e-6

SUBLANES = 8
PROMPT_TILE = 512
SUB_TILES = 1
VMEM_LIMIT_BYTES = 56 * 1024 * 1024

_BF16 = jnp.bfloat16
_F32 = jnp.float32

_WEIGHT_NAMES = (
    "norm_pre", "w_in", "sg_norm", "conv_w", "conv_b", "w_gate", "lru_ba", "lru_bx",
    "lru_lambda", "w_branch_sg", "w_branch_lru", "w_merge", "b_merge", "w_out",
    "norm_post", "w_ple", "w_ple_gate", "b_ple_gate")


def _dot(a, b):
    return jnp.dot(a, b, preferred_element_type=_F32)


def _rms_scale(x):
    var = jnp.mean(x * x, axis=-1, keepdims=True)
    return x * lax.rsqrt(var + EPS)


def _silu(x):
    return x * jax.nn.sigmoid(x)


def _in_proj(hb, w, seg):
    return _dot(hb, w["w_in"][:, seg * D_MODEL:(seg + 1) * D_MODEL])


def _head_norm(v, sgn):
    zs = []
    for h in range(SG_HEADS):
        sl = slice(h * SG_HEAD_DIM, (h + 1) * SG_HEAD_DIM)
        zs.append(_rms_scale(v[:, sl]) * sgn[:, sl])
    return jnp.concatenate(zs, axis=-1)


def _lru_gate_dots(xc, wg_ref):
    xcb = xc.astype(_BF16)
    rs, is_ = [], []
    for n in range(LRU_BLOCKS):
        sl = slice(n * LRU_BLOCK_DIM, (n + 1) * LRU_BLOCK_DIM)
        g = _dot(xcb[:, sl], wg_ref[n])
        rs.append(g[:, :LRU_BLOCK_DIM])
        is_.append(g[:, LRU_BLOCK_DIM:])
    return jnp.concatenate(rs, axis=-1), jnp.concatenate(is_, axis=-1)


def _lru_coeffs(xc, r_pre, i_pre, w):
    r = jax.nn.sigmoid(r_pre + w["lru_ba"][...])
    i = jax.nn.sigmoid(i_pre + w["lru_bx"][...])
    log_a = (-LRU_C * jax.nn.softplus(-w["lru_lambda"][...])) * r
    a = jnp.exp(log_a)
    mult = jnp.sqrt(1.0 - a * a)
    return a, mult * (i * xc)


def _merge_gate(pre, w, half):
    return jax.nn.sigmoid(pre + w["b_merge"][:, half * D_MODEL:(half + 1) * D_MODEL])


def _post_mix(x, ga_pre, gb_pre, acc_sg, acc_lru, p, w):
    merged = _merge_gate(ga_pre, w, 0) * acc_sg + _merge_gate(gb_pre, w, 1) * acc_lru
    o = _dot(merged.astype(_BF16), w["w_out"][...])
    x1 = x + _rms_scale(o) * w["norm_post"][...]
    gate = jax.nn.sigmoid(_dot(x1.astype(_BF16), w["w_ple_gate"][...]) + w["b_ple_gate"][...])
    return x1 + gate * _dot(p.astype(_BF16), w["w_ple"][...])


def _group_scan(a, b, h_prev):
    rows, d = a.shape
    groups = rows // SUBLANES
    a3 = a.reshape(groups, SUBLANES, d)
    b3 = b.reshape(groups, SUBLANES, d)
    row = lax.broadcasted_iota(jnp.int32, a3.shape, 1)
    shift = 1
    while shift < SUBLANES:
        keep = row >= shift
        a_sh = jnp.where(keep, pltpu.roll(a3, shift, 1), 1.0)
        b_sh = jnp.where(keep, pltpu.roll(b3, shift, 1), 0.0)
        b3 = b3 + a3 * b_sh
        a3 = a3 * a_sh
        shift *= 2
    hs = []
    h = h_prev
    for g in range(groups):
        hg = b3[g] + a3[g] * h
        hs.append(hg)
        h = hg[SUBLANES - 1:SUBLANES, :]
    return jnp.concatenate(hs, axis=0), h


def _spatial_mix(z, sgw_ref, sgb_ref):
    tri = (lax.broadcasted_iota(jnp.int32, (CHUNK, CHUNK), 0)
           >= lax.broadcasted_iota(jnp.int32, (CHUNK, CHUNK), 1))
    s_rows = []
    for c in range(z.shape[0] // CHUNK):
        s_heads = []
        for h in range(SG_HEADS):
            wm = jnp.where(tri, sgw_ref[h], 0.0).astype(_BF16)
            zc = z[c * CHUNK:(c + 1) * CHUNK, h * SG_HEAD_DIM:(h + 1) * SG_HEAD_DIM]
            s_heads.append(_dot(wm, zc) + sgb_ref[:, h:h + 1])
        s_rows.append(jnp.concatenate(s_heads, axis=-1))
    return jnp.concatenate(s_rows, axis=0)


def _sub_tile(x, p, xl, h_prev, w, sgw_ref, sgb_ref):
    rows = x.shape[0]
    hb = (_rms_scale(x) * w["norm_pre"][...]).astype(_BF16)

    z = _head_norm(_in_proj(hb, w, 1), w["sg_norm"][...]).astype(_BF16)
    s = _spatial_mix(z, sgw_ref, sgb_ref)
    y_sg = _in_proj(hb, w, 0) * s * _silu(_in_proj(hb, w, 2))

    x_lru = _in_proj(hb, w, 3)
    xl[SUBLANES:, :] = x_lru
    xc = w["conv_b"][...] + w["conv_w"][CONV_WIDTH - 1:CONV_WIDTH, :] * x_lru
    for j in range(1, CONV_WIDTH):
        k = CONV_WIDTH - 1 - j
        xc = xc + w["conv_w"][k:k + 1, :] * xl[pl.ds(SUBLANES - j, rows), :]
    r_pre, i_pre = _lru_gate_dots(xc, w["w_gate"])
    a, bterm = _lru_coeffs(xc, r_pre, i_pre, w)
    hseq, h_last = _group_scan(a, bterm, h_prev)
    y_lru = hseq * _silu(_in_proj(hb, w, 4))

    ga_pre = _dot(hb, w["w_merge"][:, :D_MODEL])
    acc_sg = _dot(y_sg.astype(_BF16), w["w_branch_sg"][...])
    gb_pre = _dot(hb, w["w_merge"][:, D_MODEL:])
    acc_lru = _dot(y_lru.astype(_BF16), w["w_branch_lru"][...])
    return _post_mix(x, ga_pre, gb_pre, acc_sg, acc_lru, p, w), h_last


def _prompt_kernel(*refs):
    n_w = len(_WEIGHT_NAMES)
    x_ref, p_ref = refs[0], refs[1]
    w = dict(zip(_WEIGHT_NAMES, refs[2:2 + n_w]))
    sgw_ref, sgb_ref = refs[2 + n_w], refs[3 + n_w]
    y_ref, conv_out_ref, lru_out_ref = refs[4 + n_w:7 + n_w]
    xl_buf, h_state = refs[7 + n_w:]

    t = pl.program_id(1)
    rows = x_ref.shape[0] // SUB_TILES
    pitch = SUBLANES + rows

    @pl.when(t == 0)
    def _():
        xl_buf[0:SUBLANES, :] = jnp.zeros((SUBLANES, D_MODEL), _F32)
        h_state[...] = jnp.zeros_like(h_state)

    h = h_state[...]
    for i in range(SUB_TILES):
        xl = xl_buf.at[pl.ds(i * pitch, pitch)]
        y, h = _sub_tile(x_ref[pl.ds(i * rows, rows), :], p_ref[pl.ds(i * rows, rows), :],
                         xl, h, w, sgw_ref, sgb_ref)
        y_ref[pl.ds(i * rows, rows), :] = y
        nxt = ((i + 1) % SUB_TILES) * pitch
        xl_buf[nxt:nxt + SUBLANES, :] = xl[rows:rows + SUBLANES, :]
    h_state[...] = h

    @pl.when(t == pl.num_programs(1) - 1)
    def _():
        last = SUB_TILES * pitch - (CONV_WIDTH - 1)
        conv_out_ref[...] = xl_buf[pl.ds(last, CONV_WIDTH - 1), :]
        lru_out_ref[...] = h


def _resident(shape):
    return pl.BlockSpec(shape, lambda *_: (0,) * len(shape), pipeline_mode=pl.Buffered(1))


def _prompt_call(x, p, weights, sgw, sgb_t):
    batch, seq, d = x.shape
    tile = PROMPT_TILE
    grid = (batch, seq // tile)
    in_specs = [
        pl.BlockSpec((None, tile, d), lambda b, t: (b, t, 0)),
        pl.BlockSpec((None, tile, PLE_DIM), lambda b, t: (b, t, 0)),
    ]
    in_specs += [_resident(wt.shape) for wt in weights]
    in_specs += [_resident(sgw.shape), _resident(sgb_t.shape)]
    out_shape = (
        jax.ShapeDtypeStruct((batch, seq, d), _F32),
        jax.ShapeDtypeStruct((batch, CONV_WIDTH - 1, d), _F32),
        jax.ShapeDtypeStruct((batch, 1, d), _F32),
    )
    out_specs = (
        pl.BlockSpec((None, tile, d), lambda b, t: (b, t, 0)),
        pl.BlockSpec((None, CONV_WIDTH - 1, d), lambda b, t: (b, 0, 0)),
        pl.BlockSpec((None, 1, d), lambda b, t: (b, 0, 0)),
    )
    return pl.pallas_call(
        _prompt_kernel,
        grid=grid,
        in_specs=in_specs,
        out_specs=out_specs,
        out_shape=out_shape,
        scratch_shapes=[
            pltpu.VMEM((SUB_TILES * (SUBLANES + tile // SUB_TILES), d), _F32),
            pltpu.VMEM((1, d), _F32),
        ],
        compiler_params=pltpu.CompilerParams(
            dimension_semantics=("arbitrary", "arbitrary"),
            vmem_limit_bytes=VMEM_LIMIT_BYTES),
        name="prompt_layer",
    )(x, p, *weights, sgw, sgb_t)


def _sample_kernel(*refs, steps):
    n_w = len(_WEIGHT_NAMES)
    x_ref, p_ref, conv_in_ref, h0_ref = refs[:4]
    w = dict(zip(_WEIGHT_NAMES, refs[4:4 + n_w]))
    sgw_ref, sgb_ref = refs[4 + n_w], refs[5 + n_w]
    y_ref, conv_out_ref, lru_out_ref, z_out_ref = refs[6 + n_w:]
    d = D_MODEL
    nb = x_ref.shape[0]

    def slab(v, t):
        return v[t * nb:(t + 1) * nb]

    x = jnp.concatenate([x_ref[:, t * d:(t + 1) * d] for t in range(steps)], axis=0)
    p = jnp.concatenate([p_ref[:, t * PLE_DIM:(t + 1) * PLE_DIM] for t in range(steps)], axis=0)
    hb = (_rms_scale(x) * w["norm_pre"][...]).astype(_BF16)

    z = _head_norm(_in_proj(hb, w, 1), w["sg_norm"][...])
    for t in range(steps):
        z_out_ref[:, t * d:(t + 1) * d] = slab(z, t)
    s_slabs = []
    for t in range(steps):
        acc = jnp.broadcast_to(sgb_ref[t:t + 1, :], (nb, d))
        for u in range(t + 1):
            acc = acc + sgw_ref[t * steps + u:t * steps + u + 1, :] * slab(z, u)
        s_slabs.append(acc)
    s = jnp.concatenate(s_slabs, axis=0)
    y_sg = _in_proj(hb, w, 0) * s * _silu(_in_proj(hb, w, 2))

    x_lru = _in_proj(hb, w, 3)
    hist = [conv_in_ref[:, k * d:(k + 1) * d] for k in range(CONV_WIDTH - 1)]
    hist += [slab(x_lru, t) for t in range(steps)]
    for k in range(CONV_WIDTH - 1):
        conv_out_ref[:, k * d:(k + 1) * d] = hist[steps + k]
    xc_slabs = []
    for t in range(steps):
        acc = w["conv_b"][...] + w["conv_w"][0:1, :] * hist[t]
        for k in range(1, CONV_WIDTH):
            acc = acc + w["conv_w"][k:k + 1, :] * hist[t + k]
        xc_slabs.append(acc)
    xc = jnp.concatenate(xc_slabs, axis=0)
    r_pre, i_pre = _lru_gate_dots(xc, w["w_gate"])
    a, bterm = _lru_coeffs(xc, r_pre, i_pre, w)
    h = h0_ref[...]
    h_slabs = []
    for t in range(steps):
        h = slab(a, t) * h + slab(bterm, t)
        h_slabs.append(h)
    lru_out_ref[...] = h
    y_lru = jnp.concatenate(h_slabs, axis=0) * _silu(_in_proj(hb, w, 4))

    y = _post_mix(x, _dot(hb, w["w_merge"][:, :D_MODEL]), _dot(hb, w["w_merge"][:, D_MODEL:]),
                  _dot(y_sg.astype(_BF16), w["w_branch_sg"][...]),
                  _dot(y_lru.astype(_BF16), w["w_branch_lru"][...]), p, w)
    for t in range(steps):
        y_ref[:, t * d:(t + 1) * d] = slab(y, t)


def _sample_call(x2, p2, conv2, h0, weights, sgw_rows, sgb_rows, steps):
    nb = x2.shape[0]
    d = D_MODEL
    out_shape = (
        jax.ShapeDtypeStruct((nb, steps * d), _F32),
        jax.ShapeDtypeStruct((nb, (CONV_WIDTH - 1) * d), _F32),
        jax.ShapeDtypeStruct((nb, d), _F32),
        jax.ShapeDtypeStruct((nb, steps * d), _F32),
    )
    return pl.pallas_call(
        functools.partial(_sample_kernel, steps=steps),
        out_shape=out_shape,
        compiler_params=pltpu.CompilerParams(vmem_limit_bytes=VMEM_LIMIT_BYTES),
        name="sample_layer",
    )(x2, p2, conv2, h0, *weights, sgw_rows, sgb_rows)


def _layer_weights(l, norm_pre, w_in, sg_norm, conv_w, conv_b, lru_wa, lru_ba, lru_wx, lru_bx,
                   lru_lambda, w_branch_sg, w_branch_lru, w_merge, b_merge, w_out, norm_post,
                   w_ple, w_ple_gate, b_ple_gate):
    row = lambda v: v[l].reshape(1, -1)
    w_gate = jnp.concatenate([lru_wa[l], lru_wx[l]], axis=-1).astype(_BF16)
    by_name = dict(
        norm_pre=row(norm_pre), w_in=w_in[l].astype(_BF16), sg_norm=row(sg_norm),
        conv_w=conv_w[l], conv_b=row(conv_b), w_gate=w_gate, lru_ba=row(lru_ba),
        lru_bx=row(lru_bx), lru_lambda=row(lru_lambda),
        w_branch_sg=w_branch_sg[l].astype(_BF16), w_branch_lru=w_branch_lru[l].astype(_BF16),
        w_merge=w_merge[l].astype(_BF16), b_merge=row(b_merge), w_out=w_out[l].astype(_BF16),
        norm_post=row(norm_post), w_ple=w_ple[l].astype(_BF16),
        w_ple_gate=w_ple_gate[l].astype(_BF16), b_ple_gate=row(b_ple_gate))
    return [by_name[n] for n in _WEIGHT_NAMES]


def kernel(x_prompt, x_sample, p_prompt, p_sample, state_conv, state_lru, norm_pre, w_in, sg_norm,
           sg_w, sg_b, conv_w, conv_b, lru_wa, lru_ba, lru_wx, lru_bx, lru_lambda, w_branch_sg,
           w_branch_lru, w_merge, b_merge, w_out, norm_post, w_ple, w_ple_gate, b_ple_gate):
    depth = norm_pre.shape[0]
    nb, steps, d = x_sample.shape
    xp, xs = x_prompt, x_sample.reshape(nb, steps * d)
    conv_p, lru_p, conv_s, lru_s, chunk_s = [], [], [], [], []
    for l in range(depth):
        weights = _layer_weights(
            l, norm_pre, w_in, sg_norm, conv_w, conv_b, lru_wa, lru_ba, lru_wx, lru_bx,
            lru_lambda, w_branch_sg, w_branch_lru, w_merge, b_merge, w_out, norm_post, w_ple,
            w_ple_gate, b_ple_gate)
        xp, cp, hp = _prompt_call(xp, p_prompt[l], weights, sg_w[l], sg_b[l].T)
        conv_p.append(cp)
        lru_p.append(hp.reshape(-1, d))
        sgw_rows = jnp.repeat(sg_w[l][:, :steps, :steps].reshape(SG_HEADS, steps * steps).T,
                              SG_HEAD_DIM, axis=1)
        sgb_rows = jnp.repeat(sg_b[l][:, :steps].T, SG_HEAD_DIM, axis=1)
        xs, cs, hs, zs = _sample_call(
            xs, p_sample[l].reshape(nb, steps * PLE_DIM),
            state_conv[l].reshape(nb, (CONV_WIDTH - 1) * d), state_lru[l],
            weights, sgw_rows, sgb_rows, steps)
        conv_s.append(cs.reshape(nb, CONV_WIDTH - 1, d))
        lru_s.append(hs)
        chunk_s.append(zs.reshape(nb, steps, d))
    return (xp, xs.reshape(nb, steps, d), jnp.stack(conv_p), jnp.stack(lru_p),
            jnp.stack(conv_s), jnp.stack(lru_s), jnp.stack(chunk_s))
```

```python
import functools

import jax
import jax.numpy as jnp
from jax import lax
from jax.experimental import pallas as pl
from jax.experimental.pallas import tpu as pltpu

D_MODEL = 1024
PLE_DIM = 256
SG_HEADS = 4
SG_HEAD_DIM = D_MODEL // SG_HEADS
CHUNK = 128
LRU_BLOCKS = 8
LRU_BLOCK_DIM = D_MODEL // LRU_BLOCKS
CONV_WIDTH = 4
LRU_C = 8.0
EPS = 1e-6

SUBLANES = 8
PROMPT_TILE = 256
VMEM_LIMIT_BYTES = 56 * 1024 * 1024

_BF16 = jnp.bfloat16
_F32 = jnp.float32

_WEIGHT_NAMES = (
    "norm_pre", "w_in", "sg_norm", "conv_w", "conv_b", "w_gate", "lru_ba", "lru_bx",
    "lru_lambda", "w_branch_sg", "w_branch_lru", "w_merge", "b_merge", "w_out",
    "norm_post", "w_ple", "w_ple_gate", "b_ple_gate")

COL_U, COL_V, COL_G_SG, COL_X_LRU, COL_G_LRU = range(5)


def _dot(a, b):
    return jnp.dot(a, b, preferred_element_type=_F32)


def _rms_scale(x):
    var = jnp.mean(x * x, axis=-1, keepdims=True)
    return x * lax.rsqrt(var + EPS)


def _silu(x):
    return x * jax.nn.sigmoid(x)


def _in_proj(hb, w, col):
    return _dot(hb, w["w_in"][:, col * D_MODEL:(col + 1) * D_MODEL])


def _merge_pre(hb, w, half):
    return _dot(hb, w["w_merge"][:, half * D_MODEL:(half + 1) * D_MODEL])


def _head_norm(v, sgn):
    zs = []
    for h in range(SG_HEADS):
        sl = slice(h * SG_HEAD_DIM, (h + 1) * SG_HEAD_DIM)
        zs.append(_rms_scale(v[:, sl]) * sgn[:, sl])
    return jnp.concatenate(zs, axis=-1)


def _lru_gate_dot(xcb, wg_ref, n):
    return _dot(xcb[:, n * LRU_BLOCK_DIM:(n + 1) * LRU_BLOCK_DIM], wg_ref[n])


def _split_gates(gates):
    r_pre = jnp.concatenate([g[:, :LRU_BLOCK_DIM] for g in gates], axis=-1)
    i_pre = jnp.concatenate([g[:, LRU_BLOCK_DIM:] for g in gates], axis=-1)
    return r_pre, i_pre


def _lru_coeffs(xc, r_pre, i_pre, w):
    r = jax.nn.sigmoid(r_pre + w["lru_ba"][...])
    i = jax.nn.sigmoid(i_pre + w["lru_bx"][...])
    log_a = (-LRU_C * jax.nn.softplus(-w["lru_lambda"][...])) * r
    a = jnp.exp(log_a)
    mult = jnp.sqrt(1.0 - a * a)
    return a, mult * (i * xc)


def _merge_gate(pre, w, half):
    return jax.nn.sigmoid(pre + w["b_merge"][:, half * D_MODEL:(half + 1) * D_MODEL])


def _post_mix(x, ga_pre, gb_pre, acc_sg, acc_lru, p, w):
    merged = _merge_gate(ga_pre, w, 0) * acc_sg + _merge_gate(gb_pre, w, 1) * acc_lru
    o = _dot(merged.astype(_BF16), w["w_out"][...])
    x1 = x + _rms_scale(o) * w["norm_post"][...]
    gate = jax.nn.sigmoid(_dot(x1.astype(_BF16), w["w_ple_gate"][...]) + w["b_ple_gate"][...])
    return x1 + gate * _dot(p.astype(_BF16), w["w_ple"][...])


def _group_scan(a, b, h_prev):
    rows, d = a.shape
    groups = rows // SUBLANES
    a3 = a.reshape(groups, SUBLANES, d)
    b3 = b.reshape(groups, SUBLANES, d)
    row = lax.broadcasted_iota(jnp.int32, a3.shape, 1)
    shift = 1
    while shift < SUBLANES:
        keep = row >= shift
        a_sh = jnp.where(keep, pltpu.roll(a3, shift, 1), 1.0)
        b_sh = jnp.where(keep, pltpu.roll(b3, shift, 1), 0.0)
        b3 = b3 + a3 * b_sh
        a3 = a3 * a_sh
        shift *= 2
    hs = []
    h = h_prev
    for g in range(groups):
        hg = b3[g] + a3[g] * h
        hs.append(hg)
        h = hg[SUBLANES - 1:SUBLANES, :]
    return jnp.concatenate(hs, axis=0), h


def _spatial_mix(z, sgw_ref, sgb_ref):
    tri = (lax.broadcasted_iota(jnp.int32, (CHUNK, CHUNK), 0)
           >= lax.broadcasted_iota(jnp.int32, (CHUNK, CHUNK), 1))
    s_rows = []
    for c in range(z.shape[0] // CHUNK):
        s_heads = []
        for h in range(SG_HEADS):
            wm = jnp.where(tri, sgw_ref[h], 0.0).astype(_BF16)
            zc = z[c * CHUNK:(c + 1) * CHUNK, h * SG_HEAD_DIM:(h + 1) * SG_HEAD_DIM]
            s_heads.append(_dot(wm, zc) + sgb_ref[:, h:h + 1])
        s_rows.append(jnp.concatenate(s_heads, axis=-1))
    return jnp.concatenate(s_rows, axis=0)


def _prompt_kernel(*refs):
    n_w = len(_WEIGHT_NAMES)
    x_ref, p_ref = refs[0], refs[1]
    w = dict(zip(_WEIGHT_NAMES, refs[2:2 + n_w]))
    sgw_ref, sgb_ref = refs[2 + n_w], refs[3 + n_w]
    y_ref, conv_out_ref, lru_out_ref = refs[4 + n_w:7 + n_w]
    xl_buf, conv_state, h_state = refs[7 + n_w:]

    t = pl.program_id(1)
    tile = x_ref.shape[0]

    @pl.when(t == 0)
    def _():
        conv_state[...] = jnp.zeros_like(conv_state)
        h_state[...] = jnp.zeros_like(h_state)

    x = x_ref[...]
    hb = (_rms_scale(x) * w["norm_pre"][...]).astype(_BF16)

    z = _head_norm(_in_proj(hb, w, COL_V), w["sg_norm"][...]).astype(_BF16)
    s = _spatial_mix(z, sgw_ref, sgb_ref)
    y_sg = _in_proj(hb, w, COL_U) * s * _silu(_in_proj(hb, w, COL_G_SG))

    x_lru = _in_proj(hb, w, COL_X_LRU)
    xl_buf[0:SUBLANES, :] = conv_state[...]
    xl_buf[SUBLANES:, :] = x_lru
    conv_state[...] = xl_buf[tile:tile + SUBLANES, :]
    xc = w["conv_b"][...] + w["conv_w"][CONV_WIDTH - 1:CONV_WIDTH, :] * x_lru
    for j in range(1, CONV_WIDTH):
        k = CONV_WIDTH - 1 - j
        xc = xc + w["conv_w"][k:k + 1, :] * xl_buf[pl.ds(SUBLANES - j, tile), :]
    xcb = xc.astype(_BF16)
    r_pre, i_pre = _split_gates([_lru_gate_dot(xcb, w["w_gate"], n) for n in range(LRU_BLOCKS)])

    a, bterm = _lru_coeffs(xc, r_pre, i_pre, w)
    hseq, h_last = _group_scan(a, bterm, h_state[...])
    h_state[...] = h_last
    y_lru = hseq * _silu(_in_proj(hb, w, COL_G_LRU))

    ga_pre = _merge_pre(hb, w, 0)
    acc_sg = _dot(y_sg.astype(_BF16), w["w_branch_sg"][...])
    gb_pre = _merge_pre(hb, w, 1)
    acc_lru = _dot(y_lru.astype(_BF16), w["w_branch_lru"][...])
    y_ref[...] = _post_mix(x, ga_pre, gb_pre, acc_sg, acc_lru, p_ref[...], w)

    @pl.when(t == pl.num_programs(1) - 1)
    def _():
        conv_out_ref[...] = xl_buf[pl.ds(tile + SUBLANES - (CONV_WIDTH - 1), CONV_WIDTH - 1), :]
        lru_out_ref[...] = h_last


def _resident(shape):
    return pl.BlockSpec(shape, lambda *_: (0,) * len(shape), pipeline_mode=pl.Buffered(1))


def _prompt_call(x, p, weights, sgw, sgb_t):
    batch, seq, d = x.shape
    tile = PROMPT_TILE
    grid = (batch, seq // tile)
    in_specs = [
        pl.BlockSpec((None, tile, d), lambda b, t: (b, t, 0)),
        pl.BlockSpec((None, tile, PLE_DIM), lambda b, t: (b, t, 0)),
    ]
    in_specs += [_resident(wt.shape) for wt in weights]
    in_specs += [_resident(sgw.shape), _resident(sgb_t.shape)]
    out_shape = (
        jax.ShapeDtypeStruct((batch, seq, d), _F32),
        jax.ShapeDtypeStruct((batch, CONV_WIDTH - 1, d), _F32),
        jax.ShapeDtypeStruct((batch, 1, d), _F32),
    )
    out_specs = (
        pl.BlockSpec((None, tile, d), lambda b, t: (b, t, 0)),
        pl.BlockSpec((None, CONV_WIDTH - 1, d), lambda b, t: (b, 0, 0)),
        pl.BlockSpec((None, 1, d), lambda b, t: (b, 0, 0)),
    )
    return pl.pallas_call(
        _prompt_kernel,
        grid=grid,
        in_specs=in_specs,
        out_specs=out_specs,
        out_shape=out_shape,
        scratch_shapes=[
            pltpu.VMEM((tile + SUBLANES, d), _F32),
            pltpu.VMEM((SUBLANES, d), _F32),
            pltpu.VMEM((1, d), _F32),
        ],
        compiler_params=pltpu.CompilerParams(
            dimension_semantics=("arbitrary", "arbitrary"),
            vmem_limit_bytes=VMEM_LIMIT_BYTES),
        name="prompt_layer",
    )(x, p, *weights, sgw, sgb_t)


def _sample_kernel(*refs, steps):
    n_w = len(_WEIGHT_NAMES)
    x_ref, p_ref, conv_in_ref, h0_ref = refs[:4]
    w = dict(zip(_WEIGHT_NAMES, refs[4:4 + n_w]))
    sgw_ref, sgb_ref = refs[4 + n_w], refs[5 + n_w]
    y_ref, conv_out_ref, lru_out_ref, z_out_ref = refs[6 + n_w:]
    d = D_MODEL
    nb = x_ref.shape[0]

    def slab(v, t):
        return v[t * nb:(t + 1) * nb]

    x = jnp.concatenate([x_ref[:, t, :] for t in range(steps)], axis=0)
    p = jnp.concatenate([p_ref[:, t, :] for t in range(steps)], axis=0)
    hb = (_rms_scale(x) * w["norm_pre"][...]).astype(_BF16)

    z = _head_norm(_in_proj(hb, w, COL_V), w["sg_norm"][...])
    for t in range(steps):
        z_out_ref[:, t, :] = slab(z, t)
    s_slabs = []
    for t in range(steps):
        acc = jnp.broadcast_to(sgb_ref[t:t + 1, :], (nb, d))
        for u in range(t + 1):
            acc = acc + sgw_ref[t * steps + u:t * steps + u + 1, :] * slab(z, u)
        s_slabs.append(acc)
    s = jnp.concatenate(s_slabs, axis=0)
    y_sg = _in_proj(hb, w, COL_U) * s * _silu(_in_proj(hb, w, COL_G_SG))

    x_lru = _in_proj(hb, w, COL_X_LRU)
    hist = [conv_in_ref[:, k, :] for k in range(CONV_WIDTH - 1)]
    hist += [slab(x_lru, t) for t in range(steps)]
    for k in range(CONV_WIDTH - 1):
        conv_out_ref[:, k, :] = hist[steps + k]
    xc_slabs = []
    for t in range(steps):
        acc = w["conv_b"][...] + w["conv_w"][0:1, :] * hist[t]
        for k in range(1, CONV_WIDTH):
            acc = acc + w["conv_w"][k:k + 1, :] * hist[t + k]
        xc_slabs.append(acc)
    xc = jnp.concatenate(xc_slabs, axis=0)
    xcb = xc.astype(_BF16)
    r_pre, i_pre = _split_gates([_lru_gate_dot(xcb, w["w_gate"], n) for n in range(LRU_BLOCKS)])
    a, bterm = _lru_coeffs(xc, r_pre, i_pre, w)
    h = h0_ref[...]
    h_slabs = []
    for t in range(steps):
        h = slab(a, t) * h + slab(bterm, t)
        h_slabs.append(h)
    lru_out_ref[...] = h
    y_lru = jnp.concatenate(h_slabs, axis=0) * _silu(_in_proj(hb, w, COL_G_LRU))

    y = _post_mix(x, _merge_pre(hb, w, 0), _merge_pre(hb, w, 1),
                  _dot(y_sg.astype(_BF16), w["w_branch_sg"][...]),
                  _dot(y_lru.astype(_BF16), w["w_branch_lru"][...]), p, w)
    for t in range(steps):
        y_ref[:, t, :] = slab(y, t)


def _sample_call(x, p, conv_in, h0, weights, sgw_rows, sgb_rows):
    nb, steps, d = x.shape
    out_shape = (
        jax.ShapeDtypeStruct((nb, steps, d), _F32),
        jax.ShapeDtypeStruct((nb, CONV_WIDTH - 1, d), _F32),
        jax.ShapeDtypeStruct((nb, d), _F32),
        jax.ShapeDtypeStruct((nb, steps, d), _F32),
    )
    return pl.pallas_call(
        functools.partial(_sample_kernel, steps=steps),
        out_shape=out_shape,
        compiler_params=pltpu.CompilerParams(vmem_limit_bytes=VMEM_LIMIT_BYTES),
        name="sample_layer",
    )(x, p, conv_in, h0, *weights, sgw_rows, sgb_rows)


def _layer_weights(l, norm_pre, w_in, sg_norm, conv_w, conv_b, lru_wa, lru_ba, lru_wx, lru_bx,
                   lru_lambda, w_branch_sg, w_branch_lru, w_merge, b_merge, w_out, norm_post,
                   w_ple, w_ple_gate, b_ple_gate):
    row = lambda v: v[l].reshape(1, -1)
    w_gate = jnp.concatenate([lru_wa[l], lru_wx[l]], axis=-1).astype(_BF16)
    by_name = dict(
        norm_pre=row(norm_pre), w_in=w_in[l].astype(_BF16), sg_norm=row(sg_norm),
        conv_w=conv_w[l], conv_b=row(conv_b), w_gate=w_gate, lru_ba=row(lru_ba),
        lru_bx=row(lru_bx), lru_lambda=row(lru_lambda),
        w_branch_sg=w_branch_sg[l].astype(_BF16), w_branch_lru=w_branch_lru[l].astype(_BF16),
        w_merge=w_merge[l].astype(_BF16), b_merge=row(b_merge), w_out=w_out[l].astype(_BF16),
        norm_post=row(norm_post), w_ple=w_ple[l].astype(_BF16),
        w_ple_gate=w_ple_gate[l].astype(_BF16), b_ple_gate=row(b_ple_gate))
    return [by_name[n] for n in _WEIGHT_NAMES]


def kernel(x_prompt, x_sample, p_prompt, p_sample, state_conv, state_lru, norm_pre, w_in, sg_norm,
           sg_w, sg_b, conv_w, conv_b, lru_wa, lru_ba, lru_wx, lru_bx, lru_lambda, w_branch_sg,
           w_branch_lru, w_merge, b_merge, w_out, norm_post, w_ple, w_ple_gate, b_ple_gate):
    depth = norm_pre.shape[0]
    nb, steps, d = x_sample.shape
    xp, xs = x_prompt, x_sample
    conv_p, lru_p, conv_s, lru_s, chunk_s = [], [], [], [], []
    for l in range(depth):
        weights = _layer_weights(
            l, norm_pre, w_in, sg_norm, conv_w, conv_b, lru_wa, lru_ba, lru_wx, lru_bx,
            lru_lambda, w_branch_sg, w_branch_lru, w_merge, b_merge, w_out, norm_post, w_ple,
            w_ple_gate, b_ple_gate)
        xp, cp, hp = _prompt_call(xp, p_prompt[l], weights, sg_w[l], sg_b[l].T)
        conv_p.append(cp)
        lru_p.append(hp.reshape(-1, d))
        sgw_rows = jnp.repeat(sg_w[l][:, :steps, :steps].reshape(SG_HEADS, steps * steps).T,
                              SG_HEAD_DIM, axis=1)
        sgb_rows = jnp.repeat(sg_b[l][:, :steps].T, SG_HEAD_DIM, axis=1)
        xs, cs, hs, zs = _sample_call(
            xs, p_sample[l], state_conv[l], state_lru[l], weights, sgw_rows, sgb_rows)
        conv_s.append(cs)
        lru_s.append(hs)
        chunk_s.append(zs)
    return (xp, xs, jnp.stack(conv_p), jnp.stack(lru_p),
            jnp.stack(conv_s), jnp.stack(lru_s), jnp.stack(chunk_s))
```

```python
import functools

import jax
import jax.numpy as jnp
from jax import lax
from jax.experimental import pallas as pl
from jax.experimental.pallas import tpu as pltpu

D_MODEL = 1024
PLE_DIM = 256
SG_HEADS = 4
SG_HEAD_DIM = D_MODEL // SG_HEADS
CHUNK = 128
LRU_BLOCKS = 8
LRU_BLOCK_DIM = D_MODEL // LRU_BLOCKS
CONV_WIDTH = 4
LRU_C = 8.0
EPS = 1e-6

SUBLANES = 8
PROMPT_TILE = 256
VMEM_LIMIT_BYTES = 56 * 1024 * 1024

_BF16 = jnp.bfloat16
_F32 = jnp.float32

_WEIGHT_NAMES = (
    "norm_pre", "w_in", "sg_norm", "conv_w", "conv_b", "w_gate", "lru_ba", "lru_bx",
    "lru_lambda", "w_branch_sg", "w_branch_lru", "w_merge", "b_merge", "w_out",
    "norm_post", "w_ple", "w_ple_gate", "b_ple_gate")

COL_U, COL_V, COL_G_SG, COL_X_LRU, COL_G_LRU = range(5)


def _dot(a, b):
    return jnp.dot(a, b, preferred_element_type=_F32)


def _rms_scale(x):
    var = jnp.mean(x * x, axis=-1, keepdims=True)
    return x * lax.rsqrt(var + EPS)


def _silu(x):
    return x * jax.nn.sigmoid(x)


def _in_proj(hb, w, col):
    return _dot(hb, w["w_in"][:, col * D_MODEL:(col + 1) * D_MODEL])


def _merge_pre(hb, w, half):
    return _dot(hb, w["w_merge"][:, half * D_MODEL:(half + 1) * D_MODEL])


def _head_norm(v, sgn):
    zs = []
    for h in range(SG_HEADS):
        sl = slice(h * SG_HEAD_DIM, (h + 1) * SG_HEAD_DIM)
        zs.append(_rms_scale(v[:, sl]) * sgn[:, sl])
    return jnp.concatenate(zs, axis=-1)


def _lru_gate_dot(xcb, wg_ref, n):
    return _dot(xcb[:, n * LRU_BLOCK_DIM:(n + 1) * LRU_BLOCK_DIM], wg_ref[n])


def _split_gates(gates):
    r_pre = jnp.concatenate([g[:, :LRU_BLOCK_DIM] for g in gates], axis=-1)
    i_pre = jnp.concatenate([g[:, LRU_BLOCK_DIM:] for g in gates], axis=-1)
    return r_pre, i_pre


def _lru_coeffs(xc, r_pre, i_pre, w):
    r = jax.nn.sigmoid(r_pre + w["lru_ba"][...])
    i = jax.nn.sigmoid(i_pre + w["lru_bx"][...])
    log_a = (-LRU_C * jax.nn.softplus(-w["lru_lambda"][...])) * r
    a = jnp.exp(log_a)
    mult = jnp.sqrt(1.0 - a * a)
    return a, mult * (i * xc)


def _merge_gate(pre, w, half):
    return jax.nn.sigmoid(pre + w["b_merge"][:, half * D_MODEL:(half + 1) * D_MODEL])


def _post_mix(x, hb, y_sg, y_lru, p, w):
    g_a = _merge_gate(_merge_pre(hb, w, 0), w, 0)
    merged = g_a * _dot(y_sg.astype(_BF16), w["w_branch_sg"][...])
    g_b = _merge_gate(_merge_pre(hb, w, 1), w, 1)
    merged = merged + g_b * _dot(y_lru.astype(_BF16), w["w_branch_lru"][...])
    o = _dot(merged.astype(_BF16), w["w_out"][...])
    x1 = x + _rms_scale(o) * w["norm_post"][...]
    gate = jax.nn.sigmoid(_dot(x1.astype(_BF16), w["w_ple_gate"][...]) + w["b_ple_gate"][...])
    return x1 + gate * _dot(p.astype(_BF16), w["w_ple"][...])


def _group_scan(a, b, h_prev):
    rows, d = a.shape
    groups = rows // SUBLANES
    a3 = a.reshape(groups, SUBLANES, d)
    b3 = b.reshape(groups, SUBLANES, d)
    row = lax.broadcasted_iota(jnp.int32, a3.shape, 1)
    shift = 1
    while shift < SUBLANES:
        keep = row >= shift
        a_sh = jnp.where(keep, pltpu.roll(a3, shift, 1), 1.0)
        b_sh = jnp.where(keep, pltpu.roll(b3, shift, 1), 0.0)
        b3 = b3 + a3 * b_sh
        a3 = a3 * a_sh
        shift *= 2
    hs = []
    h = h_prev
    for g in range(groups):
        hg = b3[g] + a3[g] * h
        hs.append(hg)
        h = hg[SUBLANES - 1:SUBLANES, :]
    return jnp.concatenate(hs, axis=0), h


def _spatial_mix(z, sgw_ref, sgb_ref):
    tri = (lax.broadcasted_iota(jnp.int32, (CHUNK, CHUNK), 0)
           >= lax.broadcasted_iota(jnp.int32, (CHUNK, CHUNK), 1))
    s_rows = []
    for c in range(z.shape[0] // CHUNK):
        s_heads = []
        for h in range(SG_HEADS):
            wm = jnp.where(tri, sgw_ref[h], 0.0).astype(_BF16)
            zc = z[c * CHUNK:(c + 1) * CHUNK, h * SG_HEAD_DIM:(h + 1) * SG_HEAD_DIM]
            s_heads.append(_dot(wm, zc) + sgb_ref[:, h:h + 1])
        s_rows.append(jnp.concatenate(s_heads, axis=-1))
    return jnp.concatenate(s_rows, axis=0)


def _prompt_kernel(*refs):
    n_w = len(_WEIGHT_NAMES)
    x_ref, p_ref = refs[0], refs[1]
    w = dict(zip(_WEIGHT_NAMES, refs[2:2 + n_w]))
    sgw_ref, sgb_ref = refs[2 + n_w], refs[3 + n_w]
    y_ref, conv_out_ref, lru_out_ref = refs[4 + n_w:7 + n_w]
    xl_buf, conv_state, h_state = refs[7 + n_w:]

    t = pl.program_id(1)
    tile = x_ref.shape[0]

    @pl.when(t == 0)
    def _():
        conv_state[...] = jnp.zeros_like(conv_state)
        h_state[...] = jnp.zeros_like(h_state)

    x = x_ref[...]
    hb = (_rms_scale(x) * w["norm_pre"][...]).astype(_BF16)

    z = _head_norm(_in_proj(hb, w, COL_V), w["sg_norm"][...]).astype(_BF16)
    s = _spatial_mix(z, sgw_ref, sgb_ref)
    y_sg = _in_proj(hb, w, COL_U) * s * _silu(_in_proj(hb, w, COL_G_SG))

    x_lru = _in_proj(hb, w, COL_X_LRU)
    xl_buf[0:SUBLANES, :] = conv_state[...]
    xl_buf[SUBLANES:, :] = x_lru
    conv_state[...] = xl_buf[tile:tile + SUBLANES, :]
    xc = w["conv_b"][...] + w["conv_w"][CONV_WIDTH - 1:CONV_WIDTH, :] * x_lru
    for j in range(1, CONV_WIDTH):
        k = CONV_WIDTH - 1 - j
        xc = xc + w["conv_w"][k:k + 1, :] * xl_buf[pl.ds(SUBLANES - j, tile), :]
    xcb = xc.astype(_BF16)
    r_pre, i_pre = _split_gates([_lru_gate_dot(xcb, w["w_gate"], n) for n in range(LRU_BLOCKS)])

    a, bterm = _lru_coeffs(xc, r_pre, i_pre, w)
    hseq, h_last = _group_scan(a, bterm, h_state[...])
    h_state[...] = h_last
    y_lru = hseq * _silu(_in_proj(hb, w, COL_G_LRU))

    y_ref[...] = _post_mix(x, hb, y_sg, y_lru, p_ref[...], w)

    @pl.when(t == pl.num_programs(1) - 1)
    def _():
        conv_out_ref[...] = xl_buf[pl.ds(tile + SUBLANES - (CONV_WIDTH - 1), CONV_WIDTH - 1), :]
        lru_out_ref[...] = h_last


def _resident(shape):
    return pl.BlockSpec(shape, lambda *_: (0,) * len(shape), pipeline_mode=pl.Buffered(1))


def _prompt_call(x, p, weights, sgw, sgb_t):
    batch, seq, d = x.shape
    tile = PROMPT_TILE
    grid = (batch, seq // tile)
    in_specs = [
        pl.BlockSpec((None, tile, d), lambda b, t: (b, t, 0)),
        pl.BlockSpec((None, tile, PLE_DIM), lambda b, t: (b, t, 0)),
    ]
    in_specs += [_resident(wt.shape) for wt in weights]
    in_specs += [_resident(sgw.shape), _resident(sgb_t.shape)]
    out_shape = (
        jax.ShapeDtypeStruct((batch, seq, d), _F32),
        jax.ShapeDtypeStruct((batch, CONV_WIDTH - 1, d), _F32),
        jax.ShapeDtypeStruct((batch, 1, d), _F32),
    )
    out_specs = (
        pl.BlockSpec((None, tile, d), lambda b, t: (b, t, 0)),
        pl.BlockSpec((None, CONV_WIDTH - 1, d), lambda b, t: (b, 0, 0)),
        pl.BlockSpec((None, 1, d), lambda b, t: (b, 0, 0)),
    )
    return pl.pallas_call(
        _prompt_kernel,
        grid=grid,
        in_specs=in_specs,
        out_specs=out_specs,
        out_shape=out_shape,
        scratch_shapes=[
            pltpu.VMEM((tile + SUBLANES, d), _F32),
            pltpu.VMEM((SUBLANES, d), _F32),
            pltpu.VMEM((1, d), _F32),
        ],
        compiler_params=pltpu.CompilerParams(
            dimension_semantics=("arbitrary", "arbitrary"),
            vmem_limit_bytes=VMEM_LIMIT_BYTES),
        name="prompt_layer",
    )(x, p, *weights, sgw, sgb_t)


def _sample_kernel(*refs, steps):
    n_w = len(_WEIGHT_NAMES)
    x_ref, p_ref, conv_in_ref, h0_ref = refs[:4]
    w = dict(zip(_WEIGHT_NAMES, refs[4:4 + n_w]))
    sgw_ref, sgb_ref = refs[4 + n_w], refs[5 + n_w]
    y_ref, conv_out_ref, lru_out_ref, z_out_ref = refs[6 + n_w:]
    d = D_MODEL
    nb = x_ref.shape[0]

    def slab(v, t):
        return v[t * nb:(t + 1) * nb]

    x = jnp.concatenate([x_ref[:, t, :] for t in range(steps)], axis=0)
    p = jnp.concatenate([p_ref[:, t, :] for t in range(steps)], axis=0)
    hb = (_rms_scale(x) * w["norm_pre"][...]).astype(_BF16)

    z = _head_norm(_in_proj(hb, w, COL_V), w["sg_norm"][...])
    for t in range(steps):
        z_out_ref[:, t, :] = slab(z, t)
    s_slabs = []
    for t in range(steps):
        acc = jnp.broadcast_to(sgb_ref[t:t + 1, :], (nb, d))
        for u in range(t + 1):
            acc = acc + sgw_ref[t * steps + u:t * steps + u + 1, :] * slab(z, u)
        s_slabs.append(acc)
    s = jnp.concatenate(s_slabs, axis=0)
    y_sg = _in_proj(hb, w, COL_U) * s * _silu(_in_proj(hb, w, COL_G_SG))

    x_lru = _in_proj(hb, w, COL_X_LRU)
    hist = [conv_in_ref[:, k, :] for k in range(CONV_WIDTH - 1)]
    hist += [slab(x_lru, t) for t in range(steps)]
    for k in range(CONV_WIDTH - 1):
        conv_out_ref[:, k, :] = hist[steps + k]
    xc_slabs = []
    for t in range(steps):
        acc = w["conv_b"][...] + w["conv_w"][0:1, :] * hist[t]
        for k in range(1, CONV_WIDTH):
            acc = acc + w["conv_w"][k:k + 1, :] * hist[t + k]
        xc_slabs.append(acc)
    xc = jnp.concatenate(xc_slabs, axis=0)
    xcb = xc.astype(_BF16)
    r_pre, i_pre = _split_gates([_lru_gate_dot(xcb, w["w_gate"], n) for n in range(LRU_BLOCKS)])
    a, bterm = _lru_coeffs(xc, r_pre, i_pre, w)
    h = h0_ref[...]
    h_slabs = []
    for t in range(steps):
        h = slab(a, t) * h + slab(bterm, t)
        h_slabs.append(h)
    lru_out_ref[...] = h
    y_lru = jnp.concatenate(h_slabs, axis=0) * _silu(_in_proj(hb, w, COL_G_LRU))

    y = _post_mix(x, hb, y_sg, y_lru, p, w)
    for t in range(steps):
        y_ref[:, t, :] = slab(y, t)


def _sample_call(x, p, conv_in, h0, weights, sgw_rows, sgb_rows):
    nb, steps, d = x.shape
    out_shape = (
        jax.ShapeDtypeStruct((nb, steps, d), _F32),
        jax.ShapeDtypeStruct((nb, CONV_WIDTH - 1, d), _F32),
        jax.ShapeDtypeStruct((nb, d), _F32),
        jax.ShapeDtypeStruct((nb, steps, d), _F32),
    )
    return pl.pallas_call(
        functools.partial(_sample_kernel, steps=steps),
        out_shape=out_shape,
        compiler_params=pltpu.CompilerParams(vmem_limit_bytes=VMEM_LIMIT_BYTES),
        name="sample_layer",
    )(x, p, conv_in, h0, *weights, sgw_rows, sgb_rows)


def _layer_weights(l, norm_pre, w_in, sg_norm, conv_w, conv_b, lru_wa, lru_ba, lru_wx, lru_bx,
                   lru_lambda, w_branch_sg, w_branch_lru, w_merge, b_merge, w_out, norm_post,
                   w_ple, w_ple_gate, b_ple_gate):
    row = lambda v: v[l].reshape(1, -1)
    w_gate = jnp.concatenate([lru_wa[l], lru_wx[l]], axis=-1).astype(_BF16)
    by_name = dict(
        norm_pre=row(norm_pre), w_in=w_in[l].astype(_BF16), sg_norm=row(sg_norm),
        conv_w=conv_w[l], conv_b=row(conv_b), w_gate=w_gate, lru_ba=row(lru_ba),
        lru_bx=row(lru_bx), lru_lambda=row(lru_lambda),
        w_branch_sg=w_branch_sg[l].astype(_BF16), w_branch_lru=w_branch_lru[l].astype(_BF16),
        w_merge=w_merge[l].astype(_BF16), b_merge=row(b_merge), w_out=w_out[l].astype(_BF16),
        norm_post=row(norm_post), w_ple=w_ple[l].astype(_BF16),
        w_ple_gate=w_ple_gate[l].astype(_BF16), b_ple_gate=row(b_ple_gate))
    return [by_name[n] for n in _WEIGHT_NAMES]


def kernel(x_prompt, x_sample, p_prompt, p_sample, state_conv, state_lru, norm_pre, w_in, sg_norm,
           sg_w, sg_b, conv_w, conv_b, lru_wa, lru_ba, lru_wx, lru_bx, lru_lambda, w_branch_sg,
           w_branch_lru, w_merge, b_merge, w_out, norm_post, w_ple, w_ple_gate, b_ple_gate):
    depth = norm_pre.shape[0]
    nb, steps, d = x_sample.shape
    xp, xs = x_prompt, x_sample
    conv_p, lru_p, conv_s, lru_s, chunk_s = [], [], [], [], []
    for l in range(depth):
        weights = _layer_weights(
            l, norm_pre, w_in, sg_norm, conv_w, conv_b, lru_wa, lru_ba, lru_wx, lru_bx,
            lru_lambda, w_branch_sg, w_branch_lru, w_merge, b_merge, w_out, norm_post, w_ple,
            w_ple_gate, b_ple_gate)
        xp, cp, hp = _prompt_call(xp, p_prompt[l], weights, sg_w[l], sg_b[l].T)
        conv_p.append(cp)
        lru_p.append(hp.reshape(-1, d))
        sgw_rows = jnp.repeat(sg_w[l][:, :steps, :steps].reshape(SG_HEADS, steps * steps).T,
                              SG_HEAD_DIM, axis=1)
        sgb_rows = jnp.repeat(sg_b[l][:, :steps].T, SG_HEAD_DIM, axis=1)
        xs, cs, hs, zs = _sample_call(
            xs, p_sample[l], state_conv[l], state_lru[l], weights, sgw_rows, sgb_rows)
        conv_s.append(cs)
        lru_s.append(hs)
        chunk_s.append(zs)
    return (xp, xs, jnp.stack(conv_p), jnp.stack(lru_p),
            jnp.stack(conv_s), jnp.stack(lru_s), jnp.stack(chunk_s))
```

```python
import functools

import jax
import jax.numpy as jnp
from jax import lax
from jax.experimental import pallas as pl
from jax.experimental.pallas import tpu as pltpu

D_MODEL = 1024
PLE_DIM = 256
SG_HEADS = 4
SG_HEAD_DIM = D_MODEL // SG_HEADS
CHUNK = 128
LRU_BLOCKS = 8
LRU_BLOCK_DIM = D_MODEL // LRU_BLOCKS
CONV_WIDTH = 4
LRU_C = 8.0
EPS = 1e-6

SUBLANES = 8
PROMPT_TILE = 256
VMEM_LIMIT_BYTES = 56 * 1024 * 1024
STAGE_COLS = 512

_BF16 = jnp.bfloat16
_F32 = jnp.float32

_VECTOR_NAMES = ("norm_pre", "sg_norm", "conv_w", "conv_b", "lru_ba", "lru_bx", "lru_lambda",
                 "b_merge", "norm_post", "b_ple_gate")
_MATRIX_SHAPES = dict(
    w_in=(D_MODEL, 5 * D_MODEL), w_merge=(D_MODEL, 2 * D_MODEL),
    w_branch_sg=(D_MODEL, D_MODEL), w_branch_lru=(D_MODEL, D_MODEL), w_out=(D_MODEL, D_MODEL),
    w_ple_gate=(D_MODEL, D_MODEL), w_ple=(PLE_DIM, D_MODEL),
    w_gate=(D_MODEL, 2 * LRU_BLOCK_DIM))
_MATRIX_NAMES = tuple(_MATRIX_SHAPES)

COL_U, COL_V, COL_G_SG, COL_X_LRU, COL_G_LRU = range(5)


def _dot(a, b):
    return jnp.dot(a, b, preferred_element_type=_F32)


def _rms_scale(x):
    var = jnp.mean(x * x, axis=-1, keepdims=True)
    return x * lax.rsqrt(var + EPS)


def _silu(x):
    return x * jax.nn.sigmoid(x)


def _in_proj(hb, w, col):
    return _dot(hb, w["w_in"][:, col * D_MODEL:(col + 1) * D_MODEL])


def _merge_pre(hb, w, half):
    return _dot(hb, w["w_merge"][:, half * D_MODEL:(half + 1) * D_MODEL])


def _head_norm(v, sgn):
    zs = []
    for h in range(SG_HEADS):
        sl = slice(h * SG_HEAD_DIM, (h + 1) * SG_HEAD_DIM)
        zs.append(_rms_scale(v[:, sl]) * sgn[:, sl])
    return jnp.concatenate(zs, axis=-1)


def _lru_gates(xcb, wg_ref):
    gates = []
    for n in range(LRU_BLOCKS):
        sl = slice(n * LRU_BLOCK_DIM, (n + 1) * LRU_BLOCK_DIM)
        gates.append(_dot(xcb[:, sl], wg_ref[sl, :]))
    r_pre = jnp.concatenate([g[:, :LRU_BLOCK_DIM] for g in gates], axis=-1)
    i_pre = jnp.concatenate([g[:, LRU_BLOCK_DIM:] for g in gates], axis=-1)
    return r_pre, i_pre


def _lru_coeffs(xc, r_pre, i_pre, w):
    r = jax.nn.sigmoid(r_pre + w["lru_ba"][...])
    i = jax.nn.sigmoid(i_pre + w["lru_bx"][...])
    log_a = (-LRU_C * jax.nn.softplus(-w["lru_lambda"][...])) * r
    a = jnp.exp(log_a)
    mult = jnp.sqrt(1.0 - a * a)
    return a, mult * (i * xc)


def _merge_gate(pre, w, half):
    return jax.nn.sigmoid(pre + w["b_merge"][:, half * D_MODEL:(half + 1) * D_MODEL])


def _post_mix(x, hb, y_sg, y_lru, p, w):
    g_a = _merge_gate(_merge_pre(hb, w, 0), w, 0)
    merged = g_a * _dot(y_sg.astype(_BF16), w["w_branch_sg"][...])
    g_b = _merge_gate(_merge_pre(hb, w, 1), w, 1)
    merged = merged + g_b * _dot(y_lru.astype(_BF16), w["w_branch_lru"][...])
    o = _dot(merged.astype(_BF16), w["w_out"][...])
    x1 = x + _rms_scale(o) * w["norm_post"][...]
    gate = jax.nn.sigmoid(_dot(x1.astype(_BF16), w["w_ple_gate"][...]) + w["b_ple_gate"][...])
    return x1 + gate * _dot(p.astype(_BF16), w["w_ple"][...])


def _group_scan(a, b, h_prev):
    rows, d = a.shape
    groups = rows // SUBLANES
    a3 = a.reshape(groups, SUBLANES, d)
    b3 = b.reshape(groups, SUBLANES, d)
    row = lax.broadcasted_iota(jnp.int32, a3.shape, 1)
    shift = 1
    while shift < SUBLANES:
        keep = row >= shift
        a_sh = jnp.where(keep, pltpu.roll(a3, shift, 1), 1.0)
        b_sh = jnp.where(keep, pltpu.roll(b3, shift, 1), 0.0)
        b3 = b3 + a3 * b_sh
        a3 = a3 * a_sh
        shift *= 2
    hs = []
    h = h_prev
    for g in range(groups):
        hg = b3[g] + a3[g] * h
        hs.append(hg)
        h = hg[SUBLANES - 1:SUBLANES, :]
    return jnp.concatenate(hs, axis=0), h


def _spatial_mix(z, sgw_ref, sgb_ref):
    tri = (lax.broadcasted_iota(jnp.int32, (CHUNK, CHUNK), 0)
           >= lax.broadcasted_iota(jnp.int32, (CHUNK, CHUNK), 1))
    s_rows = []
    for c in range(z.shape[0] // CHUNK):
        s_heads = []
        for h in range(SG_HEADS):
            wm = jnp.where(tri, sgw_ref[h], 0.0).astype(_BF16)
            zc = z[c * CHUNK:(c + 1) * CHUNK, h * SG_HEAD_DIM:(h + 1) * SG_HEAD_DIM]
            s_heads.append(_dot(wm, zc) + sgb_ref[:, h:h + 1])
        s_rows.append(jnp.concatenate(s_heads, axis=-1))
    return jnp.concatenate(s_rows, axis=0)


def _prompt_kernel(*refs):
    n_w = len(_VECTOR_NAMES) + len(_MATRIX_NAMES)
    x_ref, p_ref = refs[0], refs[1]
    w = dict(zip(_VECTOR_NAMES + _MATRIX_NAMES, refs[2:2 + n_w]))
    sgw_ref, sgb_ref = refs[2 + n_w], refs[3 + n_w]
    y_ref, conv_out_ref, lru_out_ref = refs[4 + n_w:7 + n_w]
    xl_buf, conv_state, h_state = refs[7 + n_w:]

    t = pl.program_id(1)
    tile = x_ref.shape[0]

    @pl.when(t == 0)
    def _():
        conv_state[...] = jnp.zeros_like(conv_state)
        h_state[...] = jnp.zeros_like(h_state)

    x = x_ref[...]
    hb = (_rms_scale(x) * w["norm_pre"][...]).astype(_BF16)

    z = _head_norm(_in_proj(hb, w, COL_V), w["sg_norm"][...]).astype(_BF16)
    s = _spatial_mix(z, sgw_ref, sgb_ref)
    y_sg = _in_proj(hb, w, COL_U) * s * _silu(_in_proj(hb, w, COL_G_SG))

    x_lru = _in_proj(hb, w, COL_X_LRU)
    xl_buf[0:SUBLANES, :] = conv_state[...]
    xl_buf[SUBLANES:, :] = x_lru
    conv_state[...] = xl_buf[tile:tile + SUBLANES, :]
    xc = w["conv_b"][...] + w["conv_w"][CONV_WIDTH - 1:CONV_WIDTH, :] * x_lru
    for j in range(1, CONV_WIDTH):
        k = CONV_WIDTH - 1 - j
        xc = xc + w["conv_w"][k:k + 1, :] * xl_buf[pl.ds(SUBLANES - j, tile), :]
    r_pre, i_pre = _lru_gates(xc.astype(_BF16), w["w_gate"])

    a, bterm = _lru_coeffs(xc, r_pre, i_pre, w)
    hseq, h_last = _group_scan(a, bterm, h_state[...])
    h_state[...] = h_last
    y_lru = hseq * _silu(_in_proj(hb, w, COL_G_LRU))

    y_ref[...] = _post_mix(x, hb, y_sg, y_lru, p_ref[...], w)

    @pl.when(t == pl.num_programs(1) - 1)
    def _():
        conv_out_ref[...] = xl_buf[pl.ds(tile + SUBLANES - (CONV_WIDTH - 1), CONV_WIDTH - 1), :]
        lru_out_ref[...] = h_last


def _resident(shape):
    return pl.BlockSpec(shape, lambda *_: (0,) * len(shape), pipeline_mode=pl.Buffered(1))


def _prompt_call(x, p, vectors, matrices, sgw, sgb_t):
    batch, seq, d = x.shape
    tile = PROMPT_TILE
    grid = (batch, seq // tile)
    in_specs = [
        pl.BlockSpec((None, tile, d), lambda b, t: (b, t, 0)),
        pl.BlockSpec((None, tile, PLE_DIM), lambda b, t: (b, t, 0)),
    ]
    in_specs += [_resident(v.shape) for v in vectors + matrices]
    in_specs += [_resident(sgw.shape), _resident(sgb_t.shape)]
    out_shape = (
        jax.ShapeDtypeStruct((batch, seq, d), _F32),
        jax.ShapeDtypeStruct((batch, CONV_WIDTH - 1, d), _F32),
        jax.ShapeDtypeStruct((batch, 1, d), _F32),
    )
    out_specs = (
        pl.BlockSpec((None, tile, d), lambda b, t: (b, t, 0)),
        pl.BlockSpec((None, CONV_WIDTH - 1, d), lambda b, t: (b, 0, 0)),
        pl.BlockSpec((None, 1, d), lambda b, t: (b, 0, 0)),
    )
    return pl.pallas_call(
        _prompt_kernel,
        grid=grid,
        in_specs=in_specs,
        out_specs=out_specs,
        out_shape=out_shape,
        scratch_shapes=[
            pltpu.VMEM((tile + SUBLANES, d), _F32),
            pltpu.VMEM((SUBLANES, d), _F32),
            pltpu.VMEM((1, d), _F32),
        ],
        compiler_params=pltpu.CompilerParams(
            dimension_semantics=("arbitrary", "arbitrary"),
            vmem_limit_bytes=VMEM_LIMIT_BYTES),
        name="prompt_layer",
    )(x, p, *vectors, *matrices, sgw, sgb_t)


_CAST_SOURCES = (
    ("w_in", "w_in", 0), ("w_merge", "w_merge", 0), ("w_branch_sg", "w_branch_sg", 0),
    ("w_branch_lru", "w_branch_lru", 0), ("w_out", "w_out", 0), ("w_ple_gate", "w_ple_gate", 0),
    ("w_ple", "w_ple", 0), ("lru_wa", "w_gate", 0), ("lru_wx", "w_gate", LRU_BLOCK_DIM))


def _cast_weights(src, dst, out, stage, sem_in, sem_out):
    tasks = []
    for name, matrix, col0 in _CAST_SOURCES:
        cols_total = src[name].shape[1]
        for c in range(0, cols_total, STAGE_COLS):
            tasks.append((src[name], matrix, c, col0 + c, min(STAGE_COLS, cols_total - c)))

    def stage_copy(i):
        ref, _, c, _, cols = tasks[i]
        rows = ref.shape[0]
        return pltpu.make_async_copy(
            ref.at[:, pl.ds(c, cols)],
            stage.at[i % 2, pl.ds(0, rows), pl.ds(0, cols)],
            sem_in.at[i % 2])

    def out_copy(matrix):
        return pltpu.make_async_copy(dst[matrix], out[matrix],
                                     sem_out.at[_MATRIX_NAMES.index(matrix)])

    stage_copy(0).start()
    for i, (ref, matrix, _, dst_col, cols) in enumerate(tasks):
        if i + 1 < len(tasks):
            stage_copy(i + 1).start()
        stage_copy(i).wait()
        rows = ref.shape[0]
        dst[matrix][:, dst_col:dst_col + cols] = stage[i % 2, 0:rows, 0:cols].astype(_BF16)
        if i + 1 == len(tasks) or tasks[i + 1][1] != matrix:
            out_copy(matrix).start()
    return [out_copy(matrix) for matrix in _MATRIX_NAMES]


def _sample_kernel(*refs, steps):
    n_v, n_s, n_m = len(_VECTOR_NAMES), len(_CAST_SOURCES), len(_MATRIX_NAMES)
    x_ref, p_ref, conv_in_ref, h0_ref = refs[:4]
    vectors = dict(zip(_VECTOR_NAMES, refs[4:4 + n_v]))
    sgw_ref, sgb_ref = refs[4 + n_v], refs[5 + n_v]
    src = dict(zip([s[0] for s in _CAST_SOURCES], refs[6 + n_v:6 + n_v + n_s]))
    outs = refs[6 + n_v + n_s:]
    y_ref, conv_out_ref, lru_out_ref, z_out_ref = outs[:4]
    out_bf16 = dict(zip(_MATRIX_NAMES, outs[4:4 + n_m]))
    scratch = outs[4 + n_m:]
    w_bf16 = dict(zip(_MATRIX_NAMES, scratch[:n_m]))
    stage, sem_in, sem_out = scratch[n_m:]
    d = D_MODEL
    nb = x_ref.shape[0]

    pending = _cast_weights(src, w_bf16, out_bf16, stage, sem_in, sem_out)
    w = {**vectors, **w_bf16}

    def slab(v, t):
        return v[t * nb:(t + 1) * nb]

    x = jnp.concatenate([x_ref[:, t, :] for t in range(steps)], axis=0)
    p = jnp.concatenate([p_ref[:, t, :] for t in range(steps)], axis=0)
    hb = (_rms_scale(x) * w["norm_pre"][...]).astype(_BF16)

    z = _head_norm(_in_proj(hb, w, COL_V), w["sg_norm"][...])
    for t in range(steps):
        z_out_ref[:, t, :] = slab(z, t)
    s_slabs = []
    for t in range(steps):
        acc = jnp.broadcast_to(sgb_ref[t:t + 1, :], (nb, d))
        for u in range(t + 1):
            acc = acc + sgw_ref[t * steps + u:t * steps + u + 1, :] * slab(z, u)
        s_slabs.append(acc)
    s = jnp.concatenate(s_slabs, axis=0)
    y_sg = _in_proj(hb, w, COL_U) * s * _silu(_in_proj(hb, w, COL_G_SG))

    x_lru = _in_proj(hb, w, COL_X_LRU)
    hist = [conv_in_ref[:, k, :] for k in range(CONV_WIDTH - 1)]
    hist += [slab(x_lru, t) for t in range(steps)]
    for k in range(CONV_WIDTH - 1):
        conv_out_ref[:, k, :] = hist[steps + k]
    xc_slabs = []
    for t in range(steps):
        acc = w["conv_b"][...] + w["conv_w"][0:1, :] * hist[t]
        for k in range(1, CONV_WIDTH):
            acc = acc + w["conv_w"][k:k + 1, :] * hist[t + k]
        xc_slabs.append(acc)
    xc = jnp.concatenate(xc_slabs, axis=0)
    r_pre, i_pre = _lru_gates(xc.astype(_BF16), w["w_gate"])
    a, bterm = _lru_coeffs(xc, r_pre, i_pre, w)
    h = h0_ref[...]
    h_slabs = []
    for t in range(steps):
        h = slab(a, t) * h + slab(bterm, t)
        h_slabs.append(h)
    lru_out_ref[...] = h
    y_lru = jnp.concatenate(h_slabs, axis=0) * _silu(_in_proj(hb, w, COL_G_LRU))

    y = _post_mix(x, hb, y_sg, y_lru, p, w)
    for t in range(steps):
        y_ref[:, t, :] = slab(y, t)

    for copy in pending:
        copy.wait()


def _sample_call(x, p, conv_in, h0, vectors, sources, sgw_rows, sgb_rows):
    nb, steps, d = x.shape
    vmem = pl.BlockSpec(memory_space=pltpu.VMEM)
    hbm = pl.BlockSpec(memory_space=pl.ANY)
    matrix_shapes = [_MATRIX_SHAPES[m] for m in _MATRIX_NAMES]
    stage_rows = max(s.shape[0] for s in sources)
    out_shape = (
        jax.ShapeDtypeStruct((nb, steps, d), _F32),
        jax.ShapeDtypeStruct((nb, CONV_WIDTH - 1, d), _F32),
        jax.ShapeDtypeStruct((nb, d), _F32),
        jax.ShapeDtypeStruct((nb, steps, d), _F32),
    ) + tuple(jax.ShapeDtypeStruct(shape, _BF16) for shape in matrix_shapes)
    outs = pl.pallas_call(
        functools.partial(_sample_kernel, steps=steps),
        in_specs=[vmem] * (6 + len(vectors)) + [hbm] * len(sources),
        out_specs=(vmem,) * 4 + (hbm,) * len(matrix_shapes),
        out_shape=out_shape,
        scratch_shapes=[pltpu.VMEM(shape, _BF16) for shape in matrix_shapes] + [
            pltpu.VMEM((2, stage_rows, STAGE_COLS), _F32),
            pltpu.SemaphoreType.DMA((2,)),
            pltpu.SemaphoreType.DMA((len(matrix_shapes),)),
        ],
        compiler_params=pltpu.CompilerParams(vmem_limit_bytes=VMEM_LIMIT_BYTES),
        name="sample_layer",
    )(x, p, conv_in, h0, *vectors, sgw_rows, sgb_rows, *sources)
    return outs[:4], list(outs[4:])


def kernel(x_prompt, x_sample, p_prompt, p_sample, state_conv, state_lru, norm_pre, w_in, sg_norm,
           sg_w, sg_b, conv_w, conv_b, lru_wa, lru_ba, lru_wx, lru_bx, lru_lambda, w_branch_sg,
           w_branch_lru, w_merge, b_merge, w_out, norm_post, w_ple, w_ple_gate, b_ple_gate):
    depth = norm_pre.shape[0]
    nb, steps, d = x_sample.shape
    xp, xs = x_prompt, x_sample
    conv_p, lru_p, conv_s, lru_s, chunk_s = [], [], [], [], []
    for l in range(depth):
        row = lambda v: v[l].reshape(1, -1)
        by_name = dict(
            norm_pre=row(norm_pre), sg_norm=row(sg_norm), conv_w=conv_w[l], conv_b=row(conv_b),
            lru_ba=row(lru_ba), lru_bx=row(lru_bx), lru_lambda=row(lru_lambda),
            b_merge=row(b_merge), norm_post=row(norm_post), b_ple_gate=row(b_ple_gate))
        vectors = [by_name[n] for n in _VECTOR_NAMES]
        f32_by_name = dict(
            w_in=w_in[l], w_merge=w_merge[l], w_branch_sg=w_branch_sg[l],
            w_branch_lru=w_branch_lru[l], w_out=w_out[l], w_ple_gate=w_ple_gate[l],
            w_ple=w_ple[l], lru_wa=lru_wa[l].reshape(d, LRU_BLOCK_DIM),
            lru_wx=lru_wx[l].reshape(d, LRU_BLOCK_DIM))
        sources = [f32_by_name[s[0]] for s in _CAST_SOURCES]
        sgw_rows = jnp.repeat(sg_w[l][:, :steps, :steps].reshape(SG_HEADS, steps * steps).T,
                              SG_HEAD_DIM, axis=1)
        sgb_rows = jnp.repeat(sg_b[l][:, :steps].T, SG_HEAD_DIM, axis=1)
        (xs, cs, hs, zs), matrices = _sample_call(
            xs, p_sample[l], state_conv[l], state_lru[l], vectors, sources, sgw_rows, sgb_rows)
        conv_s.append(cs)
        lru_s.append(hs)
        chunk_s.append(zs)
        xp, cp, hp = _prompt_call(xp, p_prompt[l], vectors, matrices, sg_w[l], sg_b[l].T)
        conv_p.append(cp)
        lru_p.append(hp.reshape(-1, d))
    return (xp, xs, jnp.stack(conv_p), jnp.stack(lru_p),
            jnp.stack(conv_s), jnp.stack(lru_s), jnp.stack(chunk_s))
```

```python
import functools

import jax
import jax.numpy as jnp
from jax import lax
from jax.experimental import pallas as pl
from jax.experimental.pallas import tpu as pltpu

D_MODEL = 1024
PLE_DIM = 256
SG_HEADS = 4
SG_HEAD_DIM = D_MODEL // SG_HEADS
CHUNK = 128
LRU_BLOCKS = 8
LRU_BLOCK_DIM = D_MODEL // LRU_BLOCKS
CONV_WIDTH = 4
LRU_C = 8.0
EPS = 1e-6

SUBLANES = 8
PROMPT_TILE = 256
VMEM_LIMIT_BYTES = 56 * 1024 * 1024
STAGE_COLS = 512
STAGE_SLOTS = 4

_BF16 = jnp.bfloat16
_F32 = jnp.float32

_VECTOR_NAMES = ("norm_pre", "sg_norm", "conv_w", "conv_b", "lru_ba", "lru_bx", "lru_lambda",
                 "b_merge", "norm_post", "b_ple_gate")
_MATRIX_SHAPES = dict(
    w_in=(D_MODEL, 5 * D_MODEL), w_merge=(D_MODEL, 2 * D_MODEL),
    w_branch_sg=(D_MODEL, D_MODEL), w_branch_lru=(D_MODEL, D_MODEL), w_out=(D_MODEL, D_MODEL),
    w_ple_gate=(D_MODEL, D_MODEL), w_ple=(PLE_DIM, D_MODEL),
    w_gate=(D_MODEL, 2 * LRU_BLOCK_DIM))
_MATRIX_NAMES = tuple(_MATRIX_SHAPES)

COL_U, COL_V, COL_G_SG, COL_X_LRU, COL_G_LRU = range(5)


def _dot(a, b):
    return jnp.dot(a, b, preferred_element_type=_F32)


def _rms_scale(x):
    var = jnp.mean(x * x, axis=-1, keepdims=True)
    return x * lax.rsqrt(var + EPS)


def _silu(x):
    return x * jax.nn.sigmoid(x)


def _in_proj(hb, w, col):
    return _dot(hb, w["w_in"][:, col * D_MODEL:(col + 1) * D_MODEL])


def _merge_pre(hb, w, half):
    return _dot(hb, w["w_merge"][:, half * D_MODEL:(half + 1) * D_MODEL])


def _head_norm(v, sgn):
    zs = []
    for h in range(SG_HEADS):
        sl = slice(h * SG_HEAD_DIM, (h + 1) * SG_HEAD_DIM)
        zs.append(_rms_scale(v[:, sl]) * sgn[:, sl])
    return jnp.concatenate(zs, axis=-1)


def _lru_gates(xcb, wg_ref):
    gates = []
    for n in range(LRU_BLOCKS):
        sl = slice(n * LRU_BLOCK_DIM, (n + 1) * LRU_BLOCK_DIM)
        gates.append(_dot(xcb[:, sl], wg_ref[sl, :]))
    r_pre = jnp.concatenate([g[:, :LRU_BLOCK_DIM] for g in gates], axis=-1)
    i_pre = jnp.concatenate([g[:, LRU_BLOCK_DIM:] for g in gates], axis=-1)
    return r_pre, i_pre


def _lru_coeffs(xc, r_pre, i_pre, w):
    r = jax.nn.sigmoid(r_pre + w["lru_ba"][...])
    i = jax.nn.sigmoid(i_pre + w["lru_bx"][...])
    log_a = (-LRU_C * jax.nn.softplus(-w["lru_lambda"][...])) * r
    a = jnp.exp(log_a)
    mult = jnp.sqrt(1.0 - a * a)
    return a, mult * (i * xc)


def _merge_gate(pre, w, half):
    return jax.nn.sigmoid(pre + w["b_merge"][:, half * D_MODEL:(half + 1) * D_MODEL])


def _post_mix(x, hb, y_sg, y_lru, p, w):
    g_a = _merge_gate(_merge_pre(hb, w, 0), w, 0)
    merged = g_a * _dot(y_sg.astype(_BF16), w["w_branch_sg"][...])
    g_b = _merge_gate(_merge_pre(hb, w, 1), w, 1)
    merged = merged + g_b * _dot(y_lru.astype(_BF16), w["w_branch_lru"][...])
    o = _dot(merged.astype(_BF16), w["w_out"][...])
    x1 = x + _rms_scale(o) * w["norm_post"][...]
    gate = jax.nn.sigmoid(_dot(x1.astype(_BF16), w["w_ple_gate"][...]) + w["b_ple_gate"][...])
    return x1 + gate * _dot(p.astype(_BF16), w["w_ple"][...])


def _group_scan(a, b, h_prev):
    rows, d = a.shape
    groups = rows // SUBLANES
    a3 = a.reshape(groups, SUBLANES, d)
    b3 = b.reshape(groups, SUBLANES, d)
    row = lax.broadcasted_iota(jnp.int32, a3.shape, 1)
    shift = 1
    while shift < SUBLANES:
        keep = row >= shift
        a_sh = jnp.where(keep, pltpu.roll(a3, shift, 1), 1.0)
        b_sh = jnp.where(keep, pltpu.roll(b3, shift, 1), 0.0)
        b3 = b3 + a3 * b_sh
        a3 = a3 * a_sh
        shift *= 2
    hs = []
    h = h_prev
    for g in range(groups):
        hg = b3[g] + a3[g] * h
        hs.append(hg)
        h = hg[SUBLANES - 1:SUBLANES, :]
    return jnp.concatenate(hs, axis=0), h


def _spatial_mix(z, sgw_ref, sgb_ref):
    tri = (lax.broadcasted_iota(jnp.int32, (CHUNK, CHUNK), 0)
           >= lax.broadcasted_iota(jnp.int32, (CHUNK, CHUNK), 1))
    s_rows = []
    for c in range(z.shape[0] // CHUNK):
        s_heads = []
        for h in range(SG_HEADS):
            wm = jnp.where(tri, sgw_ref[h], 0.0).astype(_BF16)
            zc = z[c * CHUNK:(c + 1) * CHUNK, h * SG_HEAD_DIM:(h + 1) * SG_HEAD_DIM]
            s_heads.append(_dot(wm, zc) + sgb_ref[:, h:h + 1])
        s_rows.append(jnp.concatenate(s_heads, axis=-1))
    return jnp.concatenate(s_rows, axis=0)


def _prompt_kernel(*refs):
    n_w = len(_VECTOR_NAMES) + len(_MATRIX_NAMES)
    x_ref, p_ref = refs[0], refs[1]
    w = dict(zip(_VECTOR_NAMES + _MATRIX_NAMES, refs[2:2 + n_w]))
    sgw_ref, sgb_ref = refs[2 + n_w], refs[3 + n_w]
    y_ref, conv_out_ref, lru_out_ref = refs[4 + n_w:7 + n_w]
    xl_buf, conv_state, h_state = refs[7 + n_w:]

    t = pl.program_id(1)
    tile = x_ref.shape[0]

    @pl.when(t == 0)
    def _():
        conv_state[...] = jnp.zeros_like(conv_state)
        h_state[...] = jnp.zeros_like(h_state)

    x = x_ref[...]
    hb = (_rms_scale(x) * w["norm_pre"][...]).astype(_BF16)

    z = _head_norm(_in_proj(hb, w, COL_V), w["sg_norm"][...]).astype(_BF16)
    s = _spatial_mix(z, sgw_ref, sgb_ref)
    y_sg = _in_proj(hb, w, COL_U) * s * _silu(_in_proj(hb, w, COL_G_SG))

    x_lru = _in_proj(hb, w, COL_X_LRU)
    xl_buf[0:SUBLANES, :] = conv_state[...]
    xl_buf[SUBLANES:, :] = x_lru
    conv_state[...] = xl_buf[tile:tile + SUBLANES, :]
    xc = w["conv_b"][...] + w["conv_w"][CONV_WIDTH - 1:CONV_WIDTH, :] * x_lru
    for j in range(1, CONV_WIDTH):
        k = CONV_WIDTH - 1 - j
        xc = xc + w["conv_w"][k:k + 1, :] * xl_buf[pl.ds(SUBLANES - j, tile), :]
    r_pre, i_pre = _lru_gates(xc.astype(_BF16), w["w_gate"])

    a, bterm = _lru_coeffs(xc, r_pre, i_pre, w)
    hseq, h_last = _group_scan(a, bterm, h_state[...])
    h_state[...] = h_last
    y_lru = hseq * _silu(_in_proj(hb, w, COL_G_LRU))

    y_ref[...] = _post_mix(x, hb, y_sg, y_lru, p_ref[...], w)

    @pl.when(t == pl.num_programs(1) - 1)
    def _():
        conv_out_ref[...] = xl_buf[pl.ds(tile + SUBLANES - (CONV_WIDTH - 1), CONV_WIDTH - 1), :]
        lru_out_ref[...] = h_last


def _resident(shape):
    return pl.BlockSpec(shape, lambda *_: (0,) * len(shape), pipeline_mode=pl.Buffered(1))


def _prompt_call(x, p, vectors, matrices, sgw, sgb_t):
    batch, seq, d = x.shape
    tile = PROMPT_TILE
    grid = (batch, seq // tile)
    in_specs = [
        pl.BlockSpec((None, tile, d), lambda b, t: (b, t, 0)),
        pl.BlockSpec((None, tile, PLE_DIM), lambda b, t: (b, t, 0)),
    ]
    in_specs += [_resident(v.shape) for v in vectors + matrices]
    in_specs += [_resident(sgw.shape), _resident(sgb_t.shape)]
    out_shape = (
        jax.ShapeDtypeStruct((batch, seq, d), _F32),
        jax.ShapeDtypeStruct((batch, CONV_WIDTH - 1, d), _F32),
        jax.ShapeDtypeStruct((batch, 1, d), _F32),
    )
    out_specs = (
        pl.BlockSpec((None, tile, d), lambda b, t: (b, t, 0)),
        pl.BlockSpec((None, CONV_WIDTH - 1, d), lambda b, t: (b, 0, 0)),
        pl.BlockSpec((None, 1, d), lambda b, t: (b, 0, 0)),
    )
    return pl.pallas_call(
        _prompt_kernel,
        grid=grid,
        in_specs=in_specs,
        out_specs=out_specs,
        out_shape=out_shape,
        scratch_shapes=[
            pltpu.VMEM((tile + SUBLANES, d), _F32),
            pltpu.VMEM((SUBLANES, d), _F32),
            pltpu.VMEM((1, d), _F32),
        ],
        compiler_params=pltpu.CompilerParams(
            dimension_semantics=("arbitrary", "arbitrary"),
            vmem_limit_bytes=VMEM_LIMIT_BYTES),
        name="prompt_layer",
    )(x, p, *vectors, *matrices, sgw, sgb_t)


_CAST_SOURCES = (
    ("w_in", "w_in", 0), ("w_merge", "w_merge", 0), ("w_branch_sg", "w_branch_sg", 0),
    ("w_branch_lru", "w_branch_lru", 0), ("w_out", "w_out", 0), ("w_ple_gate", "w_ple_gate", 0),
    ("w_ple", "w_ple", 0), ("lru_wa", "w_gate", 0), ("lru_wx", "w_gate", LRU_BLOCK_DIM))


def _cast_weights(src, dst, out, stage, sem_in, sem_out):
    tasks = []
    for name, matrix, col0 in _CAST_SOURCES:
        cols_total = src[name].shape[1]
        for c in range(0, cols_total, STAGE_COLS):
            tasks.append((src[name], matrix, c, col0 + c, min(STAGE_COLS, cols_total - c)))

    def stage_copy(i):
        ref, _, c, _, cols = tasks[i]
        rows = ref.shape[0]
        return pltpu.make_async_copy(
            ref.at[:, pl.ds(c, cols)],
            stage.at[i % STAGE_SLOTS, pl.ds(0, rows), pl.ds(0, cols)],
            sem_in.at[i % STAGE_SLOTS])

    def out_copy(matrix):
        return pltpu.make_async_copy(dst[matrix], out[matrix],
                                     sem_out.at[_MATRIX_NAMES.index(matrix)])

    ahead = STAGE_SLOTS - 1
    for i in range(min(ahead, len(tasks))):
        stage_copy(i).start()
    for i, (ref, matrix, _, dst_col, cols) in enumerate(tasks):
        if i + ahead < len(tasks):
            stage_copy(i + ahead).start()
        stage_copy(i).wait()
        rows = ref.shape[0]
        staged = stage[i % STAGE_SLOTS, 0:rows, 0:cols]
        dst[matrix][:, dst_col:dst_col + cols] = staged.astype(_BF16)
        if i + 1 == len(tasks) or tasks[i + 1][1] != matrix:
            out_copy(matrix).start()
    return [out_copy(matrix) for matrix in _MATRIX_NAMES]


def _sample_kernel(*refs, steps):
    n_v, n_s, n_m = len(_VECTOR_NAMES), len(_CAST_SOURCES), len(_MATRIX_NAMES)
    x_ref, p_ref, conv_in_ref, h0_ref = refs[:4]
    vectors = dict(zip(_VECTOR_NAMES, refs[4:4 + n_v]))
    sgw_ref, sgb_ref = refs[4 + n_v], refs[5 + n_v]
    src = dict(zip([s[0] for s in _CAST_SOURCES], refs[6 + n_v:6 + n_v + n_s]))
    outs = refs[6 + n_v + n_s:]
    y_ref, conv_out_ref, lru_out_ref, z_out_ref = outs[:4]
    out_bf16 = dict(zip(_MATRIX_NAMES, outs[4:4 + n_m]))
    scratch = outs[4 + n_m:]
    w_bf16 = dict(zip(_MATRIX_NAMES, scratch[:n_m]))
    stage, sem_in, sem_out = scratch[n_m:]
    d = D_MODEL
    nb = x_ref.shape[0]

    pending = _cast_weights(src, w_bf16, out_bf16, stage, sem_in, sem_out)
    w = {**vectors, **w_bf16}

    def slab(v, t):
        return v[t * nb:(t + 1) * nb]

    x = jnp.concatenate([x_ref[:, t, :] for t in range(steps)], axis=0)
    p = jnp.concatenate([p_ref[:, t, :] for t in range(steps)], axis=0)
    hb = (_rms_scale(x) * w["norm_pre"][...]).astype(_BF16)

    z = _head_norm(_in_proj(hb, w, COL_V), w["sg_norm"][...])
    for t in range(steps):
        z_out_ref[:, t, :] = slab(z, t)
    s_slabs = []
    for t in range(steps):
        acc = jnp.broadcast_to(sgb_ref[t:t + 1, :], (nb, d))
        for u in range(t + 1):
            acc = acc + sgw_ref[t * steps + u:t * steps + u + 1, :] * slab(z, u)
        s_slabs.append(acc)
    s = jnp.concatenate(s_slabs, axis=0)
    y_sg = _in_proj(hb, w, COL_U) * s * _silu(_in_proj(hb, w, COL_G_SG))

    x_lru = _in_proj(hb, w, COL_X_LRU)
    hist = [conv_in_ref[:, k, :] for k in range(CONV_WIDTH - 1)]
    hist += [slab(x_lru, t) for t in range(steps)]
    for k in range(CONV_WIDTH - 1):
        conv_out_ref[:, k, :] = hist[steps + k]
    xc_slabs = []
    for t in range(steps):
        acc = w["conv_b"][...] + w["conv_w"][0:1, :] * hist[t]
        for k in range(1, CONV_WIDTH):
            acc = acc + w["conv_w"][k:k + 1, :] * hist[t + k]
        xc_slabs.append(acc)
    xc = jnp.concatenate(xc_slabs, axis=0)
    r_pre, i_pre = _lru_gates(xc.astype(_BF16), w["w_gate"])
    a, bterm = _lru_coeffs(xc, r_pre, i_pre, w)
    h = h0_ref[...]
    h_slabs = []
    for t in range(steps):
        h = slab(a, t) * h + slab(bterm, t)
        h_slabs.append(h)
    lru_out_ref[...] = h
    y_lru = jnp.concatenate(h_slabs, axis=0) * _silu(_in_proj(hb, w, COL_G_LRU))

    y = _post_mix(x, hb, y_sg, y_lru, p, w)
    for t in range(steps):
        y_ref[:, t, :] = slab(y, t)

    for copy in pending:
        copy.wait()


def _sample_call(x, p, conv_in, h0, vectors, sources, sgw_rows, sgb_rows):
    nb, steps, d = x.shape
    vmem = pl.BlockSpec(memory_space=pltpu.VMEM)
    hbm = pl.BlockSpec(memory_space=pl.ANY)
    matrix_shapes = [_MATRIX_SHAPES[m] for m in _MATRIX_NAMES]
    stage_rows = max(s.shape[0] for s in sources)
    out_shape = (
        jax.ShapeDtypeStruct((nb, steps, d), _F32),
        jax.ShapeDtypeStruct((nb, CONV_WIDTH - 1, d), _F32),
        jax.ShapeDtypeStruct((nb, d), _F32),
        jax.ShapeDtypeStruct((nb, steps, d), _F32),
    ) + tuple(jax.ShapeDtypeStruct(shape, _BF16) for shape in matrix_shapes)
    outs = pl.pallas_call(
        functools.partial(_sample_kernel, steps=steps),
        in_specs=[vmem] * (6 + len(vectors)) + [hbm] * len(sources),
        out_specs=(vmem,) * 4 + (hbm,) * len(matrix_shapes),
        out_shape=out_shape,
        scratch_shapes=[pltpu.VMEM(shape, _BF16) for shape in matrix_shapes] + [
            pltpu.VMEM((STAGE_SLOTS, stage_rows, STAGE_COLS), _F32),
            pltpu.SemaphoreType.DMA((STAGE_SLOTS,)),
            pltpu.SemaphoreType.DMA((len(matrix_shapes),)),
        ],
        compiler_params=pltpu.CompilerParams(vmem_limit_bytes=VMEM_LIMIT_BYTES),
        name="sample_layer",
    )(x, p, conv_in, h0, *vectors, sgw_rows, sgb_rows, *sources)
    return outs[:4], list(outs[4:])


def kernel(x_prompt, x_sample, p_prompt, p_sample, state_conv, state_lru, norm_pre, w_in, sg_norm,
           sg_w, sg_b, conv_w, conv_b, lru_wa, lru_ba, lru_wx, lru_bx, lru_lambda, w_branch_sg,
           w_branch_lru, w_merge, b_merge, w_out, norm_post, w_ple, w_ple_gate, b_ple_gate):
    depth = norm_pre.shape[0]
    nb, steps, d = x_sample.shape
    xp, xs = x_prompt, x_sample
    conv_p, lru_p, conv_s, lru_s, chunk_s = [], [], [], [], []
    for l in range(depth):
        row = lambda v: v[l].reshape(1, -1)
        by_name = dict(
            norm_pre=row(norm_pre), sg_norm=row(sg_norm), conv_w=conv_w[l], conv_b=row(conv_b),
            lru_ba=row(lru_ba), lru_bx=row(lru_bx), lru_lambda=row(lru_lambda),
            b_merge=row(b_merge), norm_post=row(norm_post), b_ple_gate=row(b_ple_gate))
        vectors = [by_name[n] for n in _VECTOR_NAMES]
        f32_by_name = dict(
            w_in=w_in[l], w_merge=w_merge[l], w_branch_sg=w_branch_sg[l],
            w_branch_lru=w_branch_lru[l], w_out=w_out[l], w_ple_gate=w_ple_gate[l],
            w_ple=w_ple[l], lru_wa=lru_wa[l].reshape(d, LRU_BLOCK_DIM),
            lru_wx=lru_wx[l].reshape(d, LRU_BLOCK_DIM))
        sources = [f32_by_name[s[0]] for s in _CAST_SOURCES]
        sgw_rows = jnp.repeat(sg_w[l][:, :steps, :steps].reshape(SG_HEADS, steps * steps).T,
                              SG_HEAD_DIM, axis=1)
        sgb_rows = jnp.repeat(sg_b[l][:, :steps].T, SG_HEAD_DIM, axis=1)
        (xs, cs, hs, zs), matrices = _sample_call(
            xs, p_sample[l], state_conv[l], state_lru[l], vectors, sources, sgw_rows, sgb_rows)
        conv_s.append(cs)
        lru_s.append(hs)
        chunk_s.append(zs)
        xp, cp, hp = _prompt_call(xp, p_prompt[l], vectors, matrices, sg_w[l], sg_b[l].T)
        conv_p.append(cp)
        lru_p.append(hp.reshape(-1, d))
    return (xp, xs, jnp.stack(conv_p), jnp.stack(lru_p),
            jnp.stack(conv_s), jnp.stack(lru_s), jnp.stack(chunk_s))
```

```python
import functools

import jax
import jax.numpy as jnp
from jax import lax
from jax.experimental import pallas as pl
from jax.experimental.pallas import tpu as pltpu

D_MODEL = 1024
PLE_DIM = 256
SG_HEADS = 4
SG_HEAD_DIM = D_MODEL // SG_HEADS
CHUNK = 128
LRU_BLOCKS = 8
LRU_BLOCK_DIM = D_MODEL // LRU_BLOCKS
CONV_WIDTH = 4
LRU_C = 8.0
EPS = 1e-6

SUBLANES = 8
PROMPT_TILE = 256
VMEM_LIMIT_BYTES = 56 * 1024 * 1024
STAGE_COLS = 512
STAGE_SLOTS = 4

_BF16 = jnp.bfloat16
_F32 = jnp.float32

_VECTOR_NAMES = ("norm_pre", "sg_norm", "conv_w", "conv_b", "lru_ba", "lru_bx", "lru_lambda",
                 "b_merge", "norm_post", "b_ple_gate")
_MATRIX_SHAPES = dict(
    w_in=(D_MODEL, 5 * D_MODEL), w_merge=(D_MODEL, 2 * D_MODEL),
    w_branch_sg=(D_MODEL, D_MODEL), w_branch_lru=(D_MODEL, D_MODEL), w_out=(D_MODEL, D_MODEL),
    w_ple_gate=(D_MODEL, D_MODEL), w_ple=(PLE_DIM, D_MODEL),
    w_gate=(D_MODEL, 2 * LRU_BLOCK_DIM))
_MATRIX_NAMES = tuple(_MATRIX_SHAPES)

COL_U, COL_V, COL_G_SG, COL_X_LRU, COL_G_LRU = range(5)


def _dot(a, b):
    return jnp.dot(a, b, preferred_element_type=_F32)


def _rms_scale(x):
    var = jnp.mean(x * x, axis=-1, keepdims=True)
    return x * lax.rsqrt(var + EPS)


def _silu(x):
    return x * jax.nn.sigmoid(x)


def _in_proj(hb, w, col):
    return _dot(hb, w["w_in"][:, col * D_MODEL:(col + 1) * D_MODEL])


def _merge_pre(hb, w, half):
    return _dot(hb, w["w_merge"][:, half * D_MODEL:(half + 1) * D_MODEL])


def _head_norm(v, sgn):
    zs = []
    for h in range(SG_HEADS):
        sl = slice(h * SG_HEAD_DIM, (h + 1) * SG_HEAD_DIM)
        zs.append(_rms_scale(v[:, sl]) * sgn[:, sl])
    return jnp.concatenate(zs, axis=-1)


def _lru_gates(xcb, wg_ref):
    gates = []
    for n in range(LRU_BLOCKS):
        sl = slice(n * LRU_BLOCK_DIM, (n + 1) * LRU_BLOCK_DIM)
        gates.append(_dot(xcb[:, sl], wg_ref[sl, :]))
    r_pre = jnp.concatenate([g[:, :LRU_BLOCK_DIM] for g in gates], axis=-1)
    i_pre = jnp.concatenate([g[:, LRU_BLOCK_DIM:] for g in gates], axis=-1)
    return r_pre, i_pre


def _lru_coeffs(xc, r_pre, i_pre, w):
    r = jax.nn.sigmoid(r_pre + w["lru_ba"][...])
    i = jax.nn.sigmoid(i_pre + w["lru_bx"][...])
    log_a = (-LRU_C * jax.nn.softplus(-w["lru_lambda"][...])) * r
    a = jnp.exp(log_a)
    mult = jnp.sqrt(1.0 - a * a)
    return a, mult * (i * xc)


def _merge_gate(pre, w, half):
    return jax.nn.sigmoid(pre + w["b_merge"][:, half * D_MODEL:(half + 1) * D_MODEL])


def _post_mix(x, hb, y_sg, y_lru, p, w):
    g_a = _merge_gate(_merge_pre(hb, w, 0), w, 0)
    merged = g_a * _dot(y_sg.astype(_BF16), w["w_branch_sg"][...])
    g_b = _merge_gate(_merge_pre(hb, w, 1), w, 1)
    merged = merged + g_b * _dot(y_lru.astype(_BF16), w["w_branch_lru"][...])
    o = _dot(merged.astype(_BF16), w["w_out"][...])
    x1 = x + _rms_scale(o) * w["norm_post"][...]
    gate = jax.nn.sigmoid(_dot(x1.astype(_BF16), w["w_ple_gate"][...]) + w["b_ple_gate"][...])
    return x1 + gate * _dot(p.astype(_BF16), w["w_ple"][...])


def _group_scan(a, b, h_prev):
    rows, d = a.shape
    groups = rows // SUBLANES
    a3 = a.reshape(groups, SUBLANES, d)
    b3 = b.reshape(groups, SUBLANES, d)
    row = lax.broadcasted_iota(jnp.int32, a3.shape, 1)
    shift = 1
    while shift < SUBLANES:
        keep = row >= shift
        a_sh = jnp.where(keep, pltpu.roll(a3, shift, 1), 1.0)
        b_sh = jnp.where(keep, pltpu.roll(b3, shift, 1), 0.0)
        b3 = b3 + a3 * b_sh
        a3 = a3 * a_sh
        shift *= 2
    hs = []
    h = h_prev
    for g in range(groups):
        hg = b3[g] + a3[g] * h
        hs.append(hg)
        h = hg[SUBLANES - 1:SUBLANES, :]
    return jnp.concatenate(hs, axis=0), h


def _spatial_mix(z, sgw_ref, sgb_ref):
    tri = (lax.broadcasted_iota(jnp.int32, (CHUNK, CHUNK), 0)
           >= lax.broadcasted_iota(jnp.int32, (CHUNK, CHUNK), 1))
    s_rows = []
    for c in range(z.shape[0] // CHUNK):
        s_heads = []
        for h in range(SG_HEADS):
            wm = jnp.where(tri, sgw_ref[h], 0.0).astype(_BF16)
            zc = z[c * CHUNK:(c + 1) * CHUNK, h * SG_HEAD_DIM:(h + 1) * SG_HEAD_DIM]
            s_heads.append(_dot(wm, zc) + sgb_ref[:, h:h + 1])
        s_rows.append(jnp.concatenate(s_heads, axis=-1))
    return jnp.concatenate(s_rows, axis=0)


def _prompt_kernel(*refs):
    n_w = len(_VECTOR_NAMES) + len(_MATRIX_NAMES)
    x_ref, p_ref = refs[0], refs[1]
    w = dict(zip(_VECTOR_NAMES + _MATRIX_NAMES, refs[2:2 + n_w]))
    sgw_ref, sgb_ref = refs[2 + n_w], refs[3 + n_w]
    y_ref, conv_out_ref, lru_out_ref = refs[4 + n_w:7 + n_w]
    xl_buf, conv_state, h_state = refs[7 + n_w:]

    t = pl.program_id(1)
    tile = x_ref.shape[0]

    @pl.when(t == 0)
    def _():
        conv_state[...] = jnp.zeros_like(conv_state)
        h_state[...] = jnp.zeros_like(h_state)

    x = x_ref[...]
    hb = (_rms_scale(x) * w["norm_pre"][...]).astype(_BF16)

    z = _head_norm(_in_proj(hb, w, COL_V), w["sg_norm"][...]).astype(_BF16)
    s = _spatial_mix(z, sgw_ref, sgb_ref)
    y_sg = _in_proj(hb, w, COL_U) * s * _silu(_in_proj(hb, w, COL_G_SG))

    x_lru = _in_proj(hb, w, COL_X_LRU)
    xl_buf[0:SUBLANES, :] = conv_state[...]
    xl_buf[SUBLANES:, :] = x_lru
    conv_state[...] = xl_buf[tile:tile + SUBLANES, :]
    xc = w["conv_b"][...] + w["conv_w"][CONV_WIDTH - 1:CONV_WIDTH, :] * x_lru
    for j in range(1, CONV_WIDTH):
        k = CONV_WIDTH - 1 - j
        xc = xc + w["conv_w"][k:k + 1, :] * xl_buf[pl.ds(SUBLANES - j, tile), :]
    r_pre, i_pre = _lru_gates(xc.astype(_BF16), w["w_gate"])

    a, bterm = _lru_coeffs(xc, r_pre, i_pre, w)
    hseq, h_last = _group_scan(a, bterm, h_state[...])
    h_state[...] = h_last
    y_lru = hseq * _silu(_in_proj(hb, w, COL_G_LRU))

    y_ref[...] = _post_mix(x, hb, y_sg, y_lru, p_ref[...], w)

    @pl.when(t == pl.num_programs(1) - 1)
    def _():
        seq = pl.ds(pl.program_id(0), 1)
        for k in range(CONV_WIDTH - 1):
            row = tile + SUBLANES - (CONV_WIDTH - 1) + k
            conv_out_ref[k, seq, :] = xl_buf[row:row + 1, :]
        lru_out_ref[seq, :] = h_last


def _resident(shape):
    return pl.BlockSpec(shape, lambda *_: (0,) * len(shape), pipeline_mode=pl.Buffered(1))


def _prompt_call(x, p, vectors, matrices, sgw, sgb_t):
    batch, seq, d = x.shape
    tile = PROMPT_TILE
    grid = (batch, seq // tile)
    in_specs = [
        pl.BlockSpec((None, tile, d), lambda b, t: (b, t, 0)),
        pl.BlockSpec((None, tile, PLE_DIM), lambda b, t: (b, t, 0)),
    ]
    in_specs += [_resident(v.shape) for v in vectors + matrices]
    in_specs += [_resident(sgw.shape), _resident(sgb_t.shape)]
    out_shape = (
        jax.ShapeDtypeStruct((batch, seq, d), _F32),
        jax.ShapeDtypeStruct((CONV_WIDTH - 1, batch, d), _F32),
        jax.ShapeDtypeStruct((batch, d), _F32),
    )
    out_specs = (
        pl.BlockSpec((None, tile, d), lambda b, t: (b, t, 0)),
        pl.BlockSpec((CONV_WIDTH - 1, batch, d), lambda b, t: (0, 0, 0)),
        pl.BlockSpec((batch, d), lambda b, t: (0, 0)),
    )
    return pl.pallas_call(
        _prompt_kernel,
        grid=grid,
        in_specs=in_specs,
        out_specs=out_specs,
        out_shape=out_shape,
        scratch_shapes=[
            pltpu.VMEM((tile + SUBLANES, d), _F32),
            pltpu.VMEM((SUBLANES, d), _F32),
            pltpu.VMEM((1, d), _F32),
        ],
        compiler_params=pltpu.CompilerParams(
            dimension_semantics=("arbitrary", "arbitrary"),
            vmem_limit_bytes=VMEM_LIMIT_BYTES),
        name="prompt_layer",
    )(x, p, *vectors, *matrices, sgw, sgb_t)


_CAST_SOURCES = (
    ("w_in", "w_in", 0), ("w_merge", "w_merge", 0), ("w_branch_sg", "w_branch_sg", 0),
    ("w_branch_lru", "w_branch_lru", 0), ("w_out", "w_out", 0), ("w_ple_gate", "w_ple_gate", 0),
    ("w_ple", "w_ple", 0), ("lru_wa", "w_gate", 0), ("lru_wx", "w_gate", LRU_BLOCK_DIM))


def _cast_weights(src, dst, out, stage, sem_in, sem_out):
    tasks = []
    for name, matrix, col0 in _CAST_SOURCES:
        cols_total = src[name].shape[1]
        for c in range(0, cols_total, STAGE_COLS):
            tasks.append((src[name], matrix, c, col0 + c, min(STAGE_COLS, cols_total - c)))

    def stage_copy(i):
        ref, _, c, _, cols = tasks[i]
        rows = ref.shape[0]
        return pltpu.make_async_copy(
            ref.at[:, pl.ds(c, cols)],
            stage.at[i % STAGE_SLOTS, pl.ds(0, rows), pl.ds(0, cols)],
            sem_in.at[i % STAGE_SLOTS])

    def out_copy(matrix):
        return pltpu.make_async_copy(dst[matrix], out[matrix],
                                     sem_out.at[_MATRIX_NAMES.index(matrix)])

    ahead = STAGE_SLOTS - 1
    for i in range(min(ahead, len(tasks))):
        stage_copy(i).start()
    for i, (ref, matrix, _, dst_col, cols) in enumerate(tasks):
        if i + ahead < len(tasks):
            stage_copy(i + ahead).start()
        stage_copy(i).wait()
        rows = ref.shape[0]
        staged = stage[i % STAGE_SLOTS, 0:rows, 0:cols]
        dst[matrix][:, dst_col:dst_col + cols] = staged.astype(_BF16)
        if i + 1 == len(tasks) or tasks[i + 1][1] != matrix:
            out_copy(matrix).start()
    return [out_copy(matrix) for matrix in _MATRIX_NAMES]


def _sample_kernel(*refs, steps):
    n_v, n_s, n_m = len(_VECTOR_NAMES), len(_CAST_SOURCES), len(_MATRIX_NAMES)
    x_ref, p_ref, conv_in_ref, h0_ref = refs[:4]
    vectors = dict(zip(_VECTOR_NAMES, refs[4:4 + n_v]))
    sgw_ref, sgb_ref = refs[4 + n_v], refs[5 + n_v]
    src = dict(zip([s[0] for s in _CAST_SOURCES], refs[6 + n_v:6 + n_v + n_s]))
    outs = refs[6 + n_v + n_s:]
    y_ref, conv_out_ref, lru_out_ref, z_out_ref = outs[:4]
    out_bf16 = dict(zip(_MATRIX_NAMES, outs[4:4 + n_m]))
    scratch = outs[4 + n_m:]
    w_bf16 = dict(zip(_MATRIX_NAMES, scratch[:n_m]))
    stage, sem_in, sem_out = scratch[n_m:]
    d = D_MODEL
    nb = x_ref.shape[0]

    pending = _cast_weights(src, w_bf16, out_bf16, stage, sem_in, sem_out)
    w = {**vectors, **w_bf16}

    def slab(v, t):
        return v[t * nb:(t + 1) * nb]

    x = jnp.concatenate([x_ref[:, t, :] for t in range(steps)], axis=0)
    p = jnp.concatenate([p_ref[:, t, :] for t in range(steps)], axis=0)
    hb = (_rms_scale(x) * w["norm_pre"][...]).astype(_BF16)

    z = _head_norm(_in_proj(hb, w, COL_V), w["sg_norm"][...])
    for t in range(steps):
        z_out_ref[:, t, :] = slab(z, t)
    s_slabs = []
    for t in range(steps):
        acc = jnp.broadcast_to(sgb_ref[t:t + 1, :], (nb, d))
        for u in range(t + 1):
            acc = acc + sgw_ref[t * steps + u:t * steps + u + 1, :] * slab(z, u)
        s_slabs.append(acc)
    s = jnp.concatenate(s_slabs, axis=0)
    y_sg = _in_proj(hb, w, COL_U) * s * _silu(_in_proj(hb, w, COL_G_SG))

    x_lru = _in_proj(hb, w, COL_X_LRU)
    hist = [conv_in_ref[k] for k in range(CONV_WIDTH - 1)]
    hist += [slab(x_lru, t) for t in range(steps)]
    for k in range(CONV_WIDTH - 1):
        conv_out_ref[k] = hist[steps + k]
    xc_slabs = []
    for t in range(steps):
        acc = w["conv_b"][...] + w["conv_w"][0:1, :] * hist[t]
        for k in range(1, CONV_WIDTH):
            acc = acc + w["conv_w"][k:k + 1, :] * hist[t + k]
        xc_slabs.append(acc)
    xc = jnp.concatenate(xc_slabs, axis=0)
    r_pre, i_pre = _lru_gates(xc.astype(_BF16), w["w_gate"])
    a, bterm = _lru_coeffs(xc, r_pre, i_pre, w)
    h = h0_ref[...]
    h_slabs = []
    for t in range(steps):
        h = slab(a, t) * h + slab(bterm, t)
        h_slabs.append(h)
    lru_out_ref[...] = h
    y_lru = jnp.concatenate(h_slabs, axis=0) * _silu(_in_proj(hb, w, COL_G_LRU))

    y = _post_mix(x, hb, y_sg, y_lru, p, w)
    for t in range(steps):
        y_ref[:, t, :] = slab(y, t)

    for copy in pending:
        copy.wait()


def _sample_call(x, p, conv_in, h0, vectors, sources, sgw_rows, sgb_rows):
    nb, steps, d = x.shape
    vmem = pl.BlockSpec(memory_space=pltpu.VMEM)
    hbm = pl.BlockSpec(memory_space=pl.ANY)
    matrix_shapes = [_MATRIX_SHAPES[m] for m in _MATRIX_NAMES]
    stage_rows = max(s.shape[0] for s in sources)
    out_shape = (
        jax.ShapeDtypeStruct((nb, steps, d), _F32),
        jax.ShapeDtypeStruct((CONV_WIDTH - 1, nb, d), _F32),
        jax.ShapeDtypeStruct((nb, d), _F32),
        jax.ShapeDtypeStruct((nb, steps, d), _F32),
    ) + tuple(jax.ShapeDtypeStruct(shape, _BF16) for shape in matrix_shapes)
    outs = pl.pallas_call(
        functools.partial(_sample_kernel, steps=steps),
        in_specs=[vmem] * (6 + len(vectors)) + [hbm] * len(sources),
        out_specs=(vmem,) * 4 + (hbm,) * len(matrix_shapes),
        out_shape=out_shape,
        scratch_shapes=[pltpu.VMEM(shape, _BF16) for shape in matrix_shapes] + [
            pltpu.VMEM((STAGE_SLOTS, stage_rows, STAGE_COLS), _F32),
            pltpu.SemaphoreType.DMA((STAGE_SLOTS,)),
            pltpu.SemaphoreType.DMA((len(matrix_shapes),)),
        ],
        compiler_params=pltpu.CompilerParams(vmem_limit_bytes=VMEM_LIMIT_BYTES),
        name="sample_layer",
    )(x, p, conv_in, h0, *vectors, sgw_rows, sgb_rows, *sources)
    return outs[:4], list(outs[4:])


def kernel(x_prompt, x_sample, p_prompt, p_sample, state_conv, state_lru, norm_pre, w_in, sg_norm,
           sg_w, sg_b, conv_w, conv_b, lru_wa, lru_ba, lru_wx, lru_bx, lru_lambda, w_branch_sg,
           w_branch_lru, w_merge, b_merge, w_out, norm_post, w_ple, w_ple_gate, b_ple_gate):
    depth = norm_pre.shape[0]
    nb, steps, d = x_sample.shape
    xp, xs = x_prompt, x_sample
    conv_p, lru_p, conv_s, lru_s, chunk_s = [], [], [], [], []
    for l in range(depth):
        row = lambda v: v[l].reshape(1, -1)
        by_name = dict(
            norm_pre=row(norm_pre), sg_norm=row(sg_norm), conv_w=conv_w[l], conv_b=row(conv_b),
            lru_ba=row(lru_ba), lru_bx=row(lru_bx), lru_lambda=row(lru_lambda),
            b_merge=row(b_merge), norm_post=row(norm_post), b_ple_gate=row(b_ple_gate))
        vectors = [by_name[n] for n in _VECTOR_NAMES]
        f32_by_name = dict(
            w_in=w_in[l], w_merge=w_merge[l], w_branch_sg=w_branch_sg[l],
            w_branch_lru=w_branch_lru[l], w_out=w_out[l], w_ple_gate=w_ple_gate[l],
            w_ple=w_ple[l], lru_wa=lru_wa[l].reshape(d, LRU_BLOCK_DIM),
            lru_wx=lru_wx[l].reshape(d, LRU_BLOCK_DIM))
        sources = [f32_by_name[s[0]] for s in _CAST_SOURCES]
        sgw_rows = jnp.repeat(sg_w[l][:, :steps, :steps].reshape(SG_HEADS, steps * steps).T,
                              SG_HEAD_DIM, axis=1)
        sgb_rows = jnp.repeat(sg_b[l][:, :steps].T, SG_HEAD_DIM, axis=1)
        (xs, cs, hs, zs), matrices = _sample_call(
            xs, p_sample[l], jnp.swapaxes(state_conv[l], 0, 1), state_lru[l], vectors, sources,
            sgw_rows, sgb_rows)
        conv_s.append(jnp.swapaxes(cs, 0, 1))
        lru_s.append(hs)
        chunk_s.append(zs)
        xp, cp, hp = _prompt_call(xp, p_prompt[l], vectors, matrices, sg_w[l], sg_b[l].T)
        conv_p.append(jnp.swapaxes(cp, 0, 1))
        lru_p.append(hp)
    return (xp, xs, jnp.stack(conv_p), jnp.stack(lru_p),
            jnp.stack(conv_s), jnp.stack(lru_s), jnp.stack(chunk_s))
```

```python
import functools

import jax
import jax.numpy as jnp
from jax import lax
from jax.experimental import pallas as pl
from jax.experimental.pallas import tpu as pltpu

D_MODEL = 1024
PLE_DIM = 256
SG_HEADS = 4
SG_HEAD_DIM = D_MODEL // SG_HEADS
CHUNK = 128
LRU_BLOCKS = 8
LRU_BLOCK_DIM = D_MODEL // LRU_BLOCKS
CONV_WIDTH = 4
LRU_C = 8.0
EPS = 1e-6

SUBLANES = 8
PROMPT_TILE = 256
VMEM_LIMIT_BYTES = 56 * 1024 * 1024
STAGE_COLS = 512
STAGE_SLOTS = 4
RING_SLOTS = 4

_BF16 = jnp.bfloat16
_F32 = jnp.float32

_VECTOR_NAMES = ("norm_pre", "sg_norm", "conv_w", "conv_b", "lru_ba", "lru_bx", "lru_lambda",
                 "b_merge", "norm_post", "b_ple_gate")
_MATRIX_SHAPES = dict(
    w_in=(D_MODEL, 5 * D_MODEL), w_merge=(D_MODEL, 2 * D_MODEL),
    w_branch_sg=(D_MODEL, D_MODEL), w_branch_lru=(D_MODEL, D_MODEL), w_out=(D_MODEL, D_MODEL),
    w_ple_gate=(D_MODEL, D_MODEL), w_ple=(PLE_DIM, D_MODEL),
    w_gate=(D_MODEL, 2 * LRU_BLOCK_DIM))
_MATRIX_NAMES = tuple(_MATRIX_SHAPES)

COL_U, COL_V, COL_G_SG, COL_X_LRU, COL_G_LRU = range(5)


def _dot(a, b):
    return jnp.dot(a, b, preferred_element_type=_F32)


def _rms_scale(x):
    var = jnp.mean(x * x, axis=-1, keepdims=True)
    return x * lax.rsqrt(var + EPS)


def _silu(x):
    return x * jax.nn.sigmoid(x)


def _in_proj(hb, w, col):
    return _dot(hb, w["w_in"][:, col * D_MODEL:(col + 1) * D_MODEL])


def _merge_pre(hb, w, half):
    return _dot(hb, w["w_merge"][:, half * D_MODEL:(half + 1) * D_MODEL])


def _head_norm(v, sgn):
    zs = []
    for h in range(SG_HEADS):
        sl = slice(h * SG_HEAD_DIM, (h + 1) * SG_HEAD_DIM)
        zs.append(_rms_scale(v[:, sl]) * sgn[:, sl])
    return jnp.concatenate(zs, axis=-1)


def _lru_gates(xcb, wg_ref):
    gates = []
    for n in range(LRU_BLOCKS):
        sl = slice(n * LRU_BLOCK_DIM, (n + 1) * LRU_BLOCK_DIM)
        gates.append(_dot(xcb[:, sl], wg_ref[sl, :]))
    r_pre = jnp.concatenate([g[:, :LRU_BLOCK_DIM] for g in gates], axis=-1)
    i_pre = jnp.concatenate([g[:, LRU_BLOCK_DIM:] for g in gates], axis=-1)
    return r_pre, i_pre


def _lru_coeffs(xc, r_pre, i_pre, w):
    r = jax.nn.sigmoid(r_pre + w["lru_ba"][...])
    i = jax.nn.sigmoid(i_pre + w["lru_bx"][...])
    log_a = (-LRU_C * jax.nn.softplus(-w["lru_lambda"][...])) * r
    a = jnp.exp(log_a)
    mult = jnp.sqrt(1.0 - a * a)
    return a, mult * (i * xc)


def _merge_gate(pre, w, half):
    return jax.nn.sigmoid(pre + w["b_merge"][:, half * D_MODEL:(half + 1) * D_MODEL])


def _post_mix(x, hb, y_sg, y_lru, p, w, mm):
    g_a = _merge_gate(mm["merge_a"](hb), w, 0)
    merged = g_a * mm["branch_sg"](y_sg.astype(_BF16))
    g_b = _merge_gate(mm["merge_b"](hb), w, 1)
    merged = merged + g_b * mm["branch_lru"](y_lru.astype(_BF16))
    o = mm["out"](merged.astype(_BF16))
    x1 = x + _rms_scale(o) * w["norm_post"][...]
    gate = jax.nn.sigmoid(mm["ple_gate"](x1.astype(_BF16)) + w["b_ple_gate"][...])
    return x1 + gate * mm["ple"](p.astype(_BF16))


def _resident_matmuls(w):
    return dict(
        merge_a=lambda lhs: _merge_pre(lhs, w, 0), merge_b=lambda lhs: _merge_pre(lhs, w, 1),
        branch_sg=lambda lhs: _dot(lhs, w["w_branch_sg"][...]),
        branch_lru=lambda lhs: _dot(lhs, w["w_branch_lru"][...]),
        out=lambda lhs: _dot(lhs, w["w_out"][...]),
        ple_gate=lambda lhs: _dot(lhs, w["w_ple_gate"][...]),
        ple=lambda lhs: _dot(lhs, w["w_ple"][...]))


def _group_scan(a, b, h_prev):
    rows, d = a.shape
    groups = rows // SUBLANES
    a3 = a.reshape(groups, SUBLANES, d)
    b3 = b.reshape(groups, SUBLANES, d)
    row = lax.broadcasted_iota(jnp.int32, a3.shape, 1)
    shift = 1
    while shift < SUBLANES:
        keep = row >= shift
        a_sh = jnp.where(keep, pltpu.roll(a3, shift, 1), 1.0)
        b_sh = jnp.where(keep, pltpu.roll(b3, shift, 1), 0.0)
        b3 = b3 + a3 * b_sh
        a3 = a3 * a_sh
        shift *= 2
    hs = []
    h = h_prev
    for g in range(groups):
        hg = b3[g] + a3[g] * h
        hs.append(hg)
        h = hg[SUBLANES - 1:SUBLANES, :]
    return jnp.concatenate(hs, axis=0), h


def _spatial_mix(z, sgw_ref, sgb_ref):
    tri = (lax.broadcasted_iota(jnp.int32, (CHUNK, CHUNK), 0)
           >= lax.broadcasted_iota(jnp.int32, (CHUNK, CHUNK), 1))
    s_rows = []
    for c in range(z.shape[0] // CHUNK):
        s_heads = []
        for h in range(SG_HEADS):
            wm = jnp.where(tri, sgw_ref[h], 0.0).astype(_BF16)
            zc = z[c * CHUNK:(c + 1) * CHUNK, h * SG_HEAD_DIM:(h + 1) * SG_HEAD_DIM]
            s_heads.append(_dot(wm, zc) + sgb_ref[:, h:h + 1])
        s_rows.append(jnp.concatenate(s_heads, axis=-1))
    return jnp.concatenate(s_rows, axis=0)


def _prompt_kernel(*refs):
    n_w = len(_VECTOR_NAMES) + len(_MATRIX_NAMES)
    x_ref, p_ref = refs[0], refs[1]
    w = dict(zip(_VECTOR_NAMES + _MATRIX_NAMES, refs[2:2 + n_w]))
    sgw_ref, sgb_ref = refs[2 + n_w], refs[3 + n_w]
    y_ref, conv_out_ref, lru_out_ref = refs[4 + n_w:7 + n_w]
    xl_buf, conv_state, h_state = refs[7 + n_w:]

    t = pl.program_id(1)
    tile = x_ref.shape[0]

    @pl.when(t == 0)
    def _():
        conv_state[...] = jnp.zeros_like(conv_state)
        h_state[...] = jnp.zeros_like(h_state)

    x = x_ref[...]
    hb = (_rms_scale(x) * w["norm_pre"][...]).astype(_BF16)

    z = _head_norm(_in_proj(hb, w, COL_V), w["sg_norm"][...]).astype(_BF16)
    s = _spatial_mix(z, sgw_ref, sgb_ref)
    y_sg = _in_proj(hb, w, COL_U) * s * _silu(_in_proj(hb, w, COL_G_SG))

    x_lru = _in_proj(hb, w, COL_X_LRU)
    xl_buf[0:SUBLANES, :] = conv_state[...]
    xl_buf[SUBLANES:, :] = x_lru
    conv_state[...] = xl_buf[tile:tile + SUBLANES, :]
    xc = w["conv_b"][...] + w["conv_w"][CONV_WIDTH - 1:CONV_WIDTH, :] * x_lru
    for j in range(1, CONV_WIDTH):
        k = CONV_WIDTH - 1 - j
        xc = xc + w["conv_w"][k:k + 1, :] * xl_buf[pl.ds(SUBLANES - j, tile), :]
    r_pre, i_pre = _lru_gates(xc.astype(_BF16), w["w_gate"])

    a, bterm = _lru_coeffs(xc, r_pre, i_pre, w)
    hseq, h_last = _group_scan(a, bterm, h_state[...])
    h_state[...] = h_last
    y_lru = hseq * _silu(_in_proj(hb, w, COL_G_LRU))

    y_ref[...] = _post_mix(x, hb, y_sg, y_lru, p_ref[...], w, _resident_matmuls(w))

    @pl.when(t == pl.num_programs(1) - 1)
    def _():
        seq = pl.ds(pl.program_id(0), 1)
        for k in range(CONV_WIDTH - 1):
            row = tile + SUBLANES - (CONV_WIDTH - 1) + k
            conv_out_ref[k, seq, :] = xl_buf[row:row + 1, :]
        lru_out_ref[seq, :] = h_last


def _resident(shape):
    return pl.BlockSpec(shape, lambda *_: (0,) * len(shape), pipeline_mode=pl.Buffered(1))


def _prompt_call(x, p, vectors, matrices, sgw, sgb_t):
    batch, seq, d = x.shape
    tile = PROMPT_TILE
    grid = (batch, seq // tile)
    in_specs = [
        pl.BlockSpec((None, tile, d), lambda b, t: (b, t, 0)),
        pl.BlockSpec((None, tile, PLE_DIM), lambda b, t: (b, t, 0)),
    ]
    in_specs += [_resident(v.shape) for v in vectors + matrices]
    in_specs += [_resident(sgw.shape), _resident(sgb_t.shape)]
    out_shape = (
        jax.ShapeDtypeStruct((batch, seq, d), _F32),
        jax.ShapeDtypeStruct((CONV_WIDTH - 1, batch, d), _F32),
        jax.ShapeDtypeStruct((batch, d), _F32),
    )
    out_specs = (
        pl.BlockSpec((None, tile, d), lambda b, t: (b, t, 0)),
        pl.BlockSpec((CONV_WIDTH - 1, batch, d), lambda b, t: (0, 0, 0)),
        pl.BlockSpec((batch, d), lambda b, t: (0, 0)),
    )
    return pl.pallas_call(
        _prompt_kernel,
        grid=grid,
        in_specs=in_specs,
        out_specs=out_specs,
        out_shape=out_shape,
        scratch_shapes=[
            pltpu.VMEM((tile + SUBLANES, d), _F32),
            pltpu.VMEM((SUBLANES, d), _F32),
            pltpu.VMEM((1, d), _F32),
        ],
        compiler_params=pltpu.CompilerParams(
            dimension_semantics=("arbitrary", "arbitrary"),
            vmem_limit_bytes=VMEM_LIMIT_BYTES),
        name="prompt_layer",
    )(x, p, *vectors, *matrices, sgw, sgb_t)


_SOURCE_NAMES = ("w_in", "w_merge", "w_branch_sg", "w_branch_lru", "w_out", "w_ple_gate",
                 "w_ple", "lru_wa", "lru_wx")


def _column_chunks(ref, matrix, first, last):
    return [([(ref, c, min(STAGE_COLS, last - c), 0)], matrix, c)
            for c in range(first, last, STAGE_COLS)]


class _WeightStream:
    def __init__(self, tasks, out, stage, ring, sem_in, sem_out):
        self.tasks, self.out, self.stage, self.ring = tasks, out, stage, ring
        self.sem_in, self.sem_out = sem_in, sem_out
        self.taken = 0
        for i in range(min(STAGE_SLOTS - 1, len(tasks))):
            self._start_in(i)

    def _shape(self, i):
        parts = self.tasks[i][0]
        return parts[0][0].shape[0], sum(part[2] for part in parts)

    def _in_copies(self, i):
        slot = i % STAGE_SLOTS
        rows, _ = self._shape(i)
        return [pltpu.make_async_copy(
            ref.at[:, pl.ds(col, cols)],
            self.stage.at[slot, pl.ds(0, rows), pl.ds(at, cols)],
            self.sem_in.at[slot]) for ref, col, cols, at in self.tasks[i][0]]

    def _start_in(self, i):
        for copy in self._in_copies(i):
            copy.start()

    def _out_copy(self, i):
        slot = i % RING_SLOTS
        rows, cols = self._shape(i)
        _, matrix, first = self.tasks[i]
        return pltpu.make_async_copy(
            self.ring.at[slot, pl.ds(0, rows), pl.ds(0, cols)],
            self.out[matrix].at[:, pl.ds(first, cols)],
            self.sem_out.at[slot])

    def take(self):
        i = self.taken
        self.taken += 1
        if i + STAGE_SLOTS - 1 < len(self.tasks):
            self._start_in(i + STAGE_SLOTS - 1)
        for copy in self._in_copies(i):
            copy.wait()
        if i >= RING_SLOTS:
            self._out_copy(i - RING_SLOTS).wait()
        rows, cols = self._shape(i)
        staged = self.stage[i % STAGE_SLOTS, 0:rows, 0:cols]
        self.ring[i % RING_SLOTS, 0:rows, 0:cols] = staged.astype(_BF16)
        self._out_copy(i).start()
        return self.ring.at[i % RING_SLOTS, pl.ds(0, rows), pl.ds(0, cols)]

    def matmul(self, lhs, chunks):
        return jnp.concatenate([_dot(lhs, self.take()[...]) for _ in range(chunks)], axis=-1)

    def finish(self):
        assert self.taken == len(self.tasks)
        for i in range(max(0, len(self.tasks) - RING_SLOTS), len(self.tasks)):
            self._out_copy(i).wait()


def _sample_tasks(src):
    d = D_MODEL
    half = lambda name, h: _column_chunks(src[name], name, h * d, (h + 1) * d)
    gate = [([(src["lru_wa"], 0, LRU_BLOCK_DIM, 0),
              (src["lru_wx"], 0, LRU_BLOCK_DIM, LRU_BLOCK_DIM)], "w_gate", 0)]
    return (_column_chunks(src["w_in"], "w_in", 0, 5 * d) + gate
            + half("w_merge", 0) + half("w_branch_sg", 0) + half("w_merge", 1)
            + half("w_branch_lru", 0) + half("w_out", 0) + half("w_ple_gate", 0)
            + half("w_ple", 0))


def _sample_kernel(*refs, steps):
    n_v, n_s, n_m = len(_VECTOR_NAMES), len(_SOURCE_NAMES), len(_MATRIX_NAMES)
    x_ref, p_ref, conv_in_ref, h0_ref = refs[:4]
    w = dict(zip(_VECTOR_NAMES, refs[4:4 + n_v]))
    sgw_ref, sgb_ref = refs[4 + n_v], refs[5 + n_v]
    src = dict(zip(_SOURCE_NAMES, refs[6 + n_v:6 + n_v + n_s]))
    outs = refs[6 + n_v + n_s:]
    y_ref, conv_out_ref, lru_out_ref, z_out_ref = outs[:4]
    out_bf16 = dict(zip(_MATRIX_NAMES, outs[4:4 + n_m]))
    stage, ring, sem_in, sem_out = outs[4 + n_m:]
    d = D_MODEL
    nb = x_ref.shape[0]

    stream = _WeightStream(_sample_tasks(src), out_bf16, stage, ring, sem_in, sem_out)
    chunks_per_block = d // STAGE_COLS

    def slab(v, t):
        return v[t * nb:(t + 1) * nb]

    x = jnp.concatenate([x_ref[:, t, :] for t in range(steps)], axis=0)
    p = jnp.concatenate([p_ref[:, t, :] for t in range(steps)], axis=0)
    hb = (_rms_scale(x) * w["norm_pre"][...]).astype(_BF16)
    proj = stream.matmul(hb, 5 * chunks_per_block)
    col = lambda c: proj[:, c * d:(c + 1) * d]

    z = _head_norm(col(COL_V), w["sg_norm"][...])
    for t in range(steps):
        z_out_ref[:, t, :] = slab(z, t)
    s_slabs = []
    for t in range(steps):
        acc = jnp.broadcast_to(sgb_ref[t:t + 1, :], (nb, d))
        for u in range(t + 1):
            acc = acc + sgw_ref[t * steps + u:t * steps + u + 1, :] * slab(z, u)
        s_slabs.append(acc)
    s = jnp.concatenate(s_slabs, axis=0)
    y_sg = col(COL_U) * s * _silu(col(COL_G_SG))

    x_lru = col(COL_X_LRU)
    hist = [conv_in_ref[k] for k in range(CONV_WIDTH - 1)]
    hist += [slab(x_lru, t) for t in range(steps)]
    for k in range(CONV_WIDTH - 1):
        conv_out_ref[k] = hist[steps + k]
    xc_slabs = []
    for t in range(steps):
        acc = w["conv_b"][...] + w["conv_w"][0:1, :] * hist[t]
        for k in range(1, CONV_WIDTH):
            acc = acc + w["conv_w"][k:k + 1, :] * hist[t + k]
        xc_slabs.append(acc)
    xc = jnp.concatenate(xc_slabs, axis=0)
    r_pre, i_pre = _lru_gates(xc.astype(_BF16), stream.take())
    a, bterm = _lru_coeffs(xc, r_pre, i_pre, w)
    h = h0_ref[...]
    h_slabs = []
    for t in range(steps):
        h = slab(a, t) * h + slab(bterm, t)
        h_slabs.append(h)
    lru_out_ref[...] = h
    y_lru = jnp.concatenate(h_slabs, axis=0) * _silu(col(COL_G_LRU))

    block = lambda lhs: stream.matmul(lhs, chunks_per_block)
    mm = {name: block for name in
          ("merge_a", "branch_sg", "merge_b", "branch_lru", "out", "ple_gate", "ple")}
    y = _post_mix(x, hb, y_sg, y_lru, p, w, mm)
    for t in range(steps):
        y_ref[:, t, :] = slab(y, t)

    stream.finish()


def _sample_call(x, p, conv_in, h0, vectors, sources, sgw_rows, sgb_rows):
    nb, steps, d = x.shape
    vmem = pl.BlockSpec(memory_space=pltpu.VMEM)
    hbm = pl.BlockSpec(memory_space=pl.ANY)
    matrix_shapes = [_MATRIX_SHAPES[m] for m in _MATRIX_NAMES]
    stage_rows = max(s.shape[0] for s in sources)
    out_shape = (
        jax.ShapeDtypeStruct((nb, steps, d), _F32),
        jax.ShapeDtypeStruct((CONV_WIDTH - 1, nb, d), _F32),
        jax.ShapeDtypeStruct((nb, d), _F32),
        jax.ShapeDtypeStruct((nb, steps, d), _F32),
    ) + tuple(jax.ShapeDtypeStruct(shape, _BF16) for shape in matrix_shapes)
    outs = pl.pallas_call(
        functools.partial(_sample_kernel, steps=steps),
        in_specs=[vmem] * (6 + len(vectors)) + [hbm] * len(sources),
        out_specs=(vmem,) * 4 + (hbm,) * len(matrix_shapes),
        out_shape=out_shape,
        scratch_shapes=[
            pltpu.VMEM((STAGE_SLOTS, stage_rows, STAGE_COLS), _F32),
            pltpu.VMEM((RING_SLOTS, stage_rows, STAGE_COLS), _BF16),
            pltpu.SemaphoreType.DMA((STAGE_SLOTS,)),
            pltpu.SemaphoreType.DMA((RING_SLOTS,)),
        ],
        compiler_params=pltpu.CompilerParams(vmem_limit_bytes=VMEM_LIMIT_BYTES),
        name="sample_layer",
    )(x, p, conv_in, h0, *vectors, sgw_rows, sgb_rows, *sources)
    return outs[:4], list(outs[4:])


def kernel(x_prompt, x_sample, p_prompt, p_sample, state_conv, state_lru, norm_pre, w_in, sg_norm,
           sg_w, sg_b, conv_w, conv_b, lru_wa, lru_ba, lru_wx, lru_bx, lru_lambda, w_branch_sg,
           w_branch_lru, w_merge, b_merge, w_out, norm_post, w_ple, w_ple_gate, b_ple_gate):
    depth = norm_pre.shape[0]
    nb, steps, d = x_sample.shape
    xp, xs = x_prompt, x_sample
    conv_p, lru_p, conv_s, lru_s, chunk_s = [], [], [], [], []
    for l in range(depth):
        row = lambda v: v[l].reshape(1, -1)
        by_name = dict(
            norm_pre=row(norm_pre), sg_norm=row(sg_norm), conv_w=conv_w[l], conv_b=row(conv_b),
            lru_ba=row(lru_ba), lru_bx=row(lru_bx), lru_lambda=row(lru_lambda),
            b_merge=row(b_merge), norm_post=row(norm_post), b_ple_gate=row(b_ple_gate))
        vectors = [by_name[n] for n in _VECTOR_NAMES]
        f32_by_name = dict(
            w_in=w_in[l], w_merge=w_merge[l], w_branch_sg=w_branch_sg[l],
            w_branch_lru=w_branch_lru[l], w_out=w_out[l], w_ple_gate=w_ple_gate[l],
            w_ple=w_ple[l], lru_wa=lru_wa[l].reshape(d, LRU_BLOCK_DIM),
            lru_wx=lru_wx[l].reshape(d, LRU_BLOCK_DIM))
        sources = [f32_by_name[name] for name in _SOURCE_NAMES]
        sgw_rows = jnp.repeat(sg_w[l][:, :steps, :steps].reshape(SG_HEADS, steps * steps).T,
                              SG_HEAD_DIM, axis=1)
        sgb_rows = jnp.repeat(sg_b[l][:, :steps].T, SG_HEAD_DIM, axis=1)
        (xs, cs, hs, zs), matrices = _sample_call(
            xs, p_sample[l], jnp.swapaxes(state_conv[l], 0, 1), state_lru[l], vectors, sources,
            sgw_rows, sgb_rows)
        conv_s.append(jnp.swapaxes(cs, 0, 1))
        lru_s.append(hs)
        chunk_s.append(zs)
        xp, cp, hp = _prompt_call(xp, p_prompt[l], vectors, matrices, sg_w[l], sg_b[l].T)
        conv_p.append(jnp.swapaxes(cp, 0, 1))
        lru_p.append(hp)
    return (xp, xs, jnp.stack(conv_p), jnp.stack(lru_p),
            jnp.stack(conv_s), jnp.stack(lru_s), jnp.stack(chunk_s))
```

```python
import functools

import jax
import jax.numpy as jnp
from jax import lax
from jax.experimental import pallas as pl
from jax.experimental.pallas import tpu as pltpu

D_MODEL = 1024
PLE_DIM = 256
SG_HEADS = 4
SG_HEAD_DIM = D_MODEL // SG_HEADS
CHUNK = 128
LRU_BLOCKS = 8
LRU_BLOCK_DIM = D_MODEL // LRU_BLOCKS
CONV_WIDTH = 4
LRU_C = 8.0
EPS = 1e-6

SUBLANES = 8
LANES = 128
PROMPT_TILE = 256
VMEM_LIMIT_BYTES = 56 * 1024 * 1024
STAGE_COLS = 512
STAGE_SLOTS = 4
RING_SLOTS = 4

_BF16 = jnp.bfloat16
_F32 = jnp.float32

_VECTOR_NAMES = ("norm_pre", "sg_norm", "conv_w", "conv_b", "lru_ba", "lru_bx", "lru_lambda",
                 "b_merge", "norm_post", "b_ple_gate")
_MATRIX_SHAPES = dict(
    w_in=(D_MODEL, 5 * D_MODEL), w_merge=(D_MODEL, 2 * D_MODEL),
    w_branch_sg=(D_MODEL, D_MODEL), w_branch_lru=(D_MODEL, D_MODEL), w_out=(D_MODEL, D_MODEL),
    w_ple_gate=(D_MODEL, D_MODEL), w_ple=(PLE_DIM, D_MODEL),
    w_gate=(D_MODEL, 2 * LRU_BLOCK_DIM))
_MATRIX_NAMES = tuple(_MATRIX_SHAPES)

COL_U, COL_V, COL_G_SG, COL_X_LRU, COL_G_LRU = range(5)


def _dot(a, b):
    return jnp.dot(a, b, preferred_element_type=_F32)


def _rms_scale(x):
    var = jnp.mean(x * x, axis=-1, keepdims=True)
    return x * lax.rsqrt(var + EPS)


def _silu(x):
    return x * jax.nn.sigmoid(x)


def _in_proj(hb, w, col):
    return _dot(hb, w["w_in"][:, col * D_MODEL:(col + 1) * D_MODEL])


def _merge_pre(hb, w, half):
    return _dot(hb, w["w_merge"][:, half * D_MODEL:(half + 1) * D_MODEL])


def _head_norm(v, sgn):
    zs = []
    for h in range(SG_HEADS):
        sl = slice(h * SG_HEAD_DIM, (h + 1) * SG_HEAD_DIM)
        zs.append(_rms_scale(v[:, sl]) * sgn[:, sl])
    return jnp.concatenate(zs, axis=-1)


def _lru_gates(xcb, wg_ref):
    gates = []
    for n in range(LRU_BLOCKS):
        sl = slice(n * LRU_BLOCK_DIM, (n + 1) * LRU_BLOCK_DIM)
        gates.append(_dot(xcb[:, sl], wg_ref[sl, :]))
    r_pre = jnp.concatenate([g[:, :LRU_BLOCK_DIM] for g in gates], axis=-1)
    i_pre = jnp.concatenate([g[:, LRU_BLOCK_DIM:] for g in gates], axis=-1)
    return r_pre, i_pre


def _lru_coeffs(xc, r_pre, i_pre, w):
    r = jax.nn.sigmoid(r_pre + w["lru_ba"][...])
    i = jax.nn.sigmoid(i_pre + w["lru_bx"][...])
    log_a = (-LRU_C * jax.nn.softplus(-w["lru_lambda"][...])) * r
    a = jnp.exp(log_a)
    mult = jnp.sqrt(1.0 - a * a)
    return a, mult * (i * xc)


def _merge_gate(pre, w, half):
    return jax.nn.sigmoid(pre + w["b_merge"][:, half * D_MODEL:(half + 1) * D_MODEL])


def _post_mix(x, hb, y_sg, y_lru, p, w, mm):
    g_a = _merge_gate(mm["merge_a"](hb), w, 0)
    merged = g_a * mm["branch_sg"](y_sg.astype(_BF16))
    g_b = _merge_gate(mm["merge_b"](hb), w, 1)
    merged = merged + g_b * mm["branch_lru"](y_lru.astype(_BF16))
    o = mm["out"](merged.astype(_BF16))
    x1 = x + _rms_scale(o) * w["norm_post"][...]
    gate = jax.nn.sigmoid(mm["ple_gate"](x1.astype(_BF16)) + w["b_ple_gate"][...])
    return x1 + gate * mm["ple"](p.astype(_BF16))


def _resident_matmuls(w):
    return dict(
        merge_a=lambda lhs: _merge_pre(lhs, w, 0), merge_b=lambda lhs: _merge_pre(lhs, w, 1),
        branch_sg=lambda lhs: _dot(lhs, w["w_branch_sg"][...]),
        branch_lru=lambda lhs: _dot(lhs, w["w_branch_lru"][...]),
        out=lambda lhs: _dot(lhs, w["w_out"][...]),
        ple_gate=lambda lhs: _dot(lhs, w["w_ple_gate"][...]),
        ple=lambda lhs: _dot(lhs, w["w_ple"][...]))


def _run_layout(tile):
    steps = tile // SUBLANES
    return steps, steps + SUBLANES


def _store_runs(buf, value, steps, pitch):
    for l in range(value.shape[1] // LANES):
        for s in range(SUBLANES):
            buf[l, s * pitch:s * pitch + steps, :] = (
                value[s * steps:(s + 1) * steps, l * LANES:(l + 1) * LANES])


def _load_runs(buf, steps, pitch):
    return jnp.concatenate(
        [jnp.concatenate([buf[l, s * pitch:s * pitch + steps, :] for l in range(buf.shape[0])],
                         axis=1) for s in range(SUBLANES)], axis=0)


def _load_step(buf, j, pitch):
    return jnp.concatenate([buf[l, pl.ds(j, SUBLANES, stride=pitch), :]
                            for l in range(buf.shape[0])], axis=1)


def _store_step(buf, j, pitch, slab):
    for l in range(buf.shape[0]):
        buf[l, pl.ds(j, SUBLANES, stride=pitch), :] = slab[:, l * LANES:(l + 1) * LANES]


def _from_previous_run(slab, first_run_rows):
    first = lax.broadcasted_iota(jnp.int32, slab.shape, 0) == 0
    return jnp.where(first, first_run_rows, pltpu.roll(slab, 1, 0))


def _conv_by_step(xs, conv_state, w):
    steps = len(xs)
    last = SUBLANES - 1
    before = [_from_previous_run(xs[steps - k], conv_state[SUBLANES - k:SUBLANES - k + 1, :])
              for k in range(CONV_WIDTH - 1, 0, -1)]
    for k in range(1, CONV_WIDTH):
        conv_state[SUBLANES - k:SUBLANES - k + 1, :] = xs[steps - k][last:, :]
    ext = before + xs
    out = []
    for j in range(steps):
        acc = w["conv_b"][...] + w["conv_w"][0:1, :] * ext[j]
        for k in range(1, CONV_WIDTH):
            acc = acc + w["conv_w"][k:k + 1, :] * ext[j + k]
        out.append(acc)
    return out


def _scan_by_step(a, b, h_prev):
    steps = a.shape[0] // SUBLANES
    slab = lambda v, j: v[j * SUBLANES:(j + 1) * SUBLANES, :]
    h0, decay = [slab(b, 0)], [slab(a, 0)]
    for j in range(1, steps):
        h0.append(slab(a, j) * h0[-1] + slab(b, j))
        decay.append(slab(a, j) * decay[-1])
    end_h, end_decay = h0[-1], decay[-1]
    row = lax.broadcasted_iota(jnp.int32, end_h.shape, 0)
    shift = 1
    while shift < SUBLANES:
        keep = row >= shift
        end_h = end_h + end_decay * jnp.where(keep, pltpu.roll(end_h, shift, 0), 0.0)
        end_decay = end_decay * jnp.where(keep, pltpu.roll(end_decay, shift, 0), 1.0)
        shift *= 2
    ends = end_h + end_decay * h_prev
    entering = _from_previous_run(ends, h_prev)
    hs = [h0[j] + decay[j] * entering for j in range(steps)]
    return hs, ends[SUBLANES - 1:, :]


def _spatial_mix(z, sgw_ref, sgb_ref):
    tri = (lax.broadcasted_iota(jnp.int32, (CHUNK, CHUNK), 0)
           >= lax.broadcasted_iota(jnp.int32, (CHUNK, CHUNK), 1))
    s_rows = []
    for c in range(z.shape[0] // CHUNK):
        s_heads = []
        for h in range(SG_HEADS):
            wm = jnp.where(tri, sgw_ref[h], 0.0).astype(_BF16)
            zc = z[c * CHUNK:(c + 1) * CHUNK, h * SG_HEAD_DIM:(h + 1) * SG_HEAD_DIM]
            s_heads.append(_dot(wm, zc) + sgb_ref[:, h:h + 1])
        s_rows.append(jnp.concatenate(s_heads, axis=-1))
    return jnp.concatenate(s_rows, axis=0)


def _prompt_kernel(*refs):
    n_w = len(_VECTOR_NAMES) + len(_MATRIX_NAMES)
    x_ref, p_ref = refs[0], refs[1]
    w = dict(zip(_VECTOR_NAMES + _MATRIX_NAMES, refs[2:2 + n_w]))
    sgw_ref, sgb_ref = refs[2 + n_w], refs[3 + n_w]
    y_ref, conv_out_ref, lru_out_ref = refs[4 + n_w:7 + n_w]
    runs_buf, conv_state, h_state = refs[7 + n_w:]

    t = pl.program_id(1)
    tile = x_ref.shape[0]

    @pl.when(t == 0)
    def _():
        conv_state[...] = jnp.zeros_like(conv_state)
        h_state[...] = jnp.zeros_like(h_state)

    x = x_ref[...]
    hb = (_rms_scale(x) * w["norm_pre"][...]).astype(_BF16)

    steps, pitch = _run_layout(tile)
    _store_runs(runs_buf, _in_proj(hb, w, COL_X_LRU), steps, pitch)

    v = _in_proj(hb, w, COL_V)

    xs = [_load_step(runs_buf, j, pitch) for j in range(steps)]
    xc = jnp.concatenate(_conv_by_step(xs, conv_state, w), axis=0)
    xcb = xc.astype(_BF16)
    gates, wide = [], []
    for pair, col in enumerate((COL_U, COL_G_SG, COL_G_LRU, None)):
        for n in (2 * pair, 2 * pair + 1):
            sl = slice(n * LRU_BLOCK_DIM, (n + 1) * LRU_BLOCK_DIM)
            gates.append(_dot(xcb[:, sl], w["w_gate"][sl, :]))
        if col is not None:
            wide.append(_in_proj(hb, w, col))
    u, g_sg, g_lru = wide
    r_pre = jnp.concatenate([g[:, :LRU_BLOCK_DIM] for g in gates], axis=-1)
    i_pre = jnp.concatenate([g[:, LRU_BLOCK_DIM:] for g in gates], axis=-1)

    z = _head_norm(v, w["sg_norm"][...]).astype(_BF16)
    s = _spatial_mix(z, sgw_ref, sgb_ref)
    y_sg = u * s * _silu(g_sg)

    a, bterm = _lru_coeffs(xc, r_pre, i_pre, w)
    hs, h_last = _scan_by_step(a, bterm, h_state[...])
    h_state[...] = h_last
    for j in range(steps):
        _store_step(runs_buf, j, pitch, hs[j])
    y_lru = _load_runs(runs_buf, steps, pitch) * _silu(g_lru)

    y_ref[...] = _post_mix(x, hb, y_sg, y_lru, p_ref[...], w, _resident_matmuls(w))

    @pl.when(t == pl.num_programs(1) - 1)
    def _():
        seq = pl.ds(pl.program_id(0), 1)
        for k in range(CONV_WIDTH - 1):
            row = SUBLANES - (CONV_WIDTH - 1) + k
            conv_out_ref[k, seq, :] = conv_state[row:row + 1, :]
        lru_out_ref[seq, :] = h_last


def _resident(shape):
    return pl.BlockSpec(shape, lambda *_: (0,) * len(shape), pipeline_mode=pl.Buffered(1))


def _prompt_call(x, p, vectors, matrices, sgw, sgb_t):
    batch, seq, d = x.shape
    tile = PROMPT_TILE
    grid = (batch, seq // tile)
    in_specs = [
        pl.BlockSpec((None, tile, d), lambda b, t: (b, t, 0)),
        pl.BlockSpec((None, tile, PLE_DIM), lambda b, t: (b, t, 0)),
    ]
    in_specs += [_resident(v.shape) for v in vectors + matrices]
    in_specs += [_resident(sgw.shape), _resident(sgb_t.shape)]
    out_shape = (
        jax.ShapeDtypeStruct((batch, seq, d), _F32),
        jax.ShapeDtypeStruct((CONV_WIDTH - 1, batch, d), _F32),
        jax.ShapeDtypeStruct((batch, d), _F32),
    )
    out_specs = (
        pl.BlockSpec((None, tile, d), lambda b, t: (b, t, 0)),
        pl.BlockSpec((CONV_WIDTH - 1, batch, d), lambda b, t: (0, 0, 0)),
        pl.BlockSpec((batch, d), lambda b, t: (0, 0)),
    )
    return pl.pallas_call(
        _prompt_kernel,
        grid=grid,
        in_specs=in_specs,
        out_specs=out_specs,
        out_shape=out_shape,
        scratch_shapes=[
            pltpu.VMEM((d // LANES, SUBLANES * _run_layout(tile)[1], LANES), _F32),
            pltpu.VMEM((SUBLANES, d), _F32),
            pltpu.VMEM((1, d), _F32),
        ],
        compiler_params=pltpu.CompilerParams(
            dimension_semantics=("arbitrary", "arbitrary"),
            vmem_limit_bytes=VMEM_LIMIT_BYTES),
        name="prompt_layer",
    )(x, p, *vectors, *matrices, sgw, sgb_t)


_SOURCE_NAMES = ("w_in", "w_merge", "w_branch_sg", "w_branch_lru", "w_out", "w_ple_gate",
                 "w_ple", "lru_wa", "lru_wx")


def _column_chunks(ref, matrix, first, last):
    return [([(ref, c, min(STAGE_COLS, last - c), 0)], matrix, c)
            for c in range(first, last, STAGE_COLS)]


class _WeightStream:
    def __init__(self, tasks, out, stage, ring, sem_in, sem_out):
        self.tasks, self.out, self.stage, self.ring = tasks, out, stage, ring
        self.sem_in, self.sem_out = sem_in, sem_out
        self.taken = 0
        for i in range(min(STAGE_SLOTS - 1, len(tasks))):
            self._start_in(i)

    def _shape(self, i):
        parts = self.tasks[i][0]
        return parts[0][0].shape[0], sum(part[2] for part in parts)

    def _in_copies(self, i):
        slot = i % STAGE_SLOTS
        rows, _ = self._shape(i)
        return [pltpu.make_async_copy(
            ref.at[:, pl.ds(col, cols)],
            self.stage.at[slot, pl.ds(0, rows), pl.ds(at, cols)],
            self.sem_in.at[slot]) for ref, col, cols, at in self.tasks[i][0]]

    def _start_in(self, i):
        for copy in self._in_copies(i):
            copy.start()

    def _out_copy(self, i):
        slot = i % RING_SLOTS
        rows, cols = self._shape(i)
        _, matrix, first = self.tasks[i]
        return pltpu.make_async_copy(
            self.ring.at[slot, pl.ds(0, rows), pl.ds(0, cols)],
            self.out[matrix].at[:, pl.ds(first, cols)],
            self.sem_out.at[slot])

    def take(self):
        i = self.taken
        self.taken += 1
        if i + STAGE_SLOTS - 1 < len(self.tasks):
            self._start_in(i + STAGE_SLOTS - 1)
        for copy in self._in_copies(i):
            copy.wait()
        if i >= RING_SLOTS:
            self._out_copy(i - RING_SLOTS).wait()
        rows, cols = self._shape(i)
        staged = self.stage[i % STAGE_SLOTS, 0:rows, 0:cols]
        self.ring[i % RING_SLOTS, 0:rows, 0:cols] = staged.astype(_BF16)
        self._out_copy(i).start()
        return self.ring.at[i % RING_SLOTS, pl.ds(0, rows), pl.ds(0, cols)]

    def matmul(self, lhs, chunks):
        return jnp.concatenate([_dot(lhs, self.take()[...]) for _ in range(chunks)], axis=-1)

    def finish(self):
        assert self.taken == len(self.tasks)
        for i in range(max(0, len(self.tasks) - RING_SLOTS), len(self.tasks)):
            self._out_copy(i).wait()


def _sample_tasks(src):
    d = D_MODEL
    half = lambda name, h: _column_chunks(src[name], name, h * d, (h + 1) * d)
    gate = [([(src["lru_wa"], 0, LRU_BLOCK_DIM, 0),
              (src["lru_wx"], 0, LRU_BLOCK_DIM, LRU_BLOCK_DIM)], "w_gate", 0)]
    return (_column_chunks(src["w_in"], "w_in", 0, 5 * d) + gate
            + half("w_merge", 0) + half("w_branch_sg", 0) + half("w_merge", 1)
            + half("w_branch_lru", 0) + half("w_out", 0) + half("w_ple_gate", 0)
            + half("w_ple", 0))


def _sample_kernel(*refs, steps):
    n_v, n_s, n_m = len(_VECTOR_NAMES), len(_SOURCE_NAMES), len(_MATRIX_NAMES)
    x_ref, p_ref, conv_in_ref, h0_ref = refs[:4]
    w = dict(zip(_VECTOR_NAMES, refs[4:4 + n_v]))
    sgw_ref, sgb_ref = refs[4 + n_v], refs[5 + n_v]
    src = dict(zip(_SOURCE_NAMES, refs[6 + n_v:6 + n_v + n_s]))
    outs = refs[6 + n_v + n_s:]
    y_ref, conv_out_ref, lru_out_ref, z_out_ref = outs[:4]
    out_bf16 = dict(zip(_MATRIX_NAMES, outs[4:4 + n_m]))
    stage, ring, sem_in, sem_out = outs[4 + n_m:]
    d = D_MODEL
    nb = x_ref.shape[0]

    stream = _WeightStream(_sample_tasks(src), out_bf16, stage, ring, sem_in, sem_out)
    chunks_per_block = d // STAGE_COLS

    def slab(v, t):
        return v[t * nb:(t + 1) * nb]

    x = jnp.concatenate([x_ref[:, t, :] for t in range(steps)], axis=0)
    p = jnp.concatenate([p_ref[:, t, :] for t in range(steps)], axis=0)
    hb = (_rms_scale(x) * w["norm_pre"][...]).astype(_BF16)
    proj = stream.matmul(hb, 5 * chunks_per_block)
    col = lambda c: proj[:, c * d:(c + 1) * d]

    z = _head_norm(col(COL_V), w["sg_norm"][...])
    for t in range(steps):
        z_out_ref[:, t, :] = slab(z, t)
    s_slabs = []
    for t in range(steps):
        acc = jnp.broadcast_to(sgb_ref[t:t + 1, :], (nb, d))
        for u in range(t + 1):
            acc = acc + sgw_ref[t * steps + u:t * steps + u + 1, :] * slab(z, u)
        s_slabs.append(acc)
    s = jnp.concatenate(s_slabs, axis=0)
    y_sg = col(COL_U) * s * _silu(col(COL_G_SG))

    x_lru = col(COL_X_LRU)
    hist = [conv_in_ref[k] for k in range(CONV_WIDTH - 1)]
    hist += [slab(x_lru, t) for t in range(steps)]
    for k in range(CONV_WIDTH - 1):
        conv_out_ref[k] = hist[steps + k]
    xc_slabs = []
    for t in range(steps):
        acc = w["conv_b"][...] + w["conv_w"][0:1, :] * hist[t]
        for k in range(1, CONV_WIDTH):
            acc = acc + w["conv_w"][k:k + 1, :] * hist[t + k]
        xc_slabs.append(acc)
    xc = jnp.concatenate(xc_slabs, axis=0)
    r_pre, i_pre = _lru_gates(xc.astype(_BF16), stream.take())
    a, bterm = _lru_coeffs(xc, r_pre, i_pre, w)
    h = h0_ref[...]
    h_slabs = []
    for t in range(steps):
        h = slab(a, t) * h + slab(bterm, t)
        h_slabs.append(h)
    lru_out_ref[...] = h
    y_lru = jnp.concatenate(h_slabs, axis=0) * _silu(col(COL_G_LRU))

    block = lambda lhs: stream.matmul(lhs, chunks_per_block)
    mm = {name: block for name in
          ("merge_a", "branch_sg", "merge_b", "branch_lru", "out", "ple_gate", "ple")}
    y = _post_mix(x, hb, y_sg, y_lru, p, w, mm)
    for t in range(steps):
        y_ref[:, t, :] = slab(y, t)

    stream.finish()


def _sample_call(x, p, conv_in, h0, vectors, sources, sgw_rows, sgb_rows):
    nb, steps, d = x.shape
    vmem = pl.BlockSpec(memory_space=pltpu.VMEM)
    hbm = pl.BlockSpec(memory_space=pl.ANY)
    matrix_shapes = [_MATRIX_SHAPES[m] for m in _MATRIX_NAMES]
    stage_rows = max(s.shape[0] for s in sources)
    out_shape = (
        jax.ShapeDtypeStruct((nb, steps, d), _F32),
        jax.ShapeDtypeStruct((CONV_WIDTH - 1, nb, d), _F32),
        jax.ShapeDtypeStruct((nb, d), _F32),
        jax.ShapeDtypeStruct((nb, steps, d), _F32),
    ) + tuple(jax.ShapeDtypeStruct(shape, _BF16) for shape in matrix_shapes)
    outs = pl.pallas_call(
        functools.partial(_sample_kernel, steps=steps),
        in_specs=[vmem] * (6 + len(vectors)) + [hbm] * len(sources),
        out_specs=(vmem,) * 4 + (hbm,) * len(matrix_shapes),
        out_shape=out_shape,
        scratch_shapes=[
            pltpu.VMEM((STAGE_SLOTS, stage_rows, STAGE_COLS), _F32),
            pltpu.VMEM((RING_SLOTS, stage_rows, STAGE_COLS), _BF16),
            pltpu.SemaphoreType.DMA((STAGE_SLOTS,)),
            pltpu.SemaphoreType.DMA((RING_SLOTS,)),
        ],
        compiler_params=pltpu.CompilerParams(vmem_limit_bytes=VMEM_LIMIT_BYTES),
        name="sample_layer",
    )(x, p, conv_in, h0, *vectors, sgw_rows, sgb_rows, *sources)
    return outs[:4], list(outs[4:])


def kernel(x_prompt, x_sample, p_prompt, p_sample, state_conv, state_lru, norm_pre, w_in, sg_norm,
           sg_w, sg_b, conv_w, conv_b, lru_wa, lru_ba, lru_wx, lru_bx, lru_lambda, w_branch_sg,
           w_branch_lru, w_merge, b_merge, w_out, norm_post, w_ple, w_ple_gate, b_ple_gate):
    depth = norm_pre.shape[0]
    nb, steps, d = x_sample.shape
    xp, xs = x_prompt, x_sample
    conv_p, lru_p, conv_s, lru_s, chunk_s = [], [], [], [], []
    for l in range(depth):
        row = lambda v: v[l].reshape(1, -1)
        by_name = dict(
            norm_pre=row(norm_pre), sg_norm=row(sg_norm), conv_w=conv_w[l], conv_b=row(conv_b),
            lru_ba=row(lru_ba), lru_bx=row(lru_bx), lru_lambda=row(lru_lambda),
            b_merge=row(b_merge), norm_post=row(norm_post), b_ple_gate=row(b_ple_gate))
        vectors = [by_name[n] for n in _VECTOR_NAMES]
        f32_by_name = dict(
            w_in=w_in[l], w_merge=w_merge[l], w_branch_sg=w_branch_sg[l],
            w_branch_lru=w_branch_lru[l], w_out=w_out[l], w_ple_gate=w_ple_gate[l],
            w_ple=w_ple[l], lru_wa=lru_wa[l].reshape(d, LRU_BLOCK_DIM),
            lru_wx=lru_wx[l].reshape(d, LRU_BLOCK_DIM))
        sources = [f32_by_name[name] for name in _SOURCE_NAMES]
        sgw_rows = jnp.repeat(sg_w[l][:, :steps, :steps].reshape(SG_HEADS, steps * steps).T,
                              SG_HEAD_DIM, axis=1)
        sgb_rows = jnp.repeat(sg_b[l][:, :steps].T, SG_HEAD_DIM, axis=1)
        (xs, cs, hs, zs), matrices = _sample_call(
            xs, p_sample[l], jnp.swapaxes(state_conv[l], 0, 1), state_lru[l], vectors, sources,
            sgw_rows, sgb_rows)
        conv_s.append(jnp.swapaxes(cs, 0, 1))
        lru_s.append(hs)
        chunk_s.append(zs)
        xp, cp, hp = _prompt_call(xp, p_prompt[l], vectors, matrices, sg_w[l], sg_b[l].T)
        conv_p.append(jnp.swapaxes(cp, 0, 1))
        lru_p.append(hp)
    return (xp, xs, jnp.stack(conv_p), jnp.stack(lru_p),
            jnp.stack(conv_s), jnp.stack(lru_s), jnp.stack(chunk_s))
```

```python
import functools

import jax
import jax.numpy as jnp
from jax import lax
from jax.experimental import pallas as pl
from jax.experimental.pallas import tpu as pltpu

D_MODEL = 1024
PLE_DIM = 256
SG_HEADS = 4
SG_HEAD_DIM = D_MODEL // SG_HEADS
CHUNK = 128
LRU_BLOCKS = 8
LRU_BLOCK_DIM = D_MODEL // LRU_BLOCKS
CONV_WIDTH = 4
LRU_C = 8.0
EPS = 1e-6

SUBLANES = 8
LANES = 128
PROMPT_TILE = 256
PROMPT_LANES = 2
VMEM_LIMIT_BYTES = 56 * 1024 * 1024
STAGE_COLS = 512
STAGE_SLOTS = 4
RING_SLOTS = 4

_BF16 = jnp.bfloat16
_F32 = jnp.float32

_VECTOR_NAMES = ("norm_pre", "sg_norm", "conv_w", "conv_b", "lru_ba", "lru_bx", "lru_lambda",
                 "b_merge", "norm_post", "b_ple_gate")
_MATRIX_SHAPES = dict(
    w_in=(D_MODEL, 5 * D_MODEL), w_merge=(D_MODEL, 2 * D_MODEL),
    w_branch_sg=(D_MODEL, D_MODEL), w_branch_lru=(D_MODEL, D_MODEL), w_out=(D_MODEL, D_MODEL),
    w_ple_gate=(D_MODEL, D_MODEL), w_ple=(PLE_DIM, D_MODEL),
    w_gate=(D_MODEL, 2 * LRU_BLOCK_DIM))
_MATRIX_NAMES = tuple(_MATRIX_SHAPES)

COL_U, COL_V, COL_G_SG, COL_X_LRU, COL_G_LRU = range(5)


def _dot(a, b):
    return jnp.dot(a, b, preferred_element_type=_F32)


def _rms_scale(x):
    var = jnp.mean(x * x, axis=-1, keepdims=True)
    return x * lax.rsqrt(var + EPS)


def _silu(x):
    return x * jax.nn.sigmoid(x)


def _in_proj(hb, w, col):
    return _dot(hb, w["w_in"][:, col * D_MODEL:(col + 1) * D_MODEL])


def _merge_pre(hb, w, half):
    return _dot(hb, w["w_merge"][:, half * D_MODEL:(half + 1) * D_MODEL])


def _head_norm(v, sgn):
    zs = []
    for h in range(SG_HEADS):
        sl = slice(h * SG_HEAD_DIM, (h + 1) * SG_HEAD_DIM)
        zs.append(_rms_scale(v[:, sl]) * sgn[:, sl])
    return jnp.concatenate(zs, axis=-1)


def _lru_gates(xcb, wg_ref):
    gates = []
    for n in range(LRU_BLOCKS):
        sl = slice(n * LRU_BLOCK_DIM, (n + 1) * LRU_BLOCK_DIM)
        gates.append(_dot(xcb[:, sl], wg_ref[sl, :]))
    r_pre = jnp.concatenate([g[:, :LRU_BLOCK_DIM] for g in gates], axis=-1)
    i_pre = jnp.concatenate([g[:, LRU_BLOCK_DIM:] for g in gates], axis=-1)
    return r_pre, i_pre


def _lru_coeffs(xc, r_pre, i_pre, w):
    r = jax.nn.sigmoid(r_pre + w["lru_ba"][...])
    i = jax.nn.sigmoid(i_pre + w["lru_bx"][...])
    log_a = (-LRU_C * jax.nn.softplus(-w["lru_lambda"][...])) * r
    a = jnp.exp(log_a)
    mult = jnp.sqrt(1.0 - a * a)
    return a, mult * (i * xc)


def _merge_gate(pre, w, half):
    return jax.nn.sigmoid(pre + w["b_merge"][:, half * D_MODEL:(half + 1) * D_MODEL])


def _post_mix(x, hb, y_sg, y_lru, p, w, mm):
    g_a = _merge_gate(mm["merge_a"](hb), w, 0)
    merged = g_a * mm["branch_sg"](y_sg.astype(_BF16))
    g_b = _merge_gate(mm["merge_b"](hb), w, 1)
    merged = merged + g_b * mm["branch_lru"](y_lru.astype(_BF16))
    o = mm["out"](merged.astype(_BF16))
    x1 = x + _rms_scale(o) * w["norm_post"][...]
    gate = jax.nn.sigmoid(mm["ple_gate"](x1.astype(_BF16)) + w["b_ple_gate"][...])
    return x1 + gate * mm["ple"](p.astype(_BF16))


def _resident_matmuls(w):
    return dict(
        merge_a=lambda lhs: _merge_pre(lhs, w, 0), merge_b=lambda lhs: _merge_pre(lhs, w, 1),
        branch_sg=lambda lhs: _dot(lhs, w["w_branch_sg"][...]),
        branch_lru=lambda lhs: _dot(lhs, w["w_branch_lru"][...]),
        out=lambda lhs: _dot(lhs, w["w_out"][...]),
        ple_gate=lambda lhs: _dot(lhs, w["w_ple_gate"][...]),
        ple=lambda lhs: _dot(lhs, w["w_ple"][...]))


def _run_layout(tile):
    steps = tile // SUBLANES
    return steps, steps + SUBLANES


def _store_runs(buf, value, steps, pitch):
    for l in range(value.shape[1] // LANES):
        for s in range(SUBLANES):
            buf[l, s * pitch:s * pitch + steps, :] = (
                value[s * steps:(s + 1) * steps, l * LANES:(l + 1) * LANES])


def _load_runs(buf, steps, pitch):
    return jnp.concatenate(
        [jnp.concatenate([buf[l, s * pitch:s * pitch + steps, :] for l in range(buf.shape[0])],
                         axis=1) for s in range(SUBLANES)], axis=0)


def _load_step(buf, j, pitch):
    return jnp.concatenate([buf[l, pl.ds(j, SUBLANES, stride=pitch), :]
                            for l in range(buf.shape[0])], axis=1)


def _store_step(buf, j, pitch, slab):
    for l in range(buf.shape[0]):
        buf[l, pl.ds(j, SUBLANES, stride=pitch), :] = slab[:, l * LANES:(l + 1) * LANES]


def _from_previous_run(slab, first_run_rows):
    first = lax.broadcasted_iota(jnp.int32, slab.shape, 0) == 0
    return jnp.where(first, first_run_rows, pltpu.roll(slab, 1, 0))


def _conv_by_step(xs, conv_state, w):
    steps = len(xs)
    last = SUBLANES - 1
    before = [_from_previous_run(xs[steps - k], conv_state[SUBLANES - k:SUBLANES - k + 1, :])
              for k in range(CONV_WIDTH - 1, 0, -1)]
    for k in range(1, CONV_WIDTH):
        conv_state[SUBLANES - k:SUBLANES - k + 1, :] = xs[steps - k][last:, :]
    ext = before + xs
    out = []
    for j in range(steps):
        acc = w["conv_b"][...] + w["conv_w"][0:1, :] * ext[j]
        for k in range(1, CONV_WIDTH):
            acc = acc + w["conv_w"][k:k + 1, :] * ext[j + k]
        out.append(acc)
    return out


def _scan_by_step(a, b, h_prev):
    steps = a.shape[0] // SUBLANES
    slab = lambda v, j: v[j * SUBLANES:(j + 1) * SUBLANES, :]
    h0, decay = [slab(b, 0)], [slab(a, 0)]
    for j in range(1, steps):
        h0.append(slab(a, j) * h0[-1] + slab(b, j))
        decay.append(slab(a, j) * decay[-1])
    end_h, end_decay = h0[-1], decay[-1]
    row = lax.broadcasted_iota(jnp.int32, end_h.shape, 0)
    shift = 1
    while shift < SUBLANES:
        keep = row >= shift
        end_h = end_h + end_decay * jnp.where(keep, pltpu.roll(end_h, shift, 0), 0.0)
        end_decay = end_decay * jnp.where(keep, pltpu.roll(end_decay, shift, 0), 1.0)
        shift *= 2
    ends = end_h + end_decay * h_prev
    entering = _from_previous_run(ends, h_prev)
    hs = [h0[j] + decay[j] * entering for j in range(steps)]
    return hs, ends[SUBLANES - 1:, :]


def _spatial_mix(z, sgw_ref, sgb_ref):
    tri = (lax.broadcasted_iota(jnp.int32, (CHUNK, CHUNK), 0)
           >= lax.broadcasted_iota(jnp.int32, (CHUNK, CHUNK), 1))
    s_rows = []
    for c in range(z.shape[0] // CHUNK):
        s_heads = []
        for h in range(SG_HEADS):
            wm = jnp.where(tri, sgw_ref[h], 0.0).astype(_BF16)
            zc = z[c * CHUNK:(c + 1) * CHUNK, h * SG_HEAD_DIM:(h + 1) * SG_HEAD_DIM]
            s_heads.append(_dot(wm, zc) + sgb_ref[:, h:h + 1])
        s_rows.append(jnp.concatenate(s_heads, axis=-1))
    return jnp.concatenate(s_rows, axis=0)


def _tile_layer(x, p, runs_buf, conv_state, h_state, w, sgw_ref, sgb_ref):
    tile = x.shape[0]
    hb = (_rms_scale(x) * w["norm_pre"][...]).astype(_BF16)

    steps, pitch = _run_layout(tile)
    _store_runs(runs_buf, _in_proj(hb, w, COL_X_LRU), steps, pitch)

    v = _in_proj(hb, w, COL_V)

    xs = [_load_step(runs_buf, j, pitch) for j in range(steps)]
    xc = jnp.concatenate(_conv_by_step(xs, conv_state, w), axis=0)
    xcb = xc.astype(_BF16)
    gates, wide = [], []
    for pair, col in enumerate((COL_U, COL_G_SG, COL_G_LRU, None)):
        for n in (2 * pair, 2 * pair + 1):
            sl = slice(n * LRU_BLOCK_DIM, (n + 1) * LRU_BLOCK_DIM)
            gates.append(_dot(xcb[:, sl], w["w_gate"][sl, :]))
        if col is not None:
            wide.append(_in_proj(hb, w, col))
    u, g_sg, g_lru = wide
    r_pre = jnp.concatenate([g[:, :LRU_BLOCK_DIM] for g in gates], axis=-1)
    i_pre = jnp.concatenate([g[:, LRU_BLOCK_DIM:] for g in gates], axis=-1)

    z = _head_norm(v, w["sg_norm"][...]).astype(_BF16)
    s = _spatial_mix(z, sgw_ref, sgb_ref)
    y_sg = u * s * _silu(g_sg)

    a, bterm = _lru_coeffs(xc, r_pre, i_pre, w)
    hs, h_last = _scan_by_step(a, bterm, h_state[...])
    h_state[...] = h_last
    for j in range(steps):
        _store_step(runs_buf, j, pitch, hs[j])
    y_lru = _load_runs(runs_buf, steps, pitch) * _silu(g_lru)

    return _post_mix(x, hb, y_sg, y_lru, p, w, _resident_matmuls(w)), h_last


def _prompt_kernel(*refs):
    n_w = len(_VECTOR_NAMES) + len(_MATRIX_NAMES)
    x_ref, p_ref = refs[0], refs[1]
    w = dict(zip(_VECTOR_NAMES + _MATRIX_NAMES, refs[2:2 + n_w]))
    sgw_ref, sgb_ref = refs[2 + n_w], refs[3 + n_w]
    y_ref, conv_out_ref, lru_out_ref = refs[4 + n_w:7 + n_w]
    runs_buf, conv_state, h_state = refs[7 + n_w:]

    t = pl.program_id(1)
    lanes = x_ref.shape[0]

    @pl.when(t == 0)
    def _():
        conv_state[...] = jnp.zeros_like(conv_state)
        h_state[...] = jnp.zeros_like(h_state)

    h_last = []
    for i in range(lanes):
        y, h = _tile_layer(x_ref[i], p_ref[i], runs_buf.at[i], conv_state.at[i], h_state.at[i],
                           w, sgw_ref, sgb_ref)
        y_ref[i] = y
        h_last.append(h)

    @pl.when(t == pl.num_programs(1) - 1)
    def _():
        for i in range(lanes):
            seq = pl.ds(i * pl.num_programs(0) + pl.program_id(0), 1)
            for k in range(CONV_WIDTH - 1):
                row = SUBLANES - (CONV_WIDTH - 1) + k
                conv_out_ref[k, seq, :] = conv_state[i, row:row + 1, :]
            lru_out_ref[seq, :] = h_last[i]


def _resident(shape):
    return pl.BlockSpec(shape, lambda *_: (0,) * len(shape), pipeline_mode=pl.Buffered(1))


def _prompt_call(x, p, vectors, matrices, sgw, sgb_t):
    batch, seq, d = x.shape
    tile = PROMPT_TILE
    lanes = PROMPT_LANES
    grid = (batch // lanes, seq // tile)
    x4 = x.reshape(lanes, batch // lanes, seq, d)
    p4 = p.reshape(lanes, batch // lanes, seq, PLE_DIM)
    in_specs = [
        pl.BlockSpec((lanes, None, tile, d), lambda b, t: (0, b, t, 0)),
        pl.BlockSpec((lanes, None, tile, PLE_DIM), lambda b, t: (0, b, t, 0)),
    ]
    in_specs += [_resident(v.shape) for v in vectors + matrices]
    in_specs += [_resident(sgw.shape), _resident(sgb_t.shape)]
    out_shape = (
        jax.ShapeDtypeStruct((lanes, batch // lanes, seq, d), _F32),
        jax.ShapeDtypeStruct((CONV_WIDTH - 1, batch, d), _F32),
        jax.ShapeDtypeStruct((batch, d), _F32),
    )
    out_specs = (
        pl.BlockSpec((lanes, None, tile, d), lambda b, t: (0, b, t, 0)),
        pl.BlockSpec((CONV_WIDTH - 1, batch, d), lambda b, t: (0, 0, 0)),
        pl.BlockSpec((batch, d), lambda b, t: (0, 0)),
    )
    y, conv_rows, state = pl.pallas_call(
        _prompt_kernel,
        grid=grid,
        in_specs=in_specs,
        out_specs=out_specs,
        out_shape=out_shape,
        scratch_shapes=[
            pltpu.VMEM((lanes, d // LANES, SUBLANES * _run_layout(tile)[1], LANES), _F32),
            pltpu.VMEM((lanes, SUBLANES, d), _F32),
            pltpu.VMEM((lanes, 1, d), _F32),
        ],
        compiler_params=pltpu.CompilerParams(
            dimension_semantics=("arbitrary", "arbitrary"),
            vmem_limit_bytes=VMEM_LIMIT_BYTES),
        name="prompt_layer",
    )(x4, p4, *vectors, *matrices, sgw, sgb_t)
    return y.reshape(batch, seq, d), conv_rows, state


_SOURCE_NAMES = ("w_in", "w_merge", "w_branch_sg", "w_branch_lru", "w_out", "w_ple_gate",
                 "w_ple", "lru_wa", "lru_wx")


def _column_chunks(ref, matrix, first, last):
    return [([(ref, c, min(STAGE_COLS, last - c), 0)], matrix, c)
            for c in range(first, last, STAGE_COLS)]


class _WeightStream:
    def __init__(self, tasks, out, stage, ring, sem_in, sem_out):
        self.tasks, self.out, self.stage, self.ring = tasks, out, stage, ring
        self.sem_in, self.sem_out = sem_in, sem_out
        self.taken = 0
        for i in range(min(STAGE_SLOTS - 1, len(tasks))):
            self._start_in(i)

    def _shape(self, i):
        parts = self.tasks[i][0]
        return parts[0][0].shape[0], sum(part[2] for part in parts)

    def _in_copies(self, i):
        slot = i % STAGE_SLOTS
        rows, _ = self._shape(i)
        return [pltpu.make_async_copy(
            ref.at[:, pl.ds(col, cols)],
            self.stage.at[slot, pl.ds(0, rows), pl.ds(at, cols)],
            self.sem_in.at[slot]) for ref, col, cols, at in self.tasks[i][0]]

    def _start_in(self, i):
        for copy in self._in_copies(i):
            copy.start()

    def _out_copy(self, i):
        slot = i % RING_SLOTS
        rows, cols = self._shape(i)
        _, matrix, first = self.tasks[i]
        return pltpu.make_async_copy(
            self.ring.at[slot, pl.ds(0, rows), pl.ds(0, cols)],
            self.out[matrix].at[:, pl.ds(first, cols)],
            self.sem_out.at[slot])

    def take(self):
        i = self.taken
        self.taken += 1
        if i + STAGE_SLOTS - 1 < len(self.tasks):
            self._start_in(i + STAGE_SLOTS - 1)
        for copy in self._in_copies(i):
            copy.wait()
        if i >= RING_SLOTS:
            self._out_copy(i - RING_SLOTS).wait()
        rows, cols = self._shape(i)
        staged = self.stage[i % STAGE_SLOTS, 0:rows, 0:cols]
        self.ring[i % RING_SLOTS, 0:rows, 0:cols] = staged.astype(_BF16)
        self._out_copy(i).start()
        return self.ring.at[i % RING_SLOTS, pl.ds(0, rows), pl.ds(0, cols)]

    def matmul(self, lhs, chunks):
        return jnp.concatenate([_dot(lhs, self.take()[...]) for _ in range(chunks)], axis=-1)

    def finish(self):
        assert self.taken == len(self.tasks)
        for i in range(max(0, len(self.tasks) - RING_SLOTS), len(self.tasks)):
            self._out_copy(i).wait()


def _sample_tasks(src):
    d = D_MODEL
    half = lambda name, h: _column_chunks(src[name], name, h * d, (h + 1) * d)
    gate = [([(src["lru_wa"], 0, LRU_BLOCK_DIM, 0),
              (src["lru_wx"], 0, LRU_BLOCK_DIM, LRU_BLOCK_DIM)], "w_gate", 0)]
    return (_column_chunks(src["w_in"], "w_in", 0, 5 * d) + gate
            + half("w_merge", 0) + half("w_branch_sg", 0) + half("w_merge", 1)
            + half("w_branch_lru", 0) + half("w_out", 0) + half("w_ple_gate", 0)
            + half("w_ple", 0))


def _sample_kernel(*refs, steps):
    n_v, n_s, n_m = len(_VECTOR_NAMES), len(_SOURCE_NAMES), len(_MATRIX_NAMES)
    x_ref, p_ref, conv_in_ref, h0_ref = refs[:4]
    w = dict(zip(_VECTOR_NAMES, refs[4:4 + n_v]))
    sgw_ref, sgb_ref = refs[4 + n_v], refs[5 + n_v]
    src = dict(zip(_SOURCE_NAMES, refs[6 + n_v:6 + n_v + n_s]))
    outs = refs[6 + n_v + n_s:]
    y_ref, conv_out_ref, lru_out_ref, z_out_ref = outs[:4]
    out_bf16 = dict(zip(_MATRIX_NAMES, outs[4:4 + n_m]))
    stage, ring, sem_in, sem_out = outs[4 + n_m:]
    d = D_MODEL
    nb = x_ref.shape[0]

    stream = _WeightStream(_sample_tasks(src), out_bf16, stage, ring, sem_in, sem_out)
    chunks_per_block = d // STAGE_COLS

    def slab(v, t):
        return v[t * nb:(t + 1) * nb]

    x = jnp.concatenate([x_ref[:, t, :] for t in range(steps)], axis=0)
    p = jnp.concatenate([p_ref[:, t, :] for t in range(steps)], axis=0)
    hb = (_rms_scale(x) * w["norm_pre"][...]).astype(_BF16)
    proj = stream.matmul(hb, 5 * chunks_per_block)
    col = lambda c: proj[:, c * d:(c + 1) * d]

    z = _head_norm(col(COL_V), w["sg_norm"][...])
    for t in range(steps):
        z_out_ref[:, t, :] = slab(z, t)
    s_slabs = []
    for t in range(steps):
        acc = jnp.broadcast_to(sgb_ref[t:t + 1, :], (nb, d))
        for u in range(t + 1):
            acc = acc + sgw_ref[t * steps + u:t * steps + u + 1, :] * slab(z, u)
        s_slabs.append(acc)
    s = jnp.concatenate(s_slabs, axis=0)
    y_sg = col(COL_U) * s * _silu(col(COL_G_SG))

    x_lru = col(COL_X_LRU)
    hist = [conv_in_ref[k] for k in range(CONV_WIDTH - 1)]
    hist += [slab(x_lru, t) for t in range(steps)]
    for k in range(CONV_WIDTH - 1):
        conv_out_ref[k] = hist[steps + k]
    xc_slabs = []
    for t in range(steps):
        acc = w["conv_b"][...] + w["conv_w"][0:1, :] * hist[t]
        for k in range(1, CONV_WIDTH):
            acc = acc + w["conv_w"][k:k + 1, :] * hist[t + k]
        xc_slabs.append(acc)
    xc = jnp.concatenate(xc_slabs, axis=0)
    r_pre, i_pre = _lru_gates(xc.astype(_BF16), stream.take())
    a, bterm = _lru_coeffs(xc, r_pre, i_pre, w)
    h = h0_ref[...]
    h_slabs = []
    for t in range(steps):
        h = slab(a, t) * h + slab(bterm, t)
        h_slabs.append(h)
    lru_out_ref[...] = h
    y_lru = jnp.concatenate(h_slabs, axis=0) * _silu(col(COL_G_LRU))

    block = lambda lhs: stream.matmul(lhs, chunks_per_block)
    mm = {name: block for name in
          ("merge_a", "branch_sg", "merge_b", "branch_lru", "out", "ple_gate", "ple")}
    y = _post_mix(x, hb, y_sg, y_lru, p, w, mm)
    for t in range(steps):
        y_ref[:, t, :] = slab(y, t)

    stream.finish()


def _sample_call(x, p, conv_in, h0, vectors, sources, sgw_rows, sgb_rows):
    nb, steps, d = x.shape
    vmem = pl.BlockSpec(memory_space=pltpu.VMEM)
    hbm = pl.BlockSpec(memory_space=pl.ANY)
    matrix_shapes = [_MATRIX_SHAPES[m] for m in _MATRIX_NAMES]
    stage_rows = max(s.shape[0] for s in sources)
    out_shape = (
        jax.ShapeDtypeStruct((nb, steps, d), _F32),
        jax.ShapeDtypeStruct((CONV_WIDTH - 1, nb, d), _F32),
        jax.ShapeDtypeStruct((nb, d), _F32),
        jax.ShapeDtypeStruct((nb, steps, d), _F32),
    ) + tuple(jax.ShapeDtypeStruct(shape, _BF16) for shape in matrix_shapes)
    outs = pl.pallas_call(
        functools.partial(_sample_kernel, steps=steps),
        in_specs=[vmem] * (6 + len(vectors)) + [hbm] * len(sources),
        out_specs=(vmem,) * 4 + (hbm,) * len(matrix_shapes),
        out_shape=out_shape,
        scratch_shapes=[
            pltpu.VMEM((STAGE_SLOTS, stage_rows, STAGE_COLS), _F32),
            pltpu.VMEM((RING_SLOTS, stage_rows, STAGE_COLS), _BF16),
            pltpu.SemaphoreType.DMA((STAGE_SLOTS,)),
            pltpu.SemaphoreType.DMA((RING_SLOTS,)),
        ],
        compiler_params=pltpu.CompilerParams(vmem_limit_bytes=VMEM_LIMIT_BYTES),
        name="sample_layer",
    )(x, p, conv_in, h0, *vectors, sgw_rows, sgb_rows, *sources)
    return outs[:4], list(outs[4:])


def kernel(x_prompt, x_sample, p_prompt, p_sample, state_conv, state_lru, norm_pre, w_in, sg_norm,
           sg_w, sg_b, conv_w, conv_b, lru_wa, lru_ba, lru_wx, lru_bx, lru_lambda, w_branch_sg,
           w_branch_lru, w_merge, b_merge, w_out, norm_post, w_ple, w_ple_gate, b_ple_gate):
    depth = norm_pre.shape[0]
    nb, steps, d = x_sample.shape
    xp, xs = x_prompt, x_sample
    conv_p, lru_p, conv_s, lru_s, chunk_s = [], [], [], [], []
    for l in range(depth):
        row = lambda v: v[l].reshape(1, -1)
        by_name = dict(
            norm_pre=row(norm_pre), sg_norm=row(sg_norm), conv_w=conv_w[l], conv_b=row(conv_b),
            lru_ba=row(lru_ba), lru_bx=row(lru_bx), lru_lambda=row(lru_lambda),
            b_merge=row(b_merge), norm_post=row(norm_post), b_ple_gate=row(b_ple_gate))
        vectors = [by_name[n] for n in _VECTOR_NAMES]
        f32_by_name = dict(
            w_in=w_in[l], w_merge=w_merge[l], w_branch_sg=w_branch_sg[l],
            w_branch_lru=w_branch_lru[l], w_out=w_out[l], w_ple_gate=w_ple_gate[l],
            w_ple=w_ple[l], lru_wa=lru_wa[l].reshape(d, LRU_BLOCK_DIM),
            lru_wx=lru_wx[l].reshape(d, LRU_BLOCK_DIM))
        sources = [f32_by_name[name] for name in _SOURCE_NAMES]
        sgw_rows = jnp.repeat(sg_w[l][:, :steps, :steps].reshape(SG_HEADS, steps * steps).T,
                              SG_HEAD_DIM, axis=1)
        sgb_rows = jnp.repeat(sg_b[l][:, :steps].T, SG_HEAD_DIM, axis=1)
        (xs, cs, hs, zs), matrices = _sample_call(
            xs, p_sample[l], jnp.swapaxes(state_conv[l], 0, 1), state_lru[l], vectors, sources,
            sgw_rows, sgb_rows)
        conv_s.append(jnp.swapaxes(cs, 0, 1))
        lru_s.append(hs)
        chunk_s.append(zs)
        xp, cp, hp = _prompt_call(xp, p_prompt[l], vectors, matrices, sg_w[l], sg_b[l].T)
        conv_p.append(jnp.swapaxes(cp, 0, 1))
        lru_p.append(hp)
    return (xp, xs, jnp.stack(conv_p), jnp.stack(lru_p),
            jnp.stack(conv_s), jnp.stack(lru_s), jnp.stack(chunk_s))
```

```python
import functools

import jax
import jax.numpy as jnp
from jax import lax
from jax.experimental import pallas as pl
from jax.experimental.pallas import tpu as pltpu

D_MODEL = 1024
PLE_DIM = 256
SG_HEADS = 4
SG_HEAD_DIM = D_MODEL // SG_HEADS
CHUNK = 128
LRU_BLOCKS = 8
LRU_BLOCK_DIM = D_MODEL // LRU_BLOCKS
CONV_WIDTH = 4
LRU_C = 8.0
EPS = 1e-6

SUBLANES = 8
LANES = 128
PROMPT_TILE = 256
PROMPT_LANES = 1
VMEM_LIMIT_BYTES = 56 * 1024 * 1024
STAGE_COLS = 512
STAGE_SLOTS = 4
RING_SLOTS = 4

_BF16 = jnp.bfloat16
_F32 = jnp.float32

_VECTOR_NAMES = ("norm_pre", "sg_norm", "conv_w", "conv_b", "lru_ba", "lru_bx", "lru_lambda",
                 "b_merge", "norm_post", "b_ple_gate")
_MATRIX_SHAPES = dict(
    w_in=(D_MODEL, 5 * D_MODEL), w_merge=(D_MODEL, 2 * D_MODEL),
    w_branch_sg=(D_MODEL, D_MODEL), w_branch_lru=(D_MODEL, D_MODEL), w_out=(D_MODEL, D_MODEL),
    w_ple_gate=(D_MODEL, D_MODEL), w_ple=(PLE_DIM, D_MODEL),
    w_gate=(D_MODEL, 2 * LRU_BLOCK_DIM))
_MATRIX_NAMES = tuple(_MATRIX_SHAPES)

COL_U, COL_V, COL_G_SG, COL_X_LRU, COL_G_LRU = range(5)


def _dot(a, b):
    return jnp.dot(a, b, preferred_element_type=_F32)


def _rms_scale(x):
    var = jnp.mean(x * x, axis=-1, keepdims=True)
    return x * lax.rsqrt(var + EPS)


def _silu(x):
    return x * jax.nn.sigmoid(x)


def _in_proj(hb, w, col):
    return _dot(hb, w["w_in"][:, col * D_MODEL:(col + 1) * D_MODEL])


def _merge_pre(hb, w, half):
    return _dot(hb, w["w_merge"][:, half * D_MODEL:(half + 1) * D_MODEL])


def _head_norm(v, sgn):
    zs = []
    for h in range(SG_HEADS):
        sl = slice(h * SG_HEAD_DIM, (h + 1) * SG_HEAD_DIM)
        zs.append(_rms_scale(v[:, sl]) * sgn[:, sl])
    return jnp.concatenate(zs, axis=-1)


def _lru_gates(xcb, wg_ref):
    gates = []
    for n in range(LRU_BLOCKS):
        sl = slice(n * LRU_BLOCK_DIM, (n + 1) * LRU_BLOCK_DIM)
        gates.append(_dot(xcb[:, sl], wg_ref[sl, :]))
    r_pre = jnp.concatenate([g[:, :LRU_BLOCK_DIM] for g in gates], axis=-1)
    i_pre = jnp.concatenate([g[:, LRU_BLOCK_DIM:] for g in gates], axis=-1)
    return r_pre, i_pre


def _lru_coeffs(xc, r_pre, i_pre, w):
    r = jax.nn.sigmoid(r_pre + w["lru_ba"][...])
    i = jax.nn.sigmoid(i_pre + w["lru_bx"][...])
    log_a = (-LRU_C * jax.nn.softplus(-w["lru_lambda"][...])) * r
    a = jnp.exp(log_a)
    mult = jnp.sqrt(1.0 - a * a)
    return a, mult * (i * xc)


def _merge_gate(pre, w, half):
    return jax.nn.sigmoid(pre + w["b_merge"][:, half * D_MODEL:(half + 1) * D_MODEL])


def _post_mix(x, hb, y_sg, y_lru, p, w, mm):
    g_a = _merge_gate(mm["merge_a"](hb), w, 0)
    merged = g_a * mm["branch_sg"](y_sg.astype(_BF16))
    g_b = _merge_gate(mm["merge_b"](hb), w, 1)
    merged = merged + g_b * mm["branch_lru"](y_lru.astype(_BF16))
    o = mm["out"](merged.astype(_BF16))
    x1 = x + _rms_scale(o) * w["norm_post"][...]
    gate = jax.nn.sigmoid(mm["ple_gate"](x1.astype(_BF16)) + w["b_ple_gate"][...])
    return x1 + gate * mm["ple"](p.astype(_BF16))


def _resident_matmuls(w):
    return dict(
        merge_a=lambda lhs: _merge_pre(lhs, w, 0), merge_b=lambda lhs: _merge_pre(lhs, w, 1),
        branch_sg=lambda lhs: _dot(lhs, w["w_branch_sg"][...]),
        branch_lru=lambda lhs: _dot(lhs, w["w_branch_lru"][...]),
        out=lambda lhs: _dot(lhs, w["w_out"][...]),
        ple_gate=lambda lhs: _dot(lhs, w["w_ple_gate"][...]),
        ple=lambda lhs: _dot(lhs, w["w_ple"][...]))


def _run_layout(tile):
    steps = tile // SUBLANES
    return steps, steps + SUBLANES


def _store_runs(buf, value, steps, pitch):
    for l in range(value.shape[1] // LANES):
        for s in range(SUBLANES):
            buf[l, s * pitch:s * pitch + steps, :] = (
                value[s * steps:(s + 1) * steps, l * LANES:(l + 1) * LANES])


def _load_runs(buf, steps, pitch):
    return jnp.concatenate(
        [jnp.concatenate([buf[l, s * pitch:s * pitch + steps, :] for l in range(buf.shape[0])],
                         axis=1) for s in range(SUBLANES)], axis=0)


def _load_step(buf, j, pitch):
    return jnp.concatenate([buf[l, pl.ds(j, SUBLANES, stride=pitch), :]
                            for l in range(buf.shape[0])], axis=1)


def _store_step(buf, j, pitch, slab):
    for l in range(buf.shape[0]):
        buf[l, pl.ds(j, SUBLANES, stride=pitch), :] = slab[:, l * LANES:(l + 1) * LANES]


def _from_previous_run(slab, first_run_rows):
    first = lax.broadcasted_iota(jnp.int32, slab.shape, 0) == 0
    return jnp.where(first, first_run_rows, pltpu.roll(slab, 1, 0))


def _conv_by_step(xs, conv_state, w):
    steps = len(xs)
    last = SUBLANES - 1
    before = [_from_previous_run(xs[steps - k], conv_state[SUBLANES - k:SUBLANES - k + 1, :])
              for k in range(CONV_WIDTH - 1, 0, -1)]
    for k in range(1, CONV_WIDTH):
        conv_state[SUBLANES - k:SUBLANES - k + 1, :] = xs[steps - k][last:, :]
    ext = before + xs
    out = []
    for j in range(steps):
        acc = w["conv_b"][...] + w["conv_w"][0:1, :] * ext[j]
        for k in range(1, CONV_WIDTH):
            acc = acc + w["conv_w"][k:k + 1, :] * ext[j + k]
        out.append(acc)
    return out


def _scan_by_step(a, b, h_prev):
    steps = a.shape[0] // SUBLANES
    slab = lambda v, j: v[j * SUBLANES:(j + 1) * SUBLANES, :]
    h0, decay = [slab(b, 0)], [slab(a, 0)]
    for j in range(1, steps):
        h0.append(slab(a, j) * h0[-1] + slab(b, j))
        decay.append(slab(a, j) * decay[-1])
    end_h, end_decay = h0[-1], decay[-1]
    row = lax.broadcasted_iota(jnp.int32, end_h.shape, 0)
    shift = 1
    while shift < SUBLANES:
        keep = row >= shift
        end_h = end_h + end_decay * jnp.where(keep, pltpu.roll(end_h, shift, 0), 0.0)
        end_decay = end_decay * jnp.where(keep, pltpu.roll(end_decay, shift, 0), 1.0)
        shift *= 2
    ends = end_h + end_decay * h_prev
    entering = _from_previous_run(ends, h_prev)
    hs = [h0[j] + decay[j] * entering for j in range(steps)]
    return hs, ends[SUBLANES - 1:, :]


def _spatial_mix(z, sgw_ref, sgb_ref):
    tri = (lax.broadcasted_iota(jnp.int32, (CHUNK, CHUNK), 0)
           >= lax.broadcasted_iota(jnp.int32, (CHUNK, CHUNK), 1))
    s_rows = []
    for c in range(z.shape[0] // CHUNK):
        s_heads = []
        for h in range(SG_HEADS):
            wm = jnp.where(tri, sgw_ref[h], 0.0).astype(_BF16)
            zc = z[c * CHUNK:(c + 1) * CHUNK, h * SG_HEAD_DIM:(h + 1) * SG_HEAD_DIM]
            s_heads.append(_dot(wm, zc) + sgb_ref[:, h:h + 1])
        s_rows.append(jnp.concatenate(s_heads, axis=-1))
    return jnp.concatenate(s_rows, axis=0)


def _tile_layer(x, p, runs_buf, conv_state, h_state, w, sgw_ref, sgb_ref):
    tile = x.shape[0]
    hb = (_rms_scale(x) * w["norm_pre"][...]).astype(_BF16)

    steps, pitch = _run_layout(tile)
    _store_runs(runs_buf, _in_proj(hb, w, COL_X_LRU), steps, pitch)

    v = _in_proj(hb, w, COL_V)

    xs = [_load_step(runs_buf, j, pitch) for j in range(steps)]
    xc = jnp.concatenate(_conv_by_step(xs, conv_state, w), axis=0)
    xcb = xc.astype(_BF16)
    g = _dot(xcb, w["w_gate_dense"][...])
    r_pre, i_pre = g[:, :D_MODEL], g[:, D_MODEL:]
    u, g_sg, g_lru = [_in_proj(hb, w, col) for col in (COL_U, COL_G_SG, COL_G_LRU)]

    z = _head_norm(v, w["sg_norm"][...]).astype(_BF16)
    s = _spatial_mix(z, sgw_ref, sgb_ref)
    y_sg = u * s * _silu(g_sg)

    a, bterm = _lru_coeffs(xc, r_pre, i_pre, w)
    hs, h_last = _scan_by_step(a, bterm, h_state[...])
    h_state[...] = h_last
    for j in range(steps):
        _store_step(runs_buf, j, pitch, hs[j])
    y_lru = _load_runs(runs_buf, steps, pitch) * _silu(g_lru)

    return _post_mix(x, hb, y_sg, y_lru, p, w, _resident_matmuls(w)), h_last


def _prompt_kernel(*refs):
    n_w = len(_VECTOR_NAMES) + len(_MATRIX_NAMES)
    x_ref, p_ref = refs[0], refs[1]
    w = dict(zip(_VECTOR_NAMES + _MATRIX_NAMES, refs[2:2 + n_w]))
    w["w_gate_dense"] = refs[2 + n_w]
    refs = refs[:2 + n_w] + refs[3 + n_w:]
    sgw_ref, sgb_ref = refs[2 + n_w], refs[3 + n_w]
    y_ref, conv_out_ref, lru_out_ref = refs[4 + n_w:7 + n_w]
    runs_buf, conv_state, h_state = refs[7 + n_w:]

    t = pl.program_id(1)
    lanes = x_ref.shape[0]

    @pl.when(t == 0)
    def _():
        conv_state[...] = jnp.zeros_like(conv_state)
        h_state[...] = jnp.zeros_like(h_state)

    h_last = []
    for i in range(lanes):
        y, h = _tile_layer(x_ref[i], p_ref[i], runs_buf.at[i], conv_state.at[i], h_state.at[i],
                           w, sgw_ref, sgb_ref)
        y_ref[i] = y
        h_last.append(h)

    @pl.when(t == pl.num_programs(1) - 1)
    def _():
        for i in range(lanes):
            seq = pl.ds(i * pl.num_programs(0) + pl.program_id(0), 1)
            for k in range(CONV_WIDTH - 1):
                row = SUBLANES - (CONV_WIDTH - 1) + k
                conv_out_ref[k, seq, :] = conv_state[i, row:row + 1, :]
            lru_out_ref[seq, :] = h_last[i]


def _resident(shape):
    return pl.BlockSpec(shape, lambda *_: (0,) * len(shape), pipeline_mode=pl.Buffered(1))


def _prompt_call(x, p, vectors, matrices, sgw, sgb_t):
    batch, seq, d = x.shape
    tile = PROMPT_TILE
    lanes = PROMPT_LANES
    grid = (batch // lanes, seq // tile)
    x4 = x.reshape(lanes, batch // lanes, seq, d)
    p4 = p.reshape(lanes, batch // lanes, seq, PLE_DIM)
    in_specs = [
        pl.BlockSpec((lanes, None, tile, d), lambda b, t: (0, b, t, 0)),
        pl.BlockSpec((lanes, None, tile, PLE_DIM), lambda b, t: (0, b, t, 0)),
    ]
    in_specs += [_resident(v.shape) for v in vectors + matrices]
    wg = matrices[_MATRIX_NAMES.index("w_gate")]
    eye = jnp.eye(LRU_BLOCKS, dtype=wg.dtype)[:, None, :, None]
    dense = lambda blk: (eye * blk.reshape(LRU_BLOCKS, LRU_BLOCK_DIM, 1, LRU_BLOCK_DIM)).reshape(d, d)
    wg_dense = jnp.concatenate([dense(wg[:, :LRU_BLOCK_DIM]), dense(wg[:, LRU_BLOCK_DIM:])], axis=1)
    in_specs += [_resident(wg_dense.shape)]
    in_specs += [_resident(sgw.shape), _resident(sgb_t.shape)]
    out_shape = (
        jax.ShapeDtypeStruct((lanes, batch // lanes, seq, d), _F32),
        jax.ShapeDtypeStruct((CONV_WIDTH - 1, batch, d), _F32),
        jax.ShapeDtypeStruct((batch, d), _F32),
    )
    out_specs = (
        pl.BlockSpec((lanes, None, tile, d), lambda b, t: (0, b, t, 0)),
        pl.BlockSpec((CONV_WIDTH - 1, batch, d), lambda b, t: (0, 0, 0)),
        pl.BlockSpec((batch, d), lambda b, t: (0, 0)),
    )
    y, conv_rows, state = pl.pallas_call(
        _prompt_kernel,
        grid=grid,
        in_specs=in_specs,
        out_specs=out_specs,
        out_shape=out_shape,
        scratch_shapes=[
            pltpu.VMEM((lanes, d // LANES, SUBLANES * _run_layout(tile)[1], LANES), _F32),
            pltpu.VMEM((lanes, SUBLANES, d), _F32),
            pltpu.VMEM((lanes, 1, d), _F32),
        ],
        compiler_params=pltpu.CompilerParams(
            dimension_semantics=("arbitrary", "arbitrary"),
            vmem_limit_bytes=VMEM_LIMIT_BYTES),
        name="prompt_layer",
    )(x4, p4, *vectors, *matrices, wg_dense, sgw, sgb_t)
    return y.reshape(batch, seq, d), conv_rows, state


_SOURCE_NAMES = ("w_in", "w_merge", "w_branch_sg", "w_branch_lru", "w_out", "w_ple_gate",
                 "w_ple", "lru_wa", "lru_wx")


def _column_chunks(ref, matrix, first, last):
    return [([(ref, c, min(STAGE_COLS, last - c), 0)], matrix, c)
            for c in range(first, last, STAGE_COLS)]


class _WeightStream:
    def __init__(self, tasks, out, stage, ring, sem_in, sem_out):
        self.tasks, self.out, self.stage, self.ring = tasks, out, stage, ring
        self.sem_in, self.sem_out = sem_in, sem_out
        self.taken = 0
        for i in range(min(STAGE_SLOTS - 1, len(tasks))):
            self._start_in(i)

    def _shape(self, i):
        parts = self.tasks[i][0]
        return parts[0][0].shape[0], sum(part[2] for part in parts)

    def _in_copies(self, i):
        slot = i % STAGE_SLOTS
        rows, _ = self._shape(i)
        return [pltpu.make_async_copy(
            ref.at[:, pl.ds(col, cols)],
            self.stage.at[slot, pl.ds(0, rows), pl.ds(at, cols)],
            self.sem_in.at[slot]) for ref, col, cols, at in self.tasks[i][0]]

    def _start_in(self, i):
        for copy in self._in_copies(i):
            copy.start()

    def _out_copy(self, i):
        slot = i % RING_SLOTS
        rows, cols = self._shape(i)
        _, matrix, first = self.tasks[i]
        return pltpu.make_async_copy(
            self.ring.at[slot, pl.ds(0, rows), pl.ds(0, cols)],
            self.out[matrix].at[:, pl.ds(first, cols)],
            self.sem_out.at[slot])

    def take(self):
        i = self.taken
        self.taken += 1
        if i + STAGE_SLOTS - 1 < len(self.tasks):
            self._start_in(i + STAGE_SLOTS - 1)
        for copy in self._in_copies(i):
            copy.wait()
        if i >= RING_SLOTS:
            self._out_copy(i - RING_SLOTS).wait()
        rows, cols = self._shape(i)
        staged = self.stage[i % STAGE_SLOTS, 0:rows, 0:cols]
        self.ring[i % RING_SLOTS, 0:rows, 0:cols] = staged.astype(_BF16)
        self._out_copy(i).start()
        return self.ring.at[i % RING_SLOTS, pl.ds(0, rows), pl.ds(0, cols)]

    def matmul(self, lhs, chunks):
        return jnp.concatenate([_dot(lhs, self.take()[...]) for _ in range(chunks)], axis=-1)

    def finish(self):
        assert self.taken == len(self.tasks)
        for i in range(max(0, len(self.tasks) - RING_SLOTS), len(self.tasks)):
            self._out_copy(i).wait()


def _sample_tasks(src):
    d = D_MODEL
    half = lambda name, h: _column_chunks(src[name], name, h * d, (h + 1) * d)
    gate = [([(src["lru_wa"], 0, LRU_BLOCK_DIM, 0),
              (src["lru_wx"], 0, LRU_BLOCK_DIM, LRU_BLOCK_DIM)], "w_gate", 0)]
    return (_column_chunks(src["w_in"], "w_in", 0, 5 * d) + gate
            + half("w_merge", 0) + half("w_branch_sg", 0) + half("w_merge", 1)
            + half("w_branch_lru", 0) + half("w_out", 0) + half("w_ple_gate", 0)
            + half("w_ple", 0))


def _sample_kernel(*refs, steps):
    n_v, n_s, n_m = len(_VECTOR_NAMES), len(_SOURCE_NAMES), len(_MATRIX_NAMES)
    x_ref, p_ref, conv_in_ref, h0_ref = refs[:4]
    w = dict(zip(_VECTOR_NAMES, refs[4:4 + n_v]))
    sgw_ref, sgb_ref = refs[4 + n_v], refs[5 + n_v]
    src = dict(zip(_SOURCE_NAMES, refs[6 + n_v:6 + n_v + n_s]))
    outs = refs[6 + n_v + n_s:]
    y_ref, conv_out_ref, lru_out_ref, z_out_ref = outs[:4]
    out_bf16 = dict(zip(_MATRIX_NAMES, outs[4:4 + n_m]))
    stage, ring, sem_in, sem_out = outs[4 + n_m:]
    d = D_MODEL
    nb = x_ref.shape[0]

    stream = _WeightStream(_sample_tasks(src), out_bf16, stage, ring, sem_in, sem_out)
    chunks_per_block = d // STAGE_COLS

    def slab(v, t):
        return v[t * nb:(t + 1) * nb]

    x = jnp.concatenate([x_ref[:, t, :] for t in range(steps)], axis=0)
    p = jnp.concatenate([p_ref[:, t, :] for t in range(steps)], axis=0)
    hb = (_rms_scale(x) * w["norm_pre"][...]).astype(_BF16)
    proj = stream.matmul(hb, 5 * chunks_per_block)
    col = lambda c: proj[:, c * d:(c + 1) * d]

    z = _head_norm(col(COL_V), w["sg_norm"][...])
    for t in range(steps):
        z_out_ref[:, t, :] = slab(z, t)
    s_slabs = []
    for t in range(steps):
        acc = jnp.broadcast_to(sgb_ref[t:t + 1, :], (nb, d))
        for u in range(t + 1):
            acc = acc + sgw_ref[t * steps + u:t * steps + u + 1, :] * slab(z, u)
        s_slabs.append(acc)
    s = jnp.concatenate(s_slabs, axis=0)
    y_sg = col(COL_U) * s * _silu(col(COL_G_SG))

    x_lru = col(COL_X_LRU)
    hist = [conv_in_ref[k] for k in range(CONV_WIDTH - 1)]
    hist += [slab(x_lru, t) for t in range(steps)]
    for k in range(CONV_WIDTH - 1):
        conv_out_ref[k] = hist[steps + k]
    xc_slabs = []
    for t in range(steps):
        acc = w["conv_b"][...] + w["conv_w"][0:1, :] * hist[t]
        for k in range(1, CONV_WIDTH):
            acc = acc + w["conv_w"][k:k + 1, :] * hist[t + k]
        xc_slabs.append(acc)
    xc = jnp.concatenate(xc_slabs, axis=0)
    r_pre, i_pre = _lru_gates(xc.astype(_BF16), stream.take())
    a, bterm = _lru_coeffs(xc, r_pre, i_pre, w)
    h = h0_ref[...]
    h_slabs = []
    for t in range(steps):
        h = slab(a, t) * h + slab(bterm, t)
        h_slabs.append(h)
    lru_out_ref[...] = h
    y_lru = jnp.concatenate(h_slabs, axis=0) * _silu(col(COL_G_LRU))

    block = lambda lhs: stream.matmul(lhs, chunks_per_block)
    mm = {name: block for name in
          ("merge_a", "branch_sg", "merge_b", "branch_lru", "out", "ple_gate", "ple")}
    y = _post_mix(x, hb, y_sg, y_lru, p, w, mm)
    for t in range(steps):
        y_ref[:, t, :] = slab(y, t)

    stream.finish()


def _sample_call(x, p, conv_in, h0, vectors, sources, sgw_rows, sgb_rows):
    nb, steps, d = x.shape
    vmem = pl.BlockSpec(memory_space=pltpu.VMEM)
    hbm = pl.BlockSpec(memory_space=pl.ANY)
    matrix_shapes = [_MATRIX_SHAPES[m] for m in _MATRIX_NAMES]
    stage_rows = max(s.shape[0] for s in sources)
    out_shape = (
        jax.ShapeDtypeStruct((nb, steps, d), _F32),
        jax.ShapeDtypeStruct((CONV_WIDTH - 1, nb, d), _F32),
        jax.ShapeDtypeStruct((nb, d), _F32),
        jax.ShapeDtypeStruct((nb, steps, d), _F32),
    ) + tuple(jax.ShapeDtypeStruct(shape, _BF16) for shape in matrix_shapes)
    outs = pl.pallas_call(
        functools.partial(_sample_kernel, steps=steps),
        in_specs=[vmem] * (6 + len(vectors)) + [hbm] * len(sources),
        out_specs=(vmem,) * 4 + (hbm,) * len(matrix_shapes),
        out_shape=out_shape,
        scratch_shapes=[
            pltpu.VMEM((STAGE_SLOTS, stage_rows, STAGE_COLS), _F32),
            pltpu.VMEM((RING_SLOTS, stage_rows, STAGE_COLS), _BF16),
            pltpu.SemaphoreType.DMA((STAGE_SLOTS,)),
            pltpu.SemaphoreType.DMA((RING_SLOTS,)),
        ],
        compiler_params=pltpu.CompilerParams(vmem_limit_bytes=VMEM_LIMIT_BYTES),
        name="sample_layer",
    )(x, p, conv_in, h0, *vectors, sgw_rows, sgb_rows, *sources)
    return outs[:4], list(outs[4:])


def kernel(x_prompt, x_sample, p_prompt, p_sample, state_conv, state_lru, norm_pre, w_in, sg_norm,
           sg_w, sg_b, conv_w, conv_b, lru_wa, lru_ba, lru_wx, lru_bx, lru_lambda, w_branch_sg,
           w_branch_lru, w_merge, b_merge, w_out, norm_post, w_ple, w_ple_gate, b_ple_gate):
    depth = norm_pre.shape[0]
    nb, steps, d = x_sample.shape
    xp, xs = x_prompt, x_sample
    conv_p, lru_p, conv_s, lru_s, chunk_s = [], [], [], [], []
    for l in range(depth):
        row = lambda v: v[l].reshape(1, -1)
        by_name = dict(
            norm_pre=row(norm_pre), sg_norm=row(sg_norm), conv_w=conv_w[l], conv_b=row(conv_b),
            lru_ba=row(lru_ba), lru_bx=row(lru_bx), lru_lambda=row(lru_lambda),
            b_merge=row(b_merge), norm_post=row(norm_post), b_ple_gate=row(b_ple_gate))
        vectors = [by_name[n] for n in _VECTOR_NAMES]
        f32_by_name = dict(
            w_in=w_in[l], w_merge=w_merge[l], w_branch_sg=w_branch_sg[l],
            w_branch_lru=w_branch_lru[l], w_out=w_out[l], w_ple_gate=w_ple_gate[l],
            w_ple=w_ple[l], lru_wa=lru_wa[l].reshape(d, LRU_BLOCK_DIM),
            lru_wx=lru_wx[l].reshape(d, LRU_BLOCK_DIM))
        sources = [f32_by_name[name] for name in _SOURCE_NAMES]
        sgw_rows = jnp.repeat(sg_w[l][:, :steps, :steps].reshape(SG_HEADS, steps * steps).T,
                              SG_HEAD_DIM, axis=1)
        sgb_rows = jnp.repeat(sg_b[l][:, :steps].T, SG_HEAD_DIM, axis=1)
        (xs, cs, hs, zs), matrices = _sample_call(
            xs, p_sample[l], jnp.swapaxes(state_conv[l], 0, 1), state_lru[l], vectors, sources,
            sgw_rows, sgb_rows)
        conv_s.append(jnp.swapaxes(cs, 0, 1))
        lru_s.append(hs)
        chunk_s.append(zs)
        xp, cp, hp = _prompt_call(xp, p_prompt[l], vectors, matrices, sg_w[l], sg_b[l].T)
        conv_p.append(jnp.swapaxes(cp, 0, 1))
        lru_p.append(hp)
    return (xp, xs, jnp.stack(conv_p), jnp.stack(lru_p),
            jnp.stack(conv_s), jnp.stack(lru_s), jnp.stack(chunk_s))
```

```python
import functools

import jax
import jax.numpy as jnp
from jax import lax
from jax.experimental import pallas as pl
from jax.experimental.pallas import tpu as pltpu

D_MODEL = 1024
PLE_DIM = 256
SG_HEADS = 4
SG_HEAD_DIM = D_MODEL // SG_HEADS
CHUNK = 128
LRU_BLOCKS = 8
LRU_BLOCK_DIM = D_MODEL // LRU_BLOCKS
CONV_WIDTH = 4
LRU_C = 8.0
EPS = 1e-6

SUBLANES = 8
LANES = 128
PROMPT_TILE = 256
VMEM_LIMIT_BYTES = 56 * 1024 * 1024
STAGE_COLS = 1024
STAGE_SLOTS = 4
RING_SLOTS = 4

_BF16 = jnp.bfloat16
_F32 = jnp.float32

_VECTOR_NAMES = ("norm_pre", "sg_norm", "conv_w", "conv_b", "lru_ba", "lru_bx", "lru_lambda",
                 "b_merge", "norm_post", "b_ple_gate")
_MATRIX_SHAPES = dict(
    w_in=(D_MODEL, 5 * D_MODEL), w_merge=(D_MODEL, 2 * D_MODEL),
    w_branch_sg=(D_MODEL, D_MODEL), w_branch_lru=(D_MODEL, D_MODEL), w_out=(D_MODEL, D_MODEL),
    w_ple_gate=(D_MODEL, D_MODEL), w_ple=(PLE_DIM, D_MODEL),
    w_gate=(D_MODEL, 2 * LRU_BLOCK_DIM))
_MATRIX_NAMES = tuple(_MATRIX_SHAPES)

COL_U, COL_V, COL_G_SG, COL_X_LRU, COL_G_LRU = range(5)


def _dot(a, b):
    return jnp.dot(a, b, preferred_element_type=_F32)


def _rms_scale(x):
    var = jnp.mean(x * x, axis=-1, keepdims=True)
    return x * lax.rsqrt(var + EPS)


def _silu(x):
    return x * jax.nn.sigmoid(x)


def _in_proj(hb, w, col):
    return _dot(hb, w["w_in"][:, col * D_MODEL:(col + 1) * D_MODEL])


def _merge_pre(hb, w, half):
    return _dot(hb, w["w_merge"][:, half * D_MODEL:(half + 1) * D_MODEL])


def _head_norm(v, sgn):
    zs = []
    for h in range(SG_HEADS):
        sl = slice(h * SG_HEAD_DIM, (h + 1) * SG_HEAD_DIM)
        zs.append(_rms_scale(v[:, sl]) * sgn[:, sl])
    return jnp.concatenate(zs, axis=-1)


def _lru_gates(xcb, wg_ref):
    gates = []
    for n in range(LRU_BLOCKS):
        sl = slice(n * LRU_BLOCK_DIM, (n + 1) * LRU_BLOCK_DIM)
        gates.append(_dot(xcb[:, sl], wg_ref[sl, :]))
    r_pre = jnp.concatenate([g[:, :LRU_BLOCK_DIM] for g in gates], axis=-1)
    i_pre = jnp.concatenate([g[:, LRU_BLOCK_DIM:] for g in gates], axis=-1)
    return r_pre, i_pre


def _lru_coeffs(xc, r_pre, i_pre, w):
    r = jax.nn.sigmoid(r_pre + w["lru_ba"][...])
    i = jax.nn.sigmoid(i_pre + w["lru_bx"][...])
    log_a = (-LRU_C * jax.nn.softplus(-w["lru_lambda"][...])) * r
    a = jnp.exp(log_a)
    mult = jnp.sqrt(1.0 - a * a)
    return a, mult * (i * xc)


def _merge_gate(pre, w, half):
    return jax.nn.sigmoid(pre + w["b_merge"][:, half * D_MODEL:(half + 1) * D_MODEL])


def _post_mix(x, hb, y_sg, y_lru, p, w, mm):
    g_a = _merge_gate(mm["merge_a"](hb), w, 0)
    merged = g_a * mm["branch_sg"](y_sg.astype(_BF16))
    g_b = _merge_gate(mm["merge_b"](hb), w, 1)
    merged = merged + g_b * mm["branch_lru"](y_lru.astype(_BF16))
    o = mm["out"](merged.astype(_BF16))
    x1 = x + _rms_scale(o) * w["norm_post"][...]
    gate = jax.nn.sigmoid(mm["ple_gate"](x1.astype(_BF16)) + w["b_ple_gate"][...])
    return x1 + gate * mm["ple"](p.astype(_BF16))


def _resident_matmuls(w):
    return dict(
        merge_a=lambda lhs: _merge_pre(lhs, w, 0), merge_b=lambda lhs: _merge_pre(lhs, w, 1),
        branch_sg=lambda lhs: _dot(lhs, w["w_branch_sg"][...]),
        branch_lru=lambda lhs: _dot(lhs, w["w_branch_lru"][...]),
        out=lambda lhs: _dot(lhs, w["w_out"][...]),
        ple_gate=lambda lhs: _dot(lhs, w["w_ple_gate"][...]),
        ple=lambda lhs: _dot(lhs, w["w_ple"][...]))


def _run_layout(tile):
    steps = tile // SUBLANES
    return steps, steps + SUBLANES


def _store_runs(buf, value, steps, pitch):
    for l in range(value.shape[1] // LANES):
        for s in range(SUBLANES):
            buf[l, s * pitch:s * pitch + steps, :] = (
                value[s * steps:(s + 1) * steps, l * LANES:(l + 1) * LANES])


def _load_runs(buf, steps, pitch):
    return jnp.concatenate(
        [jnp.concatenate([buf[l, s * pitch:s * pitch + steps, :] for l in range(buf.shape[0])],
                         axis=1) for s in range(SUBLANES)], axis=0)


def _load_step(buf, j, pitch):
    return jnp.concatenate([buf[l, pl.ds(j, SUBLANES, stride=pitch), :]
                            for l in range(buf.shape[0])], axis=1)


def _store_step(buf, j, pitch, slab):
    for l in range(buf.shape[0]):
        buf[l, pl.ds(j, SUBLANES, stride=pitch), :] = slab[:, l * LANES:(l + 1) * LANES]


def _from_previous_run(slab, first_run_rows):
    first = lax.broadcasted_iota(jnp.int32, slab.shape, 0) == 0
    return jnp.where(first, first_run_rows, pltpu.roll(slab, 1, 0))


def _conv_by_step(xs, conv_state, w):
    steps = len(xs)
    last = SUBLANES - 1
    before = [_from_previous_run(xs[steps - k], conv_state[SUBLANES - k:SUBLANES - k + 1, :])
              for k in range(CONV_WIDTH - 1, 0, -1)]
    for k in range(1, CONV_WIDTH):
        conv_state[SUBLANES - k:SUBLANES - k + 1, :] = xs[steps - k][last:, :]
    ext = before + xs
    out = []
    for j in range(steps):
        acc = w["conv_b"][...] + w["conv_w"][0:1, :] * ext[j]
        for k in range(1, CONV_WIDTH):
            acc = acc + w["conv_w"][k:k + 1, :] * ext[j + k]
        out.append(acc)
    return out


def _scan_by_step(a, b, h_prev):
    steps = a.shape[0] // SUBLANES
    slab = lambda v, j: v[j * SUBLANES:(j + 1) * SUBLANES, :]
    h0, decay = [slab(b, 0)], [slab(a, 0)]
    for j in range(1, steps):
        h0.append(slab(a, j) * h0[-1] + slab(b, j))
        decay.append(slab(a, j) * decay[-1])
    end_h, end_decay = h0[-1], decay[-1]
    row = lax.broadcasted_iota(jnp.int32, end_h.shape, 0)
    shift = 1
    while shift < SUBLANES:
        keep = row >= shift
        end_h = end_h + end_decay * jnp.where(keep, pltpu.roll(end_h, shift, 0), 0.0)
        end_decay = end_decay * jnp.where(keep, pltpu.roll(end_decay, shift, 0), 1.0)
        shift *= 2
    ends = end_h + end_decay * h_prev
    entering = _from_previous_run(ends, h_prev)
    hs = [h0[j] + decay[j] * entering for j in range(steps)]
    return hs, ends[SUBLANES - 1:, :]


def _spatial_mix(z, sgw_ref, sgb_ref):
    tri = (lax.broadcasted_iota(jnp.int32, (CHUNK, CHUNK), 0)
           >= lax.broadcasted_iota(jnp.int32, (CHUNK, CHUNK), 1))
    s_rows = []
    for c in range(z.shape[0] // CHUNK):
        s_heads = []
        for h in range(SG_HEADS):
            wm = jnp.where(tri, sgw_ref[h], 0.0).astype(_BF16)
            zc = z[c * CHUNK:(c + 1) * CHUNK, h * SG_HEAD_DIM:(h + 1) * SG_HEAD_DIM]
            s_heads.append(_dot(wm, zc) + sgb_ref[:, h:h + 1])
        s_rows.append(jnp.concatenate(s_heads, axis=-1))
    return jnp.concatenate(s_rows, axis=0)


def _prompt_kernel(*refs):
    n_w = len(_VECTOR_NAMES) + len(_MATRIX_NAMES)
    x_ref, p_ref = refs[0], refs[1]
    w = dict(zip(_VECTOR_NAMES + _MATRIX_NAMES, refs[2:2 + n_w]))
    sgw_ref, sgb_ref = refs[2 + n_w], refs[3 + n_w]
    y_ref, conv_out_ref, lru_out_ref = refs[4 + n_w:7 + n_w]
    runs_buf, conv_state, h_state = refs[7 + n_w:]

    t = pl.program_id(1)
    tile = x_ref.shape[0]

    @pl.when(t == 0)
    def _():
        conv_state[...] = jnp.zeros_like(conv_state)
        h_state[...] = jnp.zeros_like(h_state)

    x = x_ref[...]
    hb = (_rms_scale(x) * w["norm_pre"][...]).astype(_BF16)

    steps, pitch = _run_layout(tile)
    _store_runs(runs_buf, _in_proj(hb, w, COL_X_LRU), steps, pitch)

    v = _in_proj(hb, w, COL_V)

    xs = [_load_step(runs_buf, j, pitch) for j in range(steps)]
    xc = jnp.concatenate(_conv_by_step(xs, conv_state, w), axis=0)
    xcb = xc.astype(_BF16)
    gates, wide = [], []
    for pair, col in enumerate((COL_U, COL_G_SG, COL_G_LRU, None)):
        for n in (2 * pair, 2 * pair + 1):
            sl = slice(n * LRU_BLOCK_DIM, (n + 1) * LRU_BLOCK_DIM)
            gates.append(_dot(xcb[:, sl], w["w_gate"][sl, :]))
        if col is not None:
            wide.append(_in_proj(hb, w, col))
    u, g_sg, g_lru = wide
    r_pre = jnp.concatenate([g[:, :LRU_BLOCK_DIM] for g in gates], axis=-1)
    i_pre = jnp.concatenate([g[:, LRU_BLOCK_DIM:] for g in gates], axis=-1)

    z = _head_norm(v, w["sg_norm"][...]).astype(_BF16)
    s = _spatial_mix(z, sgw_ref, sgb_ref)
    y_sg = u * s * _silu(g_sg)

    a, bterm = _lru_coeffs(xc, r_pre, i_pre, w)
    hs, h_last = _scan_by_step(a, bterm, h_state[...])
    h_state[...] = h_last
    for j in range(steps):
        _store_step(runs_buf, j, pitch, hs[j])
    y_lru = _load_runs(runs_buf, steps, pitch) * _silu(g_lru)

    y_ref[...] = _post_mix(x, hb, y_sg, y_lru, p_ref[...], w, _resident_matmuls(w))

    @pl.when(t == pl.num_programs(1) - 1)
    def _():
        seq = pl.ds(pl.program_id(0), 1)
        for k in range(CONV_WIDTH - 1):
            row = SUBLANES - (CONV_WIDTH - 1) + k
            conv_out_ref[k, seq, :] = conv_state[row:row + 1, :]
        lru_out_ref[seq, :] = h_last


def _resident(shape):
    return pl.BlockSpec(shape, lambda *_: (0,) * len(shape), pipeline_mode=pl.Buffered(1))


def _prompt_call(x, p, vectors, matrices, sgw, sgb_t):
    batch, seq, d = x.shape
    tile = PROMPT_TILE
    grid = (batch, seq // tile)
    in_specs = [
        pl.BlockSpec((None, tile, d), lambda b, t: (b, t, 0)),
        pl.BlockSpec((None, tile, PLE_DIM), lambda b, t: (b, t, 0)),
    ]
    in_specs += [_resident(v.shape) for v in vectors + matrices]
    in_specs += [_resident(sgw.shape), _resident(sgb_t.shape)]
    out_shape = (
        jax.ShapeDtypeStruct((batch, seq, d), _F32),
        jax.ShapeDtypeStruct((CONV_WIDTH - 1, batch, d), _F32),
        jax.ShapeDtypeStruct((batch, d), _F32),
    )
    out_specs = (
        pl.BlockSpec((None, tile, d), lambda b, t: (b, t, 0)),
        pl.BlockSpec((CONV_WIDTH - 1, batch, d), lambda b, t: (0, 0, 0)),
        pl.BlockSpec((batch, d), lambda b, t: (0, 0)),
    )
    return pl.pallas_call(
        _prompt_kernel,
        grid=grid,
        in_specs=in_specs,
        out_specs=out_specs,
        out_shape=out_shape,
        scratch_shapes=[
            pltpu.VMEM((d // LANES, SUBLANES * _run_layout(tile)[1], LANES), _F32),
            pltpu.VMEM((SUBLANES, d), _F32),
            pltpu.VMEM((1, d), _F32),
        ],
        compiler_params=pltpu.CompilerParams(
            dimension_semantics=("arbitrary", "arbitrary"),
            vmem_limit_bytes=VMEM_LIMIT_BYTES),
        name="prompt_layer",
    )(x, p, *vectors, *matrices, sgw, sgb_t)


_SOURCE_NAMES = ("w_in", "w_merge", "w_branch_sg", "w_branch_lru", "w_out", "w_ple_gate",
                 "w_ple", "lru_wa", "lru_wx")


def _column_chunks(ref, matrix, first, last):
    return [([(ref, c, min(STAGE_COLS, last - c), 0)], matrix, c)
            for c in range(first, last, STAGE_COLS)]


class _WeightStream:
    def __init__(self, tasks, out, stage, ring, sem_in, sem_out):
        self.tasks, self.out, self.stage, self.ring = tasks, out, stage, ring
        self.sem_in, self.sem_out = sem_in, sem_out
        self.taken = 0
        for i in range(min(STAGE_SLOTS - 1, len(tasks))):
            self._start_in(i)

    def _shape(self, i):
        parts = self.tasks[i][0]
        return parts[0][0].shape[0], sum(part[2] for part in parts)

    def _in_copies(self, i):
        slot = i % STAGE_SLOTS
        rows, _ = self._shape(i)
        return [pltpu.make_async_copy(
            ref.at[:, pl.ds(col, cols)],
            self.stage.at[slot, pl.ds(0, rows), pl.ds(at, cols)],
            self.sem_in.at[slot]) for ref, col, cols, at in self.tasks[i][0]]

    def _start_in(self, i):
        for copy in self._in_copies(i):
            copy.start()

    def _out_copy(self, i):
        slot = i % RING_SLOTS
        rows, cols = self._shape(i)
        _, matrix, first = self.tasks[i]
        return pltpu.make_async_copy(
            self.ring.at[slot, pl.ds(0, rows), pl.ds(0, cols)],
            self.out[matrix].at[:, pl.ds(first, cols)],
            self.sem_out.at[slot])

    def take(self):
        i = self.taken
        self.taken += 1
        if i + STAGE_SLOTS - 1 < len(self.tasks):
            self._start_in(i + STAGE_SLOTS - 1)
        for copy in self._in_copies(i):
            copy.wait()
        if i >= RING_SLOTS:
            self._out_copy(i - RING_SLOTS).wait()
        rows, cols = self._shape(i)
        staged = self.stage[i % STAGE_SLOTS, 0:rows, 0:cols]
        self.ring[i % RING_SLOTS, 0:rows, 0:cols] = staged.astype(_BF16)
        self._out_copy(i).start()
        return self.ring.at[i % RING_SLOTS, pl.ds(0, rows), pl.ds(0, cols)]

    def matmul(self, lhs, chunks):
        return jnp.concatenate([_dot(lhs, self.take()[...]) for _ in range(chunks)], axis=-1)

    def finish(self):
        assert self.taken == len(self.tasks)
        for i in range(max(0, len(self.tasks) - RING_SLOTS), len(self.tasks)):
            self._out_copy(i).wait()


def _sample_tasks(src):
    d = D_MODEL
    half = lambda name, h: _column_chunks(src[name], name, h * d, (h + 1) * d)
    gate = [([(src["lru_wa"], 0, LRU_BLOCK_DIM, 0),
              (src["lru_wx"], 0, LRU_BLOCK_DIM, LRU_BLOCK_DIM)], "w_gate", 0)]
    return (_column_chunks(src["w_in"], "w_in", 0, 5 * d) + gate
            + half("w_merge", 0) + half("w_branch_sg", 0) + half("w_merge", 1)
            + half("w_branch_lru", 0) + half("w_out", 0) + half("w_ple_gate", 0)
            + half("w_ple", 0))


def _sample_kernel(*refs, steps):
    n_v, n_s, n_m = len(_VECTOR_NAMES), len(_SOURCE_NAMES), len(_MATRIX_NAMES)
    x_ref, p_ref, conv_in_ref, h0_ref = refs[:4]
    w = dict(zip(_VECTOR_NAMES, refs[4:4 + n_v]))
    sgw_ref, sgb_ref = refs[4 + n_v], refs[5 + n_v]
    src = dict(zip(_SOURCE_NAMES, refs[6 + n_v:6 + n_v + n_s]))
    outs = refs[6 + n_v + n_s:]
    y_ref, conv_out_ref, lru_out_ref, z_out_ref = outs[:4]
    out_bf16 = dict(zip(_MATRIX_NAMES, outs[4:4 + n_m]))
    stage, ring, sem_in, sem_out = outs[4 + n_m:]
    d = D_MODEL
    nb = x_ref.shape[0]

    stream = _WeightStream(_sample_tasks(src), out_bf16, stage, ring, sem_in, sem_out)
    chunks_per_block = d // STAGE_COLS

    def slab(v, t):
        return v[t * nb:(t + 1) * nb]

    x = jnp.concatenate([x_ref[:, t, :] for t in range(steps)], axis=0)
    p = jnp.concatenate([p_ref[:, t, :] for t in range(steps)], axis=0)
    hb = (_rms_scale(x) * w["norm_pre"][...]).astype(_BF16)
    proj = stream.matmul(hb, 5 * chunks_per_block)
    col = lambda c: proj[:, c * d:(c + 1) * d]

    z = _head_norm(col(COL_V), w["sg_norm"][...])
    for t in range(steps):
        z_out_ref[:, t, :] = slab(z, t)
    s_slabs = []
    for t in range(steps):
        acc = jnp.broadcast_to(sgb_ref[t:t + 1, :], (nb, d))
        for u in range(t + 1):
            acc = acc + sgw_ref[t * steps + u:t * steps + u + 1, :] * slab(z, u)
        s_slabs.append(acc)
    s = jnp.concatenate(s_slabs, axis=0)
    y_sg = col(COL_U) * s * _silu(col(COL_G_SG))

    x_lru = col(COL_X_LRU)
    hist = [conv_in_ref[k] for k in range(CONV_WIDTH - 1)]
    hist += [slab(x_lru, t) for t in range(steps)]
    for k in range(CONV_WIDTH - 1):
        conv_out_ref[k] = hist[steps + k]
    xc_slabs = []
    for t in range(steps):
        acc = w["conv_b"][...] + w["conv_w"][0:1, :] * hist[t]
        for k in range(1, CONV_WIDTH):
            acc = acc + w["conv_w"][k:k + 1, :] * hist[t + k]
        xc_slabs.append(acc)
    xc = jnp.concatenate(xc_slabs, axis=0)
    r_pre, i_pre = _lru_gates(xc.astype(_BF16), stream.take())
    a, bterm = _lru_coeffs(xc, r_pre, i_pre, w)
    h = h0_ref[...]
    h_slabs = []
    for t in range(steps):
        h = slab(a, t) * h + slab(bterm, t)
        h_slabs.append(h)
    lru_out_ref[...] = h
    y_lru = jnp.concatenate(h_slabs, axis=0) * _silu(col(COL_G_LRU))

    block = lambda lhs: stream.matmul(lhs, chunks_per_block)
    mm = {name: block for name in
          ("merge_a", "branch_sg", "merge_b", "branch_lru", "out", "ple_gate", "ple")}
    y = _post_mix(x, hb, y_sg, y_lru, p, w, mm)
    for t in range(steps):
        y_ref[:, t, :] = slab(y, t)

    stream.finish()


def _sample_call(x, p, conv_in, h0, vectors, sources, sgw_rows, sgb_rows):
    nb, steps, d = x.shape
    vmem = pl.BlockSpec(memory_space=pltpu.VMEM)
    hbm = pl.BlockSpec(memory_space=pl.ANY)
    matrix_shapes = [_MATRIX_SHAPES[m] for m in _MATRIX_NAMES]
    stage_rows = max(s.shape[0] for s in sources)
    out_shape = (
        jax.ShapeDtypeStruct((nb, steps, d), _F32),
        jax.ShapeDtypeStruct((CONV_WIDTH - 1, nb, d), _F32),
        jax.ShapeDtypeStruct((nb, d), _F32),
        jax.ShapeDtypeStruct((nb, steps, d), _F32),
    ) + tuple(jax.ShapeDtypeStruct(shape, _BF16) for shape in matrix_shapes)
    outs = pl.pallas_call(
        functools.partial(_sample_kernel, steps=steps),
        in_specs=[vmem] * (6 + len(vectors)) + [hbm] * len(sources),
        out_specs=(vmem,) * 4 + (hbm,) * len(matrix_shapes),
        out_shape=out_shape,
        scratch_shapes=[
            pltpu.VMEM((STAGE_SLOTS, stage_rows, STAGE_COLS), _F32),
            pltpu.VMEM((RING_SLOTS, stage_rows, STAGE_COLS), _BF16),
            pltpu.SemaphoreType.DMA((STAGE_SLOTS,)),
            pltpu.SemaphoreType.DMA((RING_SLOTS,)),
        ],
        compiler_params=pltpu.CompilerParams(vmem_limit_bytes=VMEM_LIMIT_BYTES),
        name="sample_layer",
    )(x, p, conv_in, h0, *vectors, sgw_rows, sgb_rows, *sources)
    return outs[:4], list(outs[4:])


def kernel(x_prompt, x_sample, p_prompt, p_sample, state_conv, state_lru, norm_pre, w_in, sg_norm,
           sg_w, sg_b, conv_w, conv_b, lru_wa, lru_ba, lru_wx, lru_bx, lru_lambda, w_branch_sg,
           w_branch_lru, w_merge, b_merge, w_out, norm_post, w_ple, w_ple_gate, b_ple_gate):
    depth = norm_pre.shape[0]
    nb, steps, d = x_sample.shape
    xp, xs = x_prompt, x_sample
    conv_p, lru_p, conv_s, lru_s, chunk_s = [], [], [], [], []
    for l in range(depth):
        row = lambda v: v[l].reshape(1, -1)
        by_name = dict(
            norm_pre=row(norm_pre), sg_norm=row(sg_norm), conv_w=conv_w[l], conv_b=row(conv_b),
            lru_ba=row(lru_ba), lru_bx=row(lru_bx), lru_lambda=row(lru_lambda),
            b_merge=row(b_merge), norm_post=row(norm_post), b_ple_gate=row(b_ple_gate))
        vectors = [by_name[n] for n in _VECTOR_NAMES]
        f32_by_name = dict(
            w_in=w_in[l], w_merge=w_merge[l], w_branch_sg=w_branch_sg[l],
            w_branch_lru=w_branch_lru[l], w_out=w_out[l], w_ple_gate=w_ple_gate[l],
            w_ple=w_ple[l], lru_wa=lru_wa[l].reshape(d, LRU_BLOCK_DIM),
            lru_wx=lru_wx[l].reshape(d, LRU_BLOCK_DIM))
        sources = [f32_by_name[name] for name in _SOURCE_NAMES]
        sgw_rows = jnp.repeat(sg_w[l][:, :steps, :steps].reshape(SG_HEADS, steps * steps).T,
                              SG_HEAD_DIM, axis=1)
        sgb_rows = jnp.repeat(sg_b[l][:, :steps].T, SG_HEAD_DIM, axis=1)
        (xs, cs, hs, zs), matrices = _sample_call(
            xs, p_sample[l], jnp.swapaxes(state_conv[l], 0, 1), state_lru[l], vectors, sources,
            sgw_rows, sgb_rows)
        conv_s.append(jnp.swapaxes(cs, 0, 1))
        lru_s.append(hs)
        chunk_s.append(zs)
        xp, cp, hp = _prompt_call(xp, p_prompt[l], vectors, matrices, sg_w[l], sg_b[l].T)
        conv_p.append(jnp.swapaxes(cp, 0, 1))
        lru_p.append(hp)
    return (xp, xs, jnp.stack(conv_p), jnp.stack(lru_p),
            jnp.stack(conv_s), jnp.stack(lru_s), jnp.stack(chunk_s))
```

```python
import functools

import jax
import jax.numpy as jnp
from jax import lax
from jax.experimental import pallas as pl
from jax.experimental.pallas import tpu as pltpu

D_MODEL = 1024
PLE_DIM = 256
SG_HEADS = 4
SG_HEAD_DIM = D_MODEL // SG_HEADS
CHUNK = 128
LRU_BLOCKS = 8
LRU_BLOCK_DIM = D_MODEL // LRU_BLOCKS
CONV_WIDTH = 4
LRU_C = 8.0
EPS = 1e-6

SUBLANES = 8
LANES = 128
PROMPT_TILE = 256
VMEM_LIMIT_BYTES = 56 * 1024 * 1024
STAGE_COLS = 1024
STAGE_SLOTS = 4
RING_SLOTS = 4

_BF16 = jnp.bfloat16
_F32 = jnp.float32
_U32 = jnp.uint32
ROWS_PER_WORD = 2

_VECTOR_NAMES = ("norm_pre", "sg_norm", "conv_w", "conv_b", "lru_ba", "lru_bx", "lru_lambda",
                 "b_merge", "norm_post", "b_ple_gate")
_MATRIX_SHAPES = dict(
    w_in=(D_MODEL, 5 * D_MODEL), w_merge=(D_MODEL, 2 * D_MODEL),
    w_branch_sg=(D_MODEL, D_MODEL), w_branch_lru=(D_MODEL, D_MODEL), w_out=(D_MODEL, D_MODEL),
    w_ple_gate=(D_MODEL, D_MODEL), w_ple=(PLE_DIM, D_MODEL),
    w_gate=(D_MODEL, 2 * LRU_BLOCK_DIM))
_MATRIX_NAMES = tuple(_MATRIX_SHAPES)

COL_U, COL_V, COL_G_SG, COL_X_LRU, COL_G_LRU = range(5)


def _dot(a, b):
    return jnp.dot(a, b, preferred_element_type=_F32)


def _pack_rows(wb):
    return pltpu.bitcast(wb, _U32)


def _dot_packed(a, words):
    return _dot(a, pltpu.bitcast(words, _BF16))


def _packed_rows(first, rows):
    return slice(first // ROWS_PER_WORD, (first + rows) // ROWS_PER_WORD)


def _rms_scale(x):
    var = jnp.mean(x * x, axis=-1, keepdims=True)
    return x * lax.rsqrt(var + EPS)


def _silu(x):
    return x * jax.nn.sigmoid(x)


def _in_proj(hb, w, col):
    return _dot_packed(hb, w["w_in"][:, col * D_MODEL:(col + 1) * D_MODEL])


def _merge_pre(hb, w, half):
    return _dot_packed(hb, w["w_merge"][:, half * D_MODEL:(half + 1) * D_MODEL])


def _head_norm(v, sgn):
    zs = []
    for h in range(SG_HEADS):
        sl = slice(h * SG_HEAD_DIM, (h + 1) * SG_HEAD_DIM)
        zs.append(_rms_scale(v[:, sl]) * sgn[:, sl])
    return jnp.concatenate(zs, axis=-1)


def _lru_gates(xcb, wg_ref):
    gates = []
    for n in range(LRU_BLOCKS):
        sl = slice(n * LRU_BLOCK_DIM, (n + 1) * LRU_BLOCK_DIM)
        wg = wg_ref[_packed_rows(n * LRU_BLOCK_DIM, LRU_BLOCK_DIM), :]
        gates.append(_dot_packed(xcb[:, sl], wg))
    r_pre = jnp.concatenate([g[:, :LRU_BLOCK_DIM] for g in gates], axis=-1)
    i_pre = jnp.concatenate([g[:, LRU_BLOCK_DIM:] for g in gates], axis=-1)
    return r_pre, i_pre


def _lru_coeffs(xc, r_pre, i_pre, w):
    r = jax.nn.sigmoid(r_pre + w["lru_ba"][...])
    i = jax.nn.sigmoid(i_pre + w["lru_bx"][...])
    log_a = (-LRU_C * jax.nn.softplus(-w["lru_lambda"][...])) * r
    a = jnp.exp(log_a)
    mult = jnp.sqrt(1.0 - a * a)
    return a, mult * (i * xc)


def _merge_gate(pre, w, half):
    return jax.nn.sigmoid(pre + w["b_merge"][:, half * D_MODEL:(half + 1) * D_MODEL])


def _post_mix(x, hb, y_sg, y_lru, p, w, mm):
    g_a = _merge_gate(mm["merge_a"](hb), w, 0)
    merged = g_a * mm["branch_sg"](y_sg.astype(_BF16))
    g_b = _merge_gate(mm["merge_b"](hb), w, 1)
    merged = merged + g_b * mm["branch_lru"](y_lru.astype(_BF16))
    o = mm["out"](merged.astype(_BF16))
    x1 = x + _rms_scale(o) * w["norm_post"][...]
    gate = jax.nn.sigmoid(mm["ple_gate"](x1.astype(_BF16)) + w["b_ple_gate"][...])
    return x1 + gate * mm["ple"](p.astype(_BF16))


def _resident_matmuls(w):
    return dict(
        merge_a=lambda lhs: _merge_pre(lhs, w, 0), merge_b=lambda lhs: _merge_pre(lhs, w, 1),
        branch_sg=lambda lhs: _dot_packed(lhs, w["w_branch_sg"][...]),
        branch_lru=lambda lhs: _dot_packed(lhs, w["w_branch_lru"][...]),
        out=lambda lhs: _dot_packed(lhs, w["w_out"][...]),
        ple_gate=lambda lhs: _dot_packed(lhs, w["w_ple_gate"][...]),
        ple=lambda lhs: _dot_packed(lhs, w["w_ple"][...]))


def _run_layout(tile):
    steps = tile // SUBLANES
    return steps, steps + SUBLANES


def _store_runs(buf, value, steps, pitch):
    for l in range(value.shape[1] // LANES):
        for s in range(SUBLANES):
            buf[l, s * pitch:s * pitch + steps, :] = (
                value[s * steps:(s + 1) * steps, l * LANES:(l + 1) * LANES])


def _load_runs(buf, steps, pitch):
    return jnp.concatenate(
        [jnp.concatenate([buf[l, s * pitch:s * pitch + steps, :] for l in range(buf.shape[0])],
                         axis=1) for s in range(SUBLANES)], axis=0)


def _load_step(buf, j, pitch):
    return jnp.concatenate([buf[l, pl.ds(j, SUBLANES, stride=pitch), :]
                            for l in range(buf.shape[0])], axis=1)


def _store_step(buf, j, pitch, slab):
    for l in range(buf.shape[0]):
        buf[l, pl.ds(j, SUBLANES, stride=pitch), :] = slab[:, l * LANES:(l + 1) * LANES]


def _from_previous_run(slab, first_run_rows):
    first = lax.broadcasted_iota(jnp.int32, slab.shape, 0) == 0
    return jnp.where(first, first_run_rows, pltpu.roll(slab, 1, 0))


def _conv_by_step(xs, conv_state, w):
    steps = len(xs)
    last = SUBLANES - 1
    before = [_from_previous_run(xs[steps - k], conv_state[SUBLANES - k:SUBLANES - k + 1, :])
              for k in range(CONV_WIDTH - 1, 0, -1)]
    for k in range(1, CONV_WIDTH):
        conv_state[SUBLANES - k:SUBLANES - k + 1, :] = xs[steps - k][last:, :]
    ext = before + xs
    out = []
    for j in range(steps):
        acc = w["conv_b"][...] + w["conv_w"][0:1, :] * ext[j]
        for k in range(1, CONV_WIDTH):
            acc = acc + w["conv_w"][k:k + 1, :] * ext[j + k]
        out.append(acc)
    return out


def _scan_by_step(a, b, h_prev):
    steps = a.shape[0] // SUBLANES
    slab = lambda v, j: v[j * SUBLANES:(j + 1) * SUBLANES, :]
    h0, decay = [slab(b, 0)], [slab(a, 0)]
    for j in range(1, steps):
        h0.append(slab(a, j) * h0[-1] + slab(b, j))
        decay.append(slab(a, j) * decay[-1])
    end_h, end_decay = h0[-1], decay[-1]
    row = lax.broadcasted_iota(jnp.int32, end_h.shape, 0)
    shift = 1
    while shift < SUBLANES:
        keep = row >= shift
        end_h = end_h + end_decay * jnp.where(keep, pltpu.roll(end_h, shift, 0), 0.0)
        end_decay = end_decay * jnp.where(keep, pltpu.roll(end_decay, shift, 0), 1.0)
        shift *= 2
    ends = end_h + end_decay * h_prev
    entering = _from_previous_run(ends, h_prev)
    hs = [h0[j] + decay[j] * entering for j in range(steps)]
    return hs, ends[SUBLANES - 1:, :]


def _spatial_mix(z, sgw_ref, sgb_ref):
    tri = (lax.broadcasted_iota(jnp.int32, (CHUNK, CHUNK), 0)
           >= lax.broadcasted_iota(jnp.int32, (CHUNK, CHUNK), 1))
    s_rows = []
    for c in range(z.shape[0] // CHUNK):
        s_heads = []
        for h in range(SG_HEADS):
            wm = jnp.where(tri, sgw_ref[h], 0.0).astype(_BF16)
            zc = z[c * CHUNK:(c + 1) * CHUNK, h * SG_HEAD_DIM:(h + 1) * SG_HEAD_DIM]
            s_heads.append(_dot(wm, zc) + sgb_ref[:, h:h + 1])
        s_rows.append(jnp.concatenate(s_heads, axis=-1))
    return jnp.concatenate(s_rows, axis=0)


def _prompt_kernel(*refs):
    n_w = len(_VECTOR_NAMES) + len(_MATRIX_NAMES)
    x_ref, p_ref = refs[0], refs[1]
    w = dict(zip(_VECTOR_NAMES + _MATRIX_NAMES, refs[2:2 + n_w]))
    sgw_ref, sgb_ref = refs[2 + n_w], refs[3 + n_w]
    y_ref, conv_out_ref, lru_out_ref = refs[4 + n_w:7 + n_w]
    runs_buf, conv_state, h_state = refs[7 + n_w:]

    t = pl.program_id(1)
    tile = x_ref.shape[0]

    @pl.when(t == 0)
    def _():
        conv_state[...] = jnp.zeros_like(conv_state)
        h_state[...] = jnp.zeros_like(h_state)

    x = x_ref[...]
    hb = (_rms_scale(x) * w["norm_pre"][...]).astype(_BF16)

    steps, pitch = _run_layout(tile)
    _store_runs(runs_buf, _in_proj(hb, w, COL_X_LRU), steps, pitch)

    v = _in_proj(hb, w, COL_V)

    xs = [_load_step(runs_buf, j, pitch) for j in range(steps)]
    xc = jnp.concatenate(_conv_by_step(xs, conv_state, w), axis=0)
    xcb = xc.astype(_BF16)
    gates, wide = [], []
    for pair, col in enumerate((COL_U, COL_G_SG, COL_G_LRU, None)):
        for n in (2 * pair, 2 * pair + 1):
            sl = slice(n * LRU_BLOCK_DIM, (n + 1) * LRU_BLOCK_DIM)
            wg = w["w_gate"][_packed_rows(n * LRU_BLOCK_DIM, LRU_BLOCK_DIM), :]
            gates.append(_dot_packed(xcb[:, sl], wg))
        if col is not None:
            wide.append(_in_proj(hb, w, col))
    u, g_sg, g_lru = wide
    r_pre = jnp.concatenate([g[:, :LRU_BLOCK_DIM] for g in gates], axis=-1)
    i_pre = jnp.concatenate([g[:, LRU_BLOCK_DIM:] for g in gates], axis=-1)

    z = _head_norm(v, w["sg_norm"][...]).astype(_BF16)
    s = _spatial_mix(z, sgw_ref, sgb_ref)
    y_sg = u * s * _silu(g_sg)

    a, bterm = _lru_coeffs(xc, r_pre, i_pre, w)
    hs, h_last = _scan_by_step(a, bterm, h_state[...])
    h_state[...] = h_last
    for j in range(steps):
        _store_step(runs_buf, j, pitch, hs[j])
    y_lru = _load_runs(runs_buf, steps, pitch) * _silu(g_lru)

    y_ref[...] = _post_mix(x, hb, y_sg, y_lru, p_ref[...], w, _resident_matmuls(w))

    @pl.when(t == pl.num_programs(1) - 1)
    def _():
        seq = pl.ds(pl.program_id(0), 1)
        for k in range(CONV_WIDTH - 1):
            row = SUBLANES - (CONV_WIDTH - 1) + k
            conv_out_ref[k, seq, :] = conv_state[row:row + 1, :]
        lru_out_ref[seq, :] = h_last


def _resident(shape):
    return pl.BlockSpec(shape, lambda *_: (0,) * len(shape), pipeline_mode=pl.Buffered(1))


def _prompt_call(x, p, vectors, matrices, sgw, sgb_t):
    batch, seq, d = x.shape
    tile = PROMPT_TILE
    grid = (batch, seq // tile)
    in_specs = [
        pl.BlockSpec((None, tile, d), lambda b, t: (b, t, 0)),
        pl.BlockSpec((None, tile, PLE_DIM), lambda b, t: (b, t, 0)),
    ]
    in_specs += [_resident(v.shape) for v in vectors + matrices]
    in_specs += [_resident(sgw.shape), _resident(sgb_t.shape)]
    out_shape = (
        jax.ShapeDtypeStruct((batch, seq, d), _F32),
        jax.ShapeDtypeStruct((CONV_WIDTH - 1, batch, d), _F32),
        jax.ShapeDtypeStruct((batch, d), _F32),
    )
    out_specs = (
        pl.BlockSpec((None, tile, d), lambda b, t: (b, t, 0)),
        pl.BlockSpec((CONV_WIDTH - 1, batch, d), lambda b, t: (0, 0, 0)),
        pl.BlockSpec((batch, d), lambda b, t: (0, 0)),
    )
    return pl.pallas_call(
        _prompt_kernel,
        grid=grid,
        in_specs=in_specs,
        out_specs=out_specs,
        out_shape=out_shape,
        scratch_shapes=[
            pltpu.VMEM((d // LANES, SUBLANES * _run_layout(tile)[1], LANES), _F32),
            pltpu.VMEM((SUBLANES, d), _F32),
            pltpu.VMEM((1, d), _F32),
        ],
        compiler_params=pltpu.CompilerParams(
            dimension_semantics=("arbitrary", "arbitrary"),
            vmem_limit_bytes=VMEM_LIMIT_BYTES),
        name="prompt_layer",
    )(x, p, *vectors, *matrices, sgw, sgb_t)


_SOURCE_NAMES = ("w_in", "w_merge", "w_branch_sg", "w_branch_lru", "w_out", "w_ple_gate",
                 "w_ple", "lru_wa", "lru_wx")


def _column_chunks(ref, matrix, first, last):
    return [([(ref, c, min(STAGE_COLS, last - c), 0)], matrix, c)
            for c in range(first, last, STAGE_COLS)]


class _WeightStream:
    def __init__(self, tasks, out, stage, ring, sem_in, sem_out):
        self.tasks, self.out, self.stage, self.ring = tasks, out, stage, ring
        self.sem_in, self.sem_out = sem_in, sem_out
        self.taken = 0
        for i in range(min(STAGE_SLOTS - 1, len(tasks))):
            self._start_in(i)

    def _shape(self, i):
        parts = self.tasks[i][0]
        return parts[0][0].shape[0], sum(part[2] for part in parts)

    def _in_copies(self, i):
        slot = i % STAGE_SLOTS
        rows, _ = self._shape(i)
        return [pltpu.make_async_copy(
            ref.at[:, pl.ds(col, cols)],
            self.stage.at[slot, pl.ds(0, rows), pl.ds(at, cols)],
            self.sem_in.at[slot]) for ref, col, cols, at in self.tasks[i][0]]

    def _start_in(self, i):
        for copy in self._in_copies(i):
            copy.start()

    def _out_copy(self, i):
        slot = i % RING_SLOTS
        rows, cols = self._shape(i)
        _, matrix, first = self.tasks[i]
        return pltpu.make_async_copy(
            self.ring.at[slot, pl.ds(0, rows // ROWS_PER_WORD), pl.ds(0, cols)],
            self.out[matrix].at[:, pl.ds(first, cols)],
            self.sem_out.at[slot])

    def take(self):
        i = self.taken
        self.taken += 1
        if i + STAGE_SLOTS - 1 < len(self.tasks):
            self._start_in(i + STAGE_SLOTS - 1)
        for copy in self._in_copies(i):
            copy.wait()
        if i >= RING_SLOTS:
            self._out_copy(i - RING_SLOTS).wait()
        rows, cols = self._shape(i)
        staged = self.stage[i % STAGE_SLOTS, 0:rows, 0:cols]
        words = rows // ROWS_PER_WORD
        self.ring[i % RING_SLOTS, 0:words, 0:cols] = _pack_rows(staged.astype(_BF16))
        self._out_copy(i).start()
        return self.ring.at[i % RING_SLOTS, pl.ds(0, words), pl.ds(0, cols)]

    def matmul(self, lhs, chunks):
        return jnp.concatenate([_dot_packed(lhs, self.take()[...]) for _ in range(chunks)],
                               axis=-1)

    def finish(self):
        assert self.taken == len(self.tasks)
        for i in range(max(0, len(self.tasks) - RING_SLOTS), len(self.tasks)):
            self._out_copy(i).wait()


def _sample_tasks(src):
    d = D_MODEL
    half = lambda name, h: _column_chunks(src[name], name, h * d, (h + 1) * d)
    gate = [([(src["lru_wa"], 0, LRU_BLOCK_DIM, 0),
              (src["lru_wx"], 0, LRU_BLOCK_DIM, LRU_BLOCK_DIM)], "w_gate", 0)]
    return (_column_chunks(src["w_in"], "w_in", 0, 5 * d) + gate
            + half("w_merge", 0) + half("w_branch_sg", 0) + half("w_merge", 1)
            + half("w_branch_lru", 0) + half("w_out", 0) + half("w_ple_gate", 0)
            + half("w_ple", 0))


def _sample_kernel(*refs, steps):
    n_v, n_s, n_m = len(_VECTOR_NAMES), len(_SOURCE_NAMES), len(_MATRIX_NAMES)
    x_ref, p_ref, conv_in_ref, h0_ref = refs[:4]
    w = dict(zip(_VECTOR_NAMES, refs[4:4 + n_v]))
    sgw_ref, sgb_ref = refs[4 + n_v], refs[5 + n_v]
    src = dict(zip(_SOURCE_NAMES, refs[6 + n_v:6 + n_v + n_s]))
    outs = refs[6 + n_v + n_s:]
    y_ref, conv_out_ref, lru_out_ref, z_out_ref = outs[:4]
    out_packed = dict(zip(_MATRIX_NAMES, outs[4:4 + n_m]))
    stage, ring, sem_in, sem_out = outs[4 + n_m:]
    d = D_MODEL
    nb = x_ref.shape[0]

    stream = _WeightStream(_sample_tasks(src), out_packed, stage, ring, sem_in, sem_out)
    chunks_per_block = d // STAGE_COLS

    def slab(v, t):
        return v[t * nb:(t + 1) * nb]

    x = jnp.concatenate([x_ref[:, t, :] for t in range(steps)], axis=0)
    p = jnp.concatenate([p_ref[:, t, :] for t in range(steps)], axis=0)
    hb = (_rms_scale(x) * w["norm_pre"][...]).astype(_BF16)
    proj = stream.matmul(hb, 5 * chunks_per_block)
    col = lambda c: proj[:, c * d:(c + 1) * d]

    z = _head_norm(col(COL_V), w["sg_norm"][...])
    for t in range(steps):
        z_out_ref[:, t, :] = slab(z, t)
    s_slabs = []
    for t in range(steps):
        acc = jnp.broadcast_to(sgb_ref[t:t + 1, :], (nb, d))
        for u in range(t + 1):
            acc = acc + sgw_ref[t * steps + u:t * steps + u + 1, :] * slab(z, u)
        s_slabs.append(acc)
    s = jnp.concatenate(s_slabs, axis=0)
    y_sg = col(COL_U) * s * _silu(col(COL_G_SG))

    x_lru = col(COL_X_LRU)
    hist = [conv_in_ref[k] for k in range(CONV_WIDTH - 1)]
    hist += [slab(x_lru, t) for t in range(steps)]
    for k in range(CONV_WIDTH - 1):
        conv_out_ref[k] = hist[steps + k]
    xc_slabs = []
    for t in range(steps):
        acc = w["conv_b"][...] + w["conv_w"][0:1, :] * hist[t]
        for k in range(1, CONV_WIDTH):
            acc = acc + w["conv_w"][k:k + 1, :] * hist[t + k]
        xc_slabs.append(acc)
    xc = jnp.concatenate(xc_slabs, axis=0)
    r_pre, i_pre = _lru_gates(xc.astype(_BF16), stream.take())
    a, bterm = _lru_coeffs(xc, r_pre, i_pre, w)
    h = h0_ref[...]
    h_slabs = []
    for t in range(steps):
        h = slab(a, t) * h + slab(bterm, t)
        h_slabs.append(h)
    lru_out_ref[...] = h
    y_lru = jnp.concatenate(h_slabs, axis=0) * _silu(col(COL_G_LRU))

    block = lambda lhs: stream.matmul(lhs, chunks_per_block)
    mm = {name: block for name in
          ("merge_a", "branch_sg", "merge_b", "branch_lru", "out", "ple_gate", "ple")}
    y = _post_mix(x, hb, y_sg, y_lru, p, w, mm)
    for t in range(steps):
        y_ref[:, t, :] = slab(y, t)

    stream.finish()


def _sample_call(x, p, conv_in, h0, vectors, sources, sgw_rows, sgb_rows):
    nb, steps, d = x.shape
    vmem = pl.BlockSpec(memory_space=pltpu.VMEM)
    hbm = pl.BlockSpec(memory_space=pl.ANY)
    matrix_shapes = [_MATRIX_SHAPES[m] for m in _MATRIX_NAMES]
    stage_rows = max(s.shape[0] for s in sources)
    out_shape = (
        jax.ShapeDtypeStruct((nb, steps, d), _F32),
        jax.ShapeDtypeStruct((CONV_WIDTH - 1, nb, d), _F32),
        jax.ShapeDtypeStruct((nb, d), _F32),
        jax.ShapeDtypeStruct((nb, steps, d), _F32),
    ) + tuple(jax.ShapeDtypeStruct((rows // ROWS_PER_WORD, cols), _U32)
              for rows, cols in matrix_shapes)
    outs = pl.pallas_call(
        functools.partial(_sample_kernel, steps=steps),
        in_specs=[vmem] * (6 + len(vectors)) + [hbm] * len(sources),
        out_specs=(vmem,) * 4 + (hbm,) * len(matrix_shapes),
        out_shape=out_shape,
        scratch_shapes=[
            pltpu.VMEM((STAGE_SLOTS, stage_rows, STAGE_COLS), _F32),
            pltpu.VMEM((RING_SLOTS, stage_rows // ROWS_PER_WORD, STAGE_COLS), _U32),
            pltpu.SemaphoreType.DMA((STAGE_SLOTS,)),
            pltpu.SemaphoreType.DMA((RING_SLOTS,)),
        ],
        compiler_params=pltpu.CompilerParams(vmem_limit_bytes=VMEM_LIMIT_BYTES),
        name="sample_layer",
    )(x, p, conv_in, h0, *vectors, sgw_rows, sgb_rows, *sources)
    return outs[:4], list(outs[4:])


def kernel(x_prompt, x_sample, p_prompt, p_sample, state_conv, state_lru, norm_pre, w_in, sg_norm,
           sg_w, sg_b, conv_w, conv_b, lru_wa, lru_ba, lru_wx, lru_bx, lru_lambda, w_branch_sg,
           w_branch_lru, w_merge, b_merge, w_out, norm_post, w_ple, w_ple_gate, b_ple_gate):
    depth = norm_pre.shape[0]
    nb, steps, d = x_sample.shape
    xp, xs = x_prompt, x_sample
    conv_p, lru_p, conv_s, lru_s, chunk_s = [], [], [], [], []
    for l in range(depth):
        row = lambda v: v[l].reshape(1, -1)
        by_name = dict(
            norm_pre=row(norm_pre), sg_norm=row(sg_norm), conv_w=conv_w[l], conv_b=row(conv_b),
            lru_ba=row(lru_ba), lru_bx=row(lru_bx), lru_lambda=row(lru_lambda),
            b_merge=row(b_merge), norm_post=row(norm_post), b_ple_gate=row(b_ple_gate))
        vectors = [by_name[n] for n in _VECTOR_NAMES]
        f32_by_name = dict(
            w_in=w_in[l], w_merge=w_merge[l], w_branch_sg=w_branch_sg[l],
            w_branch_lru=w_branch_lru[l], w_out=w_out[l], w_ple_gate=w_ple_gate[l],
            w_ple=w_ple[l], lru_wa=lru_wa[l].reshape(d, LRU_BLOCK_DIM),
            lru_wx=lru_wx[l].reshape(d, LRU_BLOCK_DIM))
        sources = [f32_by_name[name] for name in _SOURCE_NAMES]
        sgw_rows = jnp.repeat(sg_w[l][:, :steps, :steps].reshape(SG_HEADS, steps * steps).T,
                              SG_HEAD_DIM, axis=1)
        sgb_rows = jnp.repeat(sg_b[l][:, :steps].T, SG_HEAD_DIM, axis=1)
        (xs, cs, hs, zs), matrices = _sample_call(
            xs, p_sample[l], jnp.swapaxes(state_conv[l], 0, 1), state_lru[l], vectors, sources,
            sgw_rows, sgb_rows)
        conv_s.append(jnp.swapaxes(cs, 0, 1))
        lru_s.append(hs)
        chunk_s.append(zs)
        xp, cp, hp = _prompt_call(xp, p_prompt[l], vectors, matrices, sg_w[l], sg_b[l].T)
        conv_p.append(jnp.swapaxes(cp, 0, 1))
        lru_p.append(hp)
    return (xp, xs, jnp.stack(conv_p), jnp.stack(lru_p),
            jnp.stack(conv_s), jnp.stack(lru_s), jnp.stack(chunk_s))
```

```python
import functools

import jax
import jax.numpy as jnp
from jax import lax
from jax.experimental import pallas as pl
from jax.experimental.pallas import tpu as pltpu

D_MODEL = 1024
PLE_DIM = 256
SG_HEADS = 4
SG_HEAD_DIM = D_MODEL // SG_HEADS
CHUNK = 128
LRU_BLOCKS = 8
LRU_BLOCK_DIM = D_MODEL // LRU_BLOCKS
CONV_WIDTH = 4
LRU_C = 8.0
EPS = 1e-6

SUBLANES = 8
LANES = 128
PROMPT_TILE = 256
VMEM_LIMIT_BYTES = 56 * 1024 * 1024
STAGE_COLS = 1024
STAGE_SLOTS = 4
RING_SLOTS = 4

_BF16 = jnp.bfloat16
_F32 = jnp.float32
_U32 = jnp.uint32
ROWS_PER_WORD = 2

_VECTOR_NAMES = ("norm_pre", "sg_norm", "conv_w", "conv_b", "lru_ba", "lru_bx", "lru_lambda",
                 "b_merge", "norm_post", "b_ple_gate")
_MATRIX_SHAPES = dict(
    w_in=(D_MODEL, 5 * D_MODEL), w_merge=(D_MODEL, 2 * D_MODEL),
    w_branch_sg=(D_MODEL, D_MODEL), w_branch_lru=(D_MODEL, D_MODEL), w_out=(D_MODEL, D_MODEL),
    w_ple_gate=(D_MODEL, D_MODEL), w_ple=(PLE_DIM, D_MODEL),
    w_gate=(D_MODEL, 2 * LRU_BLOCK_DIM))
_MATRIX_NAMES = tuple(_MATRIX_SHAPES)

COL_U, COL_V, COL_G_SG, COL_X_LRU, COL_G_LRU = range(5)


def _dot(a, b):
    return jnp.dot(a, b, preferred_element_type=_F32)


def _pack_rows(wb):
    return pltpu.bitcast(wb, _U32)


def _dot_packed(a, words):
    return _dot(a, pltpu.bitcast(words, _BF16))


def _packed_rows(first, rows):
    return slice(first // ROWS_PER_WORD, (first + rows) // ROWS_PER_WORD)


def _rms_scale(x):
    var = jnp.mean(x * x, axis=-1, keepdims=True)
    return x * lax.rsqrt(var + EPS)


def _silu(x):
    return x * jax.nn.sigmoid(x)


def _in_proj(hb, w, col):
    return _dot_packed(hb, w["w_in"][:, col * D_MODEL:(col + 1) * D_MODEL])


def _merge_pre(hb, w, half):
    return _dot_packed(hb, w["w_merge"][:, half * D_MODEL:(half + 1) * D_MODEL])


def _head_norm(v, sgn):
    zs = []
    for h in range(SG_HEADS):
        sl = slice(h * SG_HEAD_DIM, (h + 1) * SG_HEAD_DIM)
        zs.append(_rms_scale(v[:, sl]) * sgn[:, sl])
    return jnp.concatenate(zs, axis=-1)


def _lru_gates(xcb, wg_ref):
    gates = []
    for n in range(LRU_BLOCKS):
        sl = slice(n * LRU_BLOCK_DIM, (n + 1) * LRU_BLOCK_DIM)
        wg = wg_ref[_packed_rows(n * LRU_BLOCK_DIM, LRU_BLOCK_DIM), :]
        gates.append(_dot_packed(xcb[:, sl], wg))
    r_pre = jnp.concatenate([g[:, :LRU_BLOCK_DIM] for g in gates], axis=-1)
    i_pre = jnp.concatenate([g[:, LRU_BLOCK_DIM:] for g in gates], axis=-1)
    return r_pre, i_pre


def _lru_coeffs(xc, r_pre, i_pre, w):
    r = jax.nn.sigmoid(r_pre + w["lru_ba"][...])
    i = jax.nn.sigmoid(i_pre + w["lru_bx"][...])
    log_a = (-LRU_C * jax.nn.softplus(-w["lru_lambda"][...])) * r
    a = jnp.exp(log_a)
    mult = jnp.sqrt(1.0 - a * a)
    return a, mult * (i * xc)


def _merge_gate(pre, w, half):
    return jax.nn.sigmoid(pre + w["b_merge"][:, half * D_MODEL:(half + 1) * D_MODEL])


def _post_mix(x, hb, y_sg, y_lru, p, w, mm):
    g_a = _merge_gate(mm["merge_a"](hb), w, 0)
    merged = g_a * mm["branch_sg"](y_sg.astype(_BF16))
    g_b = _merge_gate(mm["merge_b"](hb), w, 1)
    merged = merged + g_b * mm["branch_lru"](y_lru.astype(_BF16))
    o = mm["out"](merged.astype(_BF16))
    x1 = x + _rms_scale(o) * w["norm_post"][...]
    gate = jax.nn.sigmoid(mm["ple_gate"](x1.astype(_BF16)) + w["b_ple_gate"][...])
    return x1 + gate * mm["ple"](p.astype(_BF16))


def _resident_matmuls(w):
    return dict(
        merge_a=lambda lhs: _merge_pre(lhs, w, 0), merge_b=lambda lhs: _merge_pre(lhs, w, 1),
        branch_sg=lambda lhs: _dot_packed(lhs, w["w_branch_sg"][...]),
        branch_lru=lambda lhs: _dot_packed(lhs, w["w_branch_lru"][...]),
        out=lambda lhs: _dot_packed(lhs, w["w_out"][...]),
        ple_gate=lambda lhs: _dot_packed(lhs, w["w_ple_gate"][...]),
        ple=lambda lhs: _dot_packed(lhs, w["w_ple"][...]))


def _run_layout(tile):
    steps = tile // SUBLANES
    return steps, steps + SUBLANES


def _store_runs(buf, value, steps, pitch):
    for l in range(value.shape[1] // LANES):
        for s in range(SUBLANES):
            buf[l, s * pitch:s * pitch + steps, :] = (
                value[s * steps:(s + 1) * steps, l * LANES:(l + 1) * LANES])


def _load_runs(buf, steps, pitch):
    return jnp.concatenate(
        [jnp.concatenate([buf[l, s * pitch:s * pitch + steps, :] for l in range(buf.shape[0])],
                         axis=1) for s in range(SUBLANES)], axis=0)


def _load_step(buf, j, pitch):
    return jnp.concatenate([buf[l, pl.ds(j, SUBLANES, stride=pitch), :]
                            for l in range(buf.shape[0])], axis=1)


def _store_step(buf, j, pitch, slab):
    for l in range(buf.shape[0]):
        buf[l, pl.ds(j, SUBLANES, stride=pitch), :] = slab[:, l * LANES:(l + 1) * LANES]


def _from_previous_run(slab, first_run_rows):
    first = lax.broadcasted_iota(jnp.int32, slab.shape, 0) == 0
    return jnp.where(first, first_run_rows, pltpu.roll(slab, 1, 0))


def _conv_by_step(xs, conv_state, w):
    steps = len(xs)
    last = SUBLANES - 1
    before = [_from_previous_run(xs[steps - k], conv_state[SUBLANES - k:SUBLANES - k + 1, :])
              for k in range(CONV_WIDTH - 1, 0, -1)]
    for k in range(1, CONV_WIDTH):
        conv_state[SUBLANES - k:SUBLANES - k + 1, :] = xs[steps - k][last:, :]
    ext = before + xs
    out = []
    for j in range(steps):
        acc = w["conv_b"][...] + w["conv_w"][0:1, :] * ext[j]
        for k in range(1, CONV_WIDTH):
            acc = acc + w["conv_w"][k:k + 1, :] * ext[j + k]
        out.append(acc)
    return out


def _scan_by_step(a, b, h_prev):
    steps = a.shape[0] // SUBLANES
    slab = lambda v, j: v[j * SUBLANES:(j + 1) * SUBLANES, :]
    h0, decay = [slab(b, 0)], [slab(a, 0)]
    for j in range(1, steps):
        h0.append(slab(a, j) * h0[-1] + slab(b, j))
        decay.append(slab(a, j) * decay[-1])
    end_h, end_decay = h0[-1], decay[-1]
    row = lax.broadcasted_iota(jnp.int32, end_h.shape, 0)
    shift = 1
    while shift < SUBLANES:
        keep = row >= shift
        end_h = end_h + end_decay * jnp.where(keep, pltpu.roll(end_h, shift, 0), 0.0)
        end_decay = end_decay * jnp.where(keep, pltpu.roll(end_decay, shift, 0), 1.0)
        shift *= 2
    ends = end_h + end_decay * h_prev
    entering = _from_previous_run(ends, h_prev)
    hs = [h0[j] + decay[j] * entering for j in range(steps)]
    return hs, ends[SUBLANES - 1:, :]


def _spatial_mix_chunk(z, c, sgw_ref, sgb_ref):
    tri = (lax.broadcasted_iota(jnp.int32, (CHUNK, CHUNK), 0)
           >= lax.broadcasted_iota(jnp.int32, (CHUNK, CHUNK), 1))
    s_heads = []
    for h in range(SG_HEADS):
        wm = jnp.where(tri, sgw_ref[h], 0.0).astype(_BF16)
        zc = z[c * CHUNK:(c + 1) * CHUNK, h * SG_HEAD_DIM:(h + 1) * SG_HEAD_DIM]
        s_heads.append(_dot(wm, zc) + sgb_ref[:, h:h + 1])
    return jnp.concatenate(s_heads, axis=-1)


def _prompt_kernel(*refs):
    n_w = len(_VECTOR_NAMES) + len(_MATRIX_NAMES)
    x_ref, p_ref = refs[0], refs[1]
    w = dict(zip(_VECTOR_NAMES + _MATRIX_NAMES, refs[2:2 + n_w]))
    sgw_ref, sgb_ref = refs[2 + n_w], refs[3 + n_w]
    y_ref, conv_out_ref, lru_out_ref = refs[4 + n_w:7 + n_w]
    runs_buf, conv_state, h_state = refs[7 + n_w:]

    t = pl.program_id(1)
    tile = x_ref.shape[0]

    @pl.when(t == 0)
    def _():
        conv_state[...] = jnp.zeros_like(conv_state)
        h_state[...] = jnp.zeros_like(h_state)

    x = x_ref[...]
    hb = (_rms_scale(x) * w["norm_pre"][...]).astype(_BF16)

    steps, pitch = _run_layout(tile)
    _store_runs(runs_buf, _in_proj(hb, w, COL_X_LRU), steps, pitch)

    v = _in_proj(hb, w, COL_V)
    z = _head_norm(v, w["sg_norm"][...]).astype(_BF16)

    xs = [_load_step(runs_buf, j, pitch) for j in range(steps)]
    xc = jnp.concatenate(_conv_by_step(xs, conv_state, w), axis=0)
    xcb = xc.astype(_BF16)
    gates, wide, s_chunks = [], [], []
    for pair, col in enumerate((COL_U, COL_G_SG, COL_G_LRU, None)):
        for n in (2 * pair, 2 * pair + 1):
            sl = slice(n * LRU_BLOCK_DIM, (n + 1) * LRU_BLOCK_DIM)
            wg = w["w_gate"][_packed_rows(n * LRU_BLOCK_DIM, LRU_BLOCK_DIM), :]
            gates.append(_dot_packed(xcb[:, sl], wg))
        if col is not None:
            wide.append(_in_proj(hb, w, col))
        if pair < tile // CHUNK:
            s_chunks.append(_spatial_mix_chunk(z, pair, sgw_ref, sgb_ref))
    u, g_sg, g_lru = wide
    r_pre = jnp.concatenate([g[:, :LRU_BLOCK_DIM] for g in gates], axis=-1)
    i_pre = jnp.concatenate([g[:, LRU_BLOCK_DIM:] for g in gates], axis=-1)

    y_sg = u * jnp.concatenate(s_chunks, axis=0) * _silu(g_sg)

    a, bterm = _lru_coeffs(xc, r_pre, i_pre, w)
    hs, h_last = _scan_by_step(a, bterm, h_state[...])
    h_state[...] = h_last
    for j in range(steps):
        _store_step(runs_buf, j, pitch, hs[j])
    y_lru = _load_runs(runs_buf, steps, pitch) * _silu(g_lru)

    y_ref[...] = _post_mix(x, hb, y_sg, y_lru, p_ref[...], w, _resident_matmuls(w))

    @pl.when(t == pl.num_programs(1) - 1)
    def _():
        seq = pl.ds(pl.program_id(0), 1)
        for k in range(CONV_WIDTH - 1):
            row = SUBLANES - (CONV_WIDTH - 1) + k
            conv_out_ref[k, seq, :] = conv_state[row:row + 1, :]
        lru_out_ref[seq, :] = h_last


def _resident(shape):
    return pl.BlockSpec(shape, lambda *_: (0,) * len(shape), pipeline_mode=pl.Buffered(1))


def _prompt_call(x, p, vectors, matrices, sgw, sgb_t):
    batch, seq, d = x.shape
    tile = PROMPT_TILE
    grid = (batch, seq // tile)
    in_specs = [
        pl.BlockSpec((None, tile, d), lambda b, t: (b, t, 0)),
        pl.BlockSpec((None, tile, PLE_DIM), lambda b, t: (b, t, 0)),
    ]
    in_specs += [_resident(v.shape) for v in vectors + matrices]
    in_specs += [_resident(sgw.shape), _resident(sgb_t.shape)]
    out_shape = (
        jax.ShapeDtypeStruct((batch, seq, d), _F32),
        jax.ShapeDtypeStruct((CONV_WIDTH - 1, batch, d), _F32),
        jax.ShapeDtypeStruct((batch, d), _F32),
    )
    out_specs = (
        pl.BlockSpec((None, tile, d), lambda b, t: (b, t, 0)),
        pl.BlockSpec((CONV_WIDTH - 1, batch, d), lambda b, t: (0, 0, 0)),
        pl.BlockSpec((batch, d), lambda b, t: (0, 0)),
    )
    return pl.pallas_call(
        _prompt_kernel,
        grid=grid,
        in_specs=in_specs,
        out_specs=out_specs,
        out_shape=out_shape,
        scratch_shapes=[
            pltpu.VMEM((d // LANES, SUBLANES * _run_layout(tile)[1], LANES), _F32),
            pltpu.VMEM((SUBLANES, d), _F32),
            pltpu.VMEM((1, d), _F32),
        ],
        compiler_params=pltpu.CompilerParams(
            dimension_semantics=("arbitrary", "arbitrary"),
            vmem_limit_bytes=VMEM_LIMIT_BYTES),
        name="prompt_layer",
    )(x, p, *vectors, *matrices, sgw, sgb_t)


_SOURCE_NAMES = ("w_in", "w_merge", "w_branch_sg", "w_branch_lru", "w_out", "w_ple_gate",
                 "w_ple", "lru_wa", "lru_wx")


def _column_chunks(ref, matrix, first, last):
    return [([(ref, c, min(STAGE_COLS, last - c), 0)], matrix, c)
            for c in range(first, last, STAGE_COLS)]


class _WeightStream:
    def __init__(self, tasks, out, stage, ring, sem_in, sem_out):
        self.tasks, self.out, self.stage, self.ring = tasks, out, stage, ring
        self.sem_in, self.sem_out = sem_in, sem_out
        self.taken = 0
        for i in range(min(STAGE_SLOTS - 1, len(tasks))):
            self._start_in(i)

    def _shape(self, i):
        parts = self.tasks[i][0]
        return parts[0][0].shape[0], sum(part[2] for part in parts)

    def _in_copies(self, i):
        slot = i % STAGE_SLOTS
        rows, _ = self._shape(i)
        return [pltpu.make_async_copy(
            ref.at[:, pl.ds(col, cols)],
            self.stage.at[slot, pl.ds(0, rows), pl.ds(at, cols)],
            self.sem_in.at[slot]) for ref, col, cols, at in self.tasks[i][0]]

    def _start_in(self, i):
        for copy in self._in_copies(i):
            copy.start()

    def _out_copy(self, i):
        slot = i % RING_SLOTS
        rows, cols = self._shape(i)
        _, matrix, first = self.tasks[i]
        return pltpu.make_async_copy(
            self.ring.at[slot, pl.ds(0, rows // ROWS_PER_WORD), pl.ds(0, cols)],
            self.out[matrix].at[:, pl.ds(first, cols)],
            self.sem_out.at[slot])

    def take(self):
        i = self.taken
        self.taken += 1
        if i + STAGE_SLOTS - 1 < len(self.tasks):
            self._start_in(i + STAGE_SLOTS - 1)
        for copy in self._in_copies(i):
            copy.wait()
        if i >= RING_SLOTS:
            self._out_copy(i - RING_SLOTS).wait()
        rows, cols = self._shape(i)
        staged = self.stage[i % STAGE_SLOTS, 0:rows, 0:cols]
        words = rows // ROWS_PER_WORD
        self.ring[i % RING_SLOTS, 0:words, 0:cols] = _pack_rows(staged.astype(_BF16))
        self._out_copy(i).start()
        return self.ring.at[i % RING_SLOTS, pl.ds(0, words), pl.ds(0, cols)]

    def matmul(self, lhs, chunks):
        return jnp.concatenate([_dot_packed(lhs, self.take()[...]) for _ in range(chunks)],
                               axis=-1)

    def finish(self):
        assert self.taken == len(self.tasks)
        for i in range(max(0, len(self.tasks) - RING_SLOTS), len(self.tasks)):
            self._out_copy(i).wait()


def _sample_tasks(src):
    d = D_MODEL
    half = lambda name, h: _column_chunks(src[name], name, h * d, (h + 1) * d)
    gate = [([(src["lru_wa"], 0, LRU_BLOCK_DIM, 0),
              (src["lru_wx"], 0, LRU_BLOCK_DIM, LRU_BLOCK_DIM)], "w_gate", 0)]
    return (_column_chunks(src["w_in"], "w_in", 0, 5 * d) + gate
            + half("w_merge", 0) + half("w_branch_sg", 0) + half("w_merge", 1)
            + half("w_branch_lru", 0) + half("w_out", 0) + half("w_ple_gate", 0)
            + half("w_ple", 0))


def _sample_kernel(*refs, steps):
    n_v, n_s, n_m = len(_VECTOR_NAMES), len(_SOURCE_NAMES), len(_MATRIX_NAMES)
    x_ref, p_ref, conv_in_ref, h0_ref = refs[:4]
    w = dict(zip(_VECTOR_NAMES, refs[4:4 + n_v]))
    sgw_ref, sgb_ref = refs[4 + n_v], refs[5 + n_v]
    src = dict(zip(_SOURCE_NAMES, refs[6 + n_v:6 + n_v + n_s]))
    outs = refs[6 + n_v + n_s:]
    y_ref, conv_out_ref, lru_out_ref, z_out_ref = outs[:4]
    out_packed = dict(zip(_MATRIX_NAMES, outs[4:4 + n_m]))
    stage, ring, sem_in, sem_out = outs[4 + n_m:]
    d = D_MODEL
    nb = x_ref.shape[0]

    stream = _WeightStream(_sample_tasks(src), out_packed, stage, ring, sem_in, sem_out)
    chunks_per_block = d // STAGE_COLS

    def slab(v, t):
        return v[t * nb:(t + 1) * nb]

    x = jnp.concatenate([x_ref[:, t, :] for t in range(steps)], axis=0)
    p = jnp.concatenate([p_ref[:, t, :] for t in range(steps)], axis=0)
    hb = (_rms_scale(x) * w["norm_pre"][...]).astype(_BF16)
    proj = stream.matmul(hb, 5 * chunks_per_block)
    col = lambda c: proj[:, c * d:(c + 1) * d]

    z = _head_norm(col(COL_V), w["sg_norm"][...])
    for t in range(steps):
        z_out_ref[:, t, :] = slab(z, t)
    s_slabs = []
    for t in range(steps):
        acc = jnp.broadcast_to(sgb_ref[t:t + 1, :], (nb, d))
        for u in range(t + 1):
            acc = acc + sgw_ref[t * steps + u:t * steps + u + 1, :] * slab(z, u)
        s_slabs.append(acc)
    s = jnp.concatenate(s_slabs, axis=0)
    y_sg = col(COL_U) * s * _silu(col(COL_G_SG))

    x_lru = col(COL_X_LRU)
    hist = [conv_in_ref[k] for k in range(CONV_WIDTH - 1)]
    hist += [slab(x_lru, t) for t in range(steps)]
    for k in range(CONV_WIDTH - 1):
        conv_out_ref[k] = hist[steps + k]
    xc_slabs = []
    for t in range(steps):
        acc = w["conv_b"][...] + w["conv_w"][0:1, :] * hist[t]
        for k in range(1, CONV_WIDTH):
            acc = acc + w["conv_w"][k:k + 1, :] * hist[t + k]
        xc_slabs.append(acc)
    xc = jnp.concatenate(xc_slabs, axis=0)
    r_pre, i_pre = _lru_gates(xc.astype(_BF16), stream.take())
    a, bterm = _lru_coeffs(xc, r_pre, i_pre, w)
    h = h0_ref[...]
    h_slabs = []
    for t in range(steps):
        h = slab(a, t) * h + slab(bterm, t)
        h_slabs.append(h)
    lru_out_ref[...] = h
    y_lru = jnp.concatenate(h_slabs, axis=0) * _silu(col(COL_G_LRU))

    block = lambda lhs: stream.matmul(lhs, chunks_per_block)
    mm = {name: block for name in
          ("merge_a", "branch_sg", "merge_b", "branch_lru", "out", "ple_gate", "ple")}
    y = _post_mix(x, hb, y_sg, y_lru, p, w, mm)
    for t in range(steps):
        y_ref[:, t, :] = slab(y, t)

    stream.finish()


def _sample_call(x, p, conv_in, h0, vectors, sources, sgw_rows, sgb_rows):
    nb, steps, d = x.shape
    vmem = pl.BlockSpec(memory_space=pltpu.VMEM)
    hbm = pl.BlockSpec(memory_space=pl.ANY)
    matrix_shapes = [_MATRIX_SHAPES[m] for m in _MATRIX_NAMES]
    stage_rows = max(s.shape[0] for s in sources)
    out_shape = (
        jax.ShapeDtypeStruct((nb, steps, d), _F32),
        jax.ShapeDtypeStruct((CONV_WIDTH - 1, nb, d), _F32),
        jax.ShapeDtypeStruct((nb, d), _F32),
        jax.ShapeDtypeStruct((nb, steps, d), _F32),
    ) + tuple(jax.ShapeDtypeStruct((rows // ROWS_PER_WORD, cols), _U32)
              for rows, cols in matrix_shapes)
    outs = pl.pallas_call(
        functools.partial(_sample_kernel, steps=steps),
        in_specs=[vmem] * (6 + len(vectors)) + [hbm] * len(sources),
        out_specs=(vmem,) * 4 + (hbm,) * len(matrix_shapes),
        out_shape=out_shape,
        scratch_shapes=[
            pltpu.VMEM((STAGE_SLOTS, stage_rows, STAGE_COLS), _F32),
            pltpu.VMEM((RING_SLOTS, stage_rows // ROWS_PER_WORD, STAGE_COLS), _U32),
            pltpu.SemaphoreType.DMA((STAGE_SLOTS,)),
            pltpu.SemaphoreType.DMA((RING_SLOTS,)),
        ],
        compiler_params=pltpu.CompilerParams(vmem_limit_bytes=VMEM_LIMIT_BYTES),
        name="sample_layer",
    )(x, p, conv_in, h0, *vectors, sgw_rows, sgb_rows, *sources)
    return outs[:4], list(outs[4:])


def kernel(x_prompt, x_sample, p_prompt, p_sample, state_conv, state_lru, norm_pre, w_in, sg_norm,
           sg_w, sg_b, conv_w, conv_b, lru_wa, lru_ba, lru_wx, lru_bx, lru_lambda, w_branch_sg,
           w_branch_lru, w_merge, b_merge, w_out, norm_post, w_ple, w_ple_gate, b_ple_gate):
    depth = norm_pre.shape[0]
    nb, steps, d = x_sample.shape
    xp, xs = x_prompt, x_sample
    conv_p, lru_p, conv_s, lru_s, chunk_s = [], [], [], [], []
    for l in range(depth):
        row = lambda v: v[l].reshape(1, -1)
        by_name = dict(
            norm_pre=row(norm_pre), sg_norm=row(sg_norm), conv_w=conv_w[l], conv_b=row(conv_b),
            lru_ba=row(lru_ba), lru_bx=row(lru_bx), lru_lambda=row(lru_lambda),
            b_merge=row(b_merge), norm_post=row(norm_post), b_ple_gate=row(b_ple_gate))
        vectors = [by_name[n] for n in _VECTOR_NAMES]
        f32_by_name = dict(
            w_in=w_in[l], w_merge=w_merge[l], w_branch_sg=w_branch_sg[l],
            w_branch_lru=w_branch_lru[l], w_out=w_out[l], w_ple_gate=w_ple_gate[l],
            w_ple=w_ple[l], lru_wa=lru_wa[l].reshape(d, LRU_BLOCK_DIM),
            lru_wx=lru_wx[l].reshape(d, LRU_BLOCK_DIM))
        sources = [f32_by_name[name] for name in _SOURCE_NAMES]
        sgw_rows = jnp.repeat(sg_w[l][:, :steps, :steps].reshape(SG_HEADS, steps * steps).T,
                              SG_HEAD_DIM, axis=1)
        sgb_rows = jnp.repeat(sg_b[l][:, :steps].T, SG_HEAD_DIM, axis=1)
        (xs, cs, hs, zs), matrices = _sample_call(
            xs, p_sample[l], jnp.swapaxes(state_conv[l], 0, 1), state_lru[l], vectors, sources,
            sgw_rows, sgb_rows)
        conv_s.append(jnp.swapaxes(cs, 0, 1))
        lru_s.append(hs)
        chunk_s.append(zs)
        xp, cp, hp = _prompt_call(xp, p_prompt[l], vectors, matrices, sg_w[l], sg_b[l].T)
        conv_p.append(jnp.swapaxes(cp, 0, 1))
        lru_p.append(hp)
    return (xp, xs, jnp.stack(conv_p), jnp.stack(lru_p),
            jnp.stack(conv_s), jnp.stack(lru_s), jnp.stack(chunk_s))
```

```python
import functools

import jax
import jax.numpy as jnp
from jax import lax
from jax.experimental import pallas as pl
from jax.experimental.pallas import tpu as pltpu

D_MODEL = 1024
PLE_DIM = 256
SG_HEADS = 4
SG_HEAD_DIM = D_MODEL // SG_HEADS
CHUNK = 128
LRU_BLOCKS = 8
LRU_BLOCK_DIM = D_MODEL // LRU_BLOCKS
CONV_WIDTH = 4
LRU_C = 8.0
EPS = 1e-6
SQRT_FLOOR = 1e-30

SUBLANES = 8
LANES = 128
PROMPT_TILE = 256
VMEM_LIMIT_BYTES = 56 * 1024 * 1024
STAGE_COLS = 1024
STAGE_SLOTS = 4
RING_SLOTS = 4

_BF16 = jnp.bfloat16
_F32 = jnp.float32
_U32 = jnp.uint32
ROWS_PER_WORD = 2

_VECTOR_NAMES = ("norm_pre", "sg_norm", "conv_w", "conv_b", "lru_ba", "lru_bx", "lru_lambda",
                 "b_merge", "norm_post", "b_ple_gate")
_MATRIX_SHAPES = dict(
    w_in=(D_MODEL, 5 * D_MODEL), w_merge=(D_MODEL, 2 * D_MODEL),
    w_branch_sg=(D_MODEL, D_MODEL), w_branch_lru=(D_MODEL, D_MODEL), w_out=(D_MODEL, D_MODEL),
    w_ple_gate=(D_MODEL, D_MODEL), w_ple=(PLE_DIM, D_MODEL),
    w_gate=(D_MODEL, 2 * LRU_BLOCK_DIM))
_MATRIX_NAMES = tuple(_MATRIX_SHAPES)

COL_U, COL_V, COL_G_SG, COL_X_LRU, COL_G_LRU = range(5)


def _dot(a, b):
    return jnp.dot(a, b, preferred_element_type=_F32)


def _pack_rows(wb):
    return pltpu.bitcast(wb, _U32)


def _dot_packed(a, words):
    return _dot(a, pltpu.bitcast(words, _BF16))


def _packed_rows(first, rows):
    return slice(first // ROWS_PER_WORD, (first + rows) // ROWS_PER_WORD)


def _rms_scale(x):
    var = jnp.mean(x * x, axis=-1, keepdims=True)
    return x * lax.rsqrt(var + EPS)


def _sigmoid(x):
    return 0.5 * jnp.tanh(0.5 * x) + 0.5


def _silu(x):
    return x * _sigmoid(x)


def _in_proj(hb, w, col):
    return _dot_packed(hb, w["w_in"][:, col * D_MODEL:(col + 1) * D_MODEL])


def _merge_pre(hb, w, half):
    return _dot_packed(hb, w["w_merge"][:, half * D_MODEL:(half + 1) * D_MODEL])


def _head_norm(v, sgn):
    zs = []
    for h in range(SG_HEADS):
        sl = slice(h * SG_HEAD_DIM, (h + 1) * SG_HEAD_DIM)
        zs.append(_rms_scale(v[:, sl]) * sgn[:, sl])
    return jnp.concatenate(zs, axis=-1)


def _lru_gates(xcb, wg_ref):
    gates = []
    for n in range(LRU_BLOCKS):
        sl = slice(n * LRU_BLOCK_DIM, (n + 1) * LRU_BLOCK_DIM)
        wg = wg_ref[_packed_rows(n * LRU_BLOCK_DIM, LRU_BLOCK_DIM), :]
        gates.append(_dot_packed(xcb[:, sl], wg))
    r_pre = jnp.concatenate([g[:, :LRU_BLOCK_DIM] for g in gates], axis=-1)
    i_pre = jnp.concatenate([g[:, LRU_BLOCK_DIM:] for g in gates], axis=-1)
    return r_pre, i_pre


def _lru_coeffs(xc, r_pre, i_pre, w):
    r = _sigmoid(r_pre + w["lru_ba"][...])
    i = _sigmoid(i_pre + w["lru_bx"][...])
    log_a = (-LRU_C * jax.nn.softplus(-w["lru_lambda"][...])) * r
    a = jnp.exp(log_a)
    gap = 1.0 - a * a
    mult = gap * lax.rsqrt(jnp.maximum(gap, SQRT_FLOOR))
    return a, mult * (i * xc)


def _merge_gate(pre, w, half):
    return _sigmoid(pre + w["b_merge"][:, half * D_MODEL:(half + 1) * D_MODEL])


def _post_mix(x, hb, y_sg, y_lru, p, w, mm):
    g_a = _merge_gate(mm["merge_a"](hb), w, 0)
    merged = g_a * mm["branch_sg"](y_sg.astype(_BF16))
    g_b = _merge_gate(mm["merge_b"](hb), w, 1)
    merged = merged + g_b * mm["branch_lru"](y_lru.astype(_BF16))
    o = mm["out"](merged.astype(_BF16))
    x1 = x + _rms_scale(o) * w["norm_post"][...]
    embedded = mm["ple"](p.astype(_BF16))
    gate = _sigmoid(mm["ple_gate"](x1.astype(_BF16)) + w["b_ple_gate"][...])
    return x1 + gate * embedded


def _resident_matmuls(w):
    return dict(
        merge_a=lambda lhs: _merge_pre(lhs, w, 0), merge_b=lambda lhs: _merge_pre(lhs, w, 1),
        branch_sg=lambda lhs: _dot_packed(lhs, w["w_branch_sg"][...]),
        branch_lru=lambda lhs: _dot_packed(lhs, w["w_branch_lru"][...]),
        out=lambda lhs: _dot_packed(lhs, w["w_out"][...]),
        ple_gate=lambda lhs: _dot_packed(lhs, w["w_ple_gate"][...]),
        ple=lambda lhs: _dot_packed(lhs, w["w_ple"][...]))


def _run_layout(tile):
    steps = tile // SUBLANES
    return steps, steps + SUBLANES


def _store_runs(buf, value, steps, pitch):
    for l in range(value.shape[1] // LANES):
        for s in range(SUBLANES):
            buf[l, s * pitch:s * pitch + steps, :] = (
                value[s * steps:(s + 1) * steps, l * LANES:(l + 1) * LANES])


def _load_runs(buf, steps, pitch):
    return jnp.concatenate(
        [jnp.concatenate([buf[l, s * pitch:s * pitch + steps, :] for l in range(buf.shape[0])],
                         axis=1) for s in range(SUBLANES)], axis=0)


def _load_step(buf, j, pitch):
    return jnp.concatenate([buf[l, pl.ds(j, SUBLANES, stride=pitch), :]
                            for l in range(buf.shape[0])], axis=1)


def _store_step(buf, j, pitch, slab):
    for l in range(buf.shape[0]):
        buf[l, pl.ds(j, SUBLANES, stride=pitch), :] = slab[:, l * LANES:(l + 1) * LANES]


def _from_previous_run(slab, first_run_rows):
    first = lax.broadcasted_iota(jnp.int32, slab.shape, 0) == 0
    return jnp.where(first, first_run_rows, pltpu.roll(slab, 1, 0))


def _conv_by_step(xs, conv_state, w):
    steps = len(xs)
    last = SUBLANES - 1
    before = [_from_previous_run(xs[steps - k], conv_state[SUBLANES - k:SUBLANES - k + 1, :])
              for k in range(CONV_WIDTH - 1, 0, -1)]
    for k in range(1, CONV_WIDTH):
        conv_state[SUBLANES - k:SUBLANES - k + 1, :] = xs[steps - k][last:, :]
    ext = before + xs
    out = []
    for j in range(steps):
        acc = w["conv_b"][...] + w["conv_w"][0:1, :] * ext[j]
        for k in range(1, CONV_WIDTH):
            acc = acc + w["conv_w"][k:k + 1, :] * ext[j + k]
        out.append(acc)
    return out


def _scan_by_step(a, b, h_prev):
    steps = a.shape[0] // SUBLANES
    slab = lambda v, j: v[j * SUBLANES:(j + 1) * SUBLANES, :]
    h0, decay = [slab(b, 0)], [slab(a, 0)]
    for j in range(1, steps):
        h0.append(slab(a, j) * h0[-1] + slab(b, j))
        decay.append(slab(a, j) * decay[-1])
    end_h, end_decay = h0[-1], decay[-1]
    row = lax.broadcasted_iota(jnp.int32, end_h.shape, 0)
    shift = 1
    while shift < SUBLANES:
        keep = row >= shift
        end_h = end_h + end_decay * jnp.where(keep, pltpu.roll(end_h, shift, 0), 0.0)
        end_decay = end_decay * jnp.where(keep, pltpu.roll(end_decay, shift, 0), 1.0)
        shift *= 2
    ends = end_h + end_decay * h_prev
    entering = _from_previous_run(ends, h_prev)
    hs = [h0[j] + decay[j] * entering for j in range(steps)]
    return hs, ends[SUBLANES - 1:, :]


def _spatial_mix_chunk(z, c, sgw_ref, sgb_ref):
    tri = (lax.broadcasted_iota(jnp.int32, (CHUNK, CHUNK), 0)
           >= lax.broadcasted_iota(jnp.int32, (CHUNK, CHUNK), 1))
    s_heads = []
    for h in range(SG_HEADS):
        wm = jnp.where(tri, sgw_ref[h], 0.0).astype(_BF16)
        zc = z[c * CHUNK:(c + 1) * CHUNK, h * SG_HEAD_DIM:(h + 1) * SG_HEAD_DIM]
        s_heads.append(_dot(wm, zc) + sgb_ref[:, h:h + 1])
    return jnp.concatenate(s_heads, axis=-1)


def _prompt_kernel(*refs):
    n_w = len(_VECTOR_NAMES) + len(_MATRIX_NAMES)
    x_ref, p_ref = refs[0], refs[1]
    w = dict(zip(_VECTOR_NAMES + _MATRIX_NAMES, refs[2:2 + n_w]))
    sgw_ref, sgb_ref = refs[2 + n_w], refs[3 + n_w]
    y_ref, conv_out_ref, lru_out_ref = refs[4 + n_w:7 + n_w]
    runs_buf, conv_state, h_state = refs[7 + n_w:]

    t = pl.program_id(1)
    tile = x_ref.shape[0]

    @pl.when(t == 0)
    def _():
        conv_state[...] = jnp.zeros_like(conv_state)
        h_state[...] = jnp.zeros_like(h_state)

    x = x_ref[...]
    hb = (_rms_scale(x) * w["norm_pre"][...]).astype(_BF16)

    steps, pitch = _run_layout(tile)
    _store_runs(runs_buf, _in_proj(hb, w, COL_X_LRU), steps, pitch)

    v = _in_proj(hb, w, COL_V)
    z = _head_norm(v, w["sg_norm"][...]).astype(_BF16)

    xs = [_load_step(runs_buf, j, pitch) for j in range(steps)]
    xc = jnp.concatenate(_conv_by_step(xs, conv_state, w), axis=0)
    xcb = xc.astype(_BF16)
    gates, wide, s_chunks = [], [], []
    for pair, col in enumerate((COL_U, COL_G_SG, COL_G_LRU, None)):
        for n in (2 * pair, 2 * pair + 1):
            sl = slice(n * LRU_BLOCK_DIM, (n + 1) * LRU_BLOCK_DIM)
            wg = w["w_gate"][_packed_rows(n * LRU_BLOCK_DIM, LRU_BLOCK_DIM), :]
            gates.append(_dot_packed(xcb[:, sl], wg))
        if col is not None:
            wide.append(_in_proj(hb, w, col))
        if pair < tile // CHUNK:
            s_chunks.append(_spatial_mix_chunk(z, pair, sgw_ref, sgb_ref))
    u, g_sg, g_lru = wide
    r_pre = jnp.concatenate([g[:, :LRU_BLOCK_DIM] for g in gates], axis=-1)
    i_pre = jnp.concatenate([g[:, LRU_BLOCK_DIM:] for g in gates], axis=-1)

    y_sg = u * jnp.concatenate(s_chunks, axis=0) * _silu(g_sg)

    a, bterm = _lru_coeffs(xc, r_pre, i_pre, w)
    hs, h_last = _scan_by_step(a, bterm, h_state[...])
    h_state[...] = h_last
    for j in range(steps):
        _store_step(runs_buf, j, pitch, hs[j])
    y_lru = _load_runs(runs_buf, steps, pitch) * _silu(g_lru)

    y_ref[...] = _post_mix(x, hb, y_sg, y_lru, p_ref[...], w, _resident_matmuls(w))

    @pl.when(t == pl.num_programs(1) - 1)
    def _():
        seq = pl.ds(pl.program_id(0), 1)
        for k in range(CONV_WIDTH - 1):
            row = SUBLANES - (CONV_WIDTH - 1) + k
            conv_out_ref[k, seq, :] = conv_state[row:row + 1, :]
        lru_out_ref[seq, :] = h_last


def _resident(shape):
    return pl.BlockSpec(shape, lambda *_: (0,) * len(shape), pipeline_mode=pl.Buffered(1))


def _prompt_call(x, p, vectors, matrices, sgw, sgb_t):
    batch, seq, d = x.shape
    tile = PROMPT_TILE
    grid = (batch, seq // tile)
    in_specs = [
        pl.BlockSpec((None, tile, d), lambda b, t: (b, t, 0)),
        pl.BlockSpec((None, tile, PLE_DIM), lambda b, t: (b, t, 0)),
    ]
    in_specs += [_resident(v.shape) for v in vectors + matrices]
    in_specs += [_resident(sgw.shape), _resident(sgb_t.shape)]
    out_shape = (
        jax.ShapeDtypeStruct((batch, seq, d), _F32),
        jax.ShapeDtypeStruct((CONV_WIDTH - 1, batch, d), _F32),
        jax.ShapeDtypeStruct((batch, d), _F32),
    )
    out_specs = (
        pl.BlockSpec((None, tile, d), lambda b, t: (b, t, 0)),
        pl.BlockSpec((CONV_WIDTH - 1, batch, d), lambda b, t: (0, 0, 0)),
        pl.BlockSpec((batch, d), lambda b, t: (0, 0)),
    )
    return pl.pallas_call(
        _prompt_kernel,
        grid=grid,
        in_specs=in_specs,
        out_specs=out_specs,
        out_shape=out_shape,
        scratch_shapes=[
            pltpu.VMEM((d // LANES, SUBLANES * _run_layout(tile)[1], LANES), _F32),
            pltpu.VMEM((SUBLANES, d), _F32),
            pltpu.VMEM((1, d), _F32),
        ],
        compiler_params=pltpu.CompilerParams(
            dimension_semantics=("arbitrary", "arbitrary"),
            vmem_limit_bytes=VMEM_LIMIT_BYTES),
        name="prompt_layer",
    )(x, p, *vectors, *matrices, sgw, sgb_t)


_SOURCE_NAMES = ("w_in", "w_merge", "w_branch_sg", "w_branch_lru", "w_out", "w_ple_gate",
                 "w_ple", "lru_wa", "lru_wx")


def _column_chunks(ref, matrix, first, last):
    return [([(ref, c, min(STAGE_COLS, last - c), 0)], matrix, c)
            for c in range(first, last, STAGE_COLS)]


class _WeightStream:
    def __init__(self, tasks, out, stage, ring, sem_in, sem_out):
        self.tasks, self.out, self.stage, self.ring = tasks, out, stage, ring
        self.sem_in, self.sem_out = sem_in, sem_out
        self.taken = 0
        for i in range(min(STAGE_SLOTS - 1, len(tasks))):
            self._start_in(i)

    def _shape(self, i):
        parts = self.tasks[i][0]
        return parts[0][0].shape[0], sum(part[2] for part in parts)

    def _in_copies(self, i):
        slot = i % STAGE_SLOTS
        rows, _ = self._shape(i)
        return [pltpu.make_async_copy(
            ref.at[:, pl.ds(col, cols)],
            self.stage.at[slot, pl.ds(0, rows), pl.ds(at, cols)],
            self.sem_in.at[slot]) for ref, col, cols, at in self.tasks[i][0]]

    def _start_in(self, i):
        for copy in self._in_copies(i):
            copy.start()

    def _out_copy(self, i):
        slot = i % RING_SLOTS
        rows, cols = self._shape(i)
        _, matrix, first = self.tasks[i]
        return pltpu.make_async_copy(
            self.ring.at[slot, pl.ds(0, rows // ROWS_PER_WORD), pl.ds(0, cols)],
            self.out[matrix].at[:, pl.ds(first, cols)],
            self.sem_out.at[slot])

    def take(self):
        i = self.taken
        self.taken += 1
        if i + STAGE_SLOTS - 1 < len(self.tasks):
            self._start_in(i + STAGE_SLOTS - 1)
        for copy in self._in_copies(i):
            copy.wait()
        if i >= RING_SLOTS:
            self._out_copy(i - RING_SLOTS).wait()
        rows, cols = self._shape(i)
        staged = self.stage[i % STAGE_SLOTS, 0:rows, 0:cols]
        words = rows // ROWS_PER_WORD
        self.ring[i % RING_SLOTS, 0:words, 0:cols] = _pack_rows(staged.astype(_BF16))
        self._out_copy(i).start()
        return self.ring.at[i % RING_SLOTS, pl.ds(0, words), pl.ds(0, cols)]

    def matmul(self, lhs, chunks):
        return jnp.concatenate([_dot_packed(lhs, self.take()[...]) for _ in range(chunks)],
                               axis=-1)

    def finish(self):
        assert self.taken == len(self.tasks)
        for i in range(max(0, len(self.tasks) - RING_SLOTS), len(self.tasks)):
            self._out_copy(i).wait()


def _sample_tasks(src):
    d = D_MODEL
    half = lambda name, h: _column_chunks(src[name], name, h * d, (h + 1) * d)
    gate = [([(src["lru_wa"], 0, LRU_BLOCK_DIM, 0),
              (src["lru_wx"], 0, LRU_BLOCK_DIM, LRU_BLOCK_DIM)], "w_gate", 0)]
    return (_column_chunks(src["w_in"], "w_in", 0, 5 * d) + gate
            + half("w_merge", 0) + half("w_branch_sg", 0) + half("w_merge", 1)
            + half("w_branch_lru", 0) + half("w_out", 0) + half("w_ple", 0)
            + half("w_ple_gate", 0))


def _sample_kernel(*refs, steps):
    n_v, n_s, n_m = len(_VECTOR_NAMES), len(_SOURCE_NAMES), len(_MATRIX_NAMES)
    x_ref, p_ref, conv_in_ref, h0_ref = refs[:4]
    w = dict(zip(_VECTOR_NAMES, refs[4:4 + n_v]))
    sgw_ref, sgb_ref = refs[4 + n_v], refs[5 + n_v]
    src = dict(zip(_SOURCE_NAMES, refs[6 + n_v:6 + n_v + n_s]))
    outs = refs[6 + n_v + n_s:]
    y_ref, conv_out_ref, lru_out_ref, z_out_ref = outs[:4]
    out_packed = dict(zip(_MATRIX_NAMES, outs[4:4 + n_m]))
    stage, ring, sem_in, sem_out = outs[4 + n_m:]
    d = D_MODEL
    nb = x_ref.shape[0]

    stream = _WeightStream(_sample_tasks(src), out_packed, stage, ring, sem_in, sem_out)
    chunks_per_block = d // STAGE_COLS

    def slab(v, t):
        return v[t * nb:(t + 1) * nb]

    x = jnp.concatenate([x_ref[:, t, :] for t in range(steps)], axis=0)
    p = jnp.concatenate([p_ref[:, t, :] for t in range(steps)], axis=0)
    hb = (_rms_scale(x) * w["norm_pre"][...]).astype(_BF16)
    proj = stream.matmul(hb, 5 * chunks_per_block)
    col = lambda c: proj[:, c * d:(c + 1) * d]

    z = _head_norm(col(COL_V), w["sg_norm"][...])
    for t in range(steps):
        z_out_ref[:, t, :] = slab(z, t)
    s_slabs = []
    for t in range(steps):
        acc = jnp.broadcast_to(sgb_ref[t:t + 1, :], (nb, d))
        for u in range(t + 1):
            acc = acc + sgw_ref[t * steps + u:t * steps + u + 1, :] * slab(z, u)
        s_slabs.append(acc)
    s = jnp.concatenate(s_slabs, axis=0)
    y_sg = col(COL_U) * s * _silu(col(COL_G_SG))

    x_lru = col(COL_X_LRU)
    hist = [conv_in_ref[k] for k in range(CONV_WIDTH - 1)]
    hist += [slab(x_lru, t) for t in range(steps)]
    for k in range(CONV_WIDTH - 1):
        conv_out_ref[k] = hist[steps + k]
    xc_slabs = []
    for t in range(steps):
        acc = w["conv_b"][...] + w["conv_w"][0:1, :] * hist[t]
        for k in range(1, CONV_WIDTH):
            acc = acc + w["conv_w"][k:k + 1, :] * hist[t + k]
        xc_slabs.append(acc)
    xc = jnp.concatenate(xc_slabs, axis=0)
    r_pre, i_pre = _lru_gates(xc.astype(_BF16), stream.take())
    a, bterm = _lru_coeffs(xc, r_pre, i_pre, w)
    h = h0_ref[...]
    h_slabs = []
    for t in range(steps):
        h = slab(a, t) * h + slab(bterm, t)
        h_slabs.append(h)
    lru_out_ref[...] = h
    y_lru = jnp.concatenate(h_slabs, axis=0) * _silu(col(COL_G_LRU))

    block = lambda lhs: stream.matmul(lhs, chunks_per_block)
    mm = {name: block for name in
          ("merge_a", "branch_sg", "merge_b", "branch_lru", "out", "ple", "ple_gate")}
    y = _post_mix(x, hb, y_sg, y_lru, p, w, mm)
    for t in range(steps):
        y_ref[:, t, :] = slab(y, t)

    stream.finish()


def _sample_call(x, p, conv_in, h0, vectors, sources, sgw_rows, sgb_rows):
    nb, steps, d = x.shape
    vmem = pl.BlockSpec(memory_space=pltpu.VMEM)
    hbm = pl.BlockSpec(memory_space=pl.ANY)
    matrix_shapes = [_MATRIX_SHAPES[m] for m in _MATRIX_NAMES]
    stage_rows = max(s.shape[0] for s in sources)
    out_shape = (
        jax.ShapeDtypeStruct((nb, steps, d), _F32),
        jax.ShapeDtypeStruct((CONV_WIDTH - 1, nb, d), _F32),
        jax.ShapeDtypeStruct((nb, d), _F32),
        jax.ShapeDtypeStruct((nb, steps, d), _F32),
    ) + tuple(jax.ShapeDtypeStruct((rows // ROWS_PER_WORD, cols), _U32)
              for rows, cols in matrix_shapes)
    outs = pl.pallas_call(
        functools.partial(_sample_kernel, steps=steps),
        in_specs=[vmem] * (6 + len(vectors)) + [hbm] * len(sources),
        out_specs=(vmem,) * 4 + (hbm,) * len(matrix_shapes),
        out_shape=out_shape,
        scratch_shapes=[
            pltpu.VMEM((STAGE_SLOTS, stage_rows, STAGE_COLS), _F32),
            pltpu.VMEM((RING_SLOTS, stage_rows // ROWS_PER_WORD, STAGE_COLS), _U32),
            pltpu.SemaphoreType.DMA((STAGE_SLOTS,)),
            pltpu.SemaphoreType.DMA((RING_SLOTS,)),
        ],
        compiler_params=pltpu.CompilerParams(vmem_limit_bytes=VMEM_LIMIT_BYTES),
        name="sample_layer",
    )(x, p, conv_in, h0, *vectors, sgw_rows, sgb_rows, *sources)
    return outs[:4], list(outs[4:])


def kernel(x_prompt, x_sample, p_prompt, p_sample, state_conv, state_lru, norm_pre, w_in, sg_norm,
           sg_w, sg_b, conv_w, conv_b, lru_wa, lru_ba, lru_wx, lru_bx, lru_lambda, w_branch_sg,
           w_branch_lru, w_merge, b_merge, w_out, norm_post, w_ple, w_ple_gate, b_ple_gate):
    depth = norm_pre.shape[0]
    nb, steps, d = x_sample.shape
    xp, xs = x_prompt, x_sample
    conv_p, lru_p, conv_s, lru_s, chunk_s = [], [], [], [], []
    for l in range(depth):
        row = lambda v: v[l].reshape(1, -1)
        by_name = dict(
            norm_pre=row(norm_pre), sg_norm=row(sg_norm), conv_w=conv_w[l], conv_b=row(conv_b),
            lru_ba=row(lru_ba), lru_bx=row(lru_bx), lru_lambda=row(lru_lambda),
            b_merge=row(b_merge), norm_post=row(norm_post), b_ple_gate=row(b_ple_gate))
        vectors = [by_name[n] for n in _VECTOR_NAMES]
        f32_by_name = dict(
            w_in=w_in[l], w_merge=w_merge[l], w_branch_sg=w_branch_sg[l],
            w_branch_lru=w_branch_lru[l], w_out=w_out[l], w_ple_gate=w_ple_gate[l],
            w_ple=w_ple[l], lru_wa=lru_wa[l].reshape(d, LRU_BLOCK_DIM),
            lru_wx=lru_wx[l].reshape(d, LRU_BLOCK_DIM))
        sources = [f32_by_name[name] for name in _SOURCE_NAMES]
        sgw_rows = jnp.repeat(sg_w[l][:, :steps, :steps].reshape(SG_HEADS, steps * steps).T,
                              SG_HEAD_DIM, axis=1)
        sgb_rows = jnp.repeat(sg_b[l][:, :steps].T, SG_HEAD_DIM, axis=1)
        (xs, cs, hs, zs), matrices = _sample_call(
            xs, p_sample[l], jnp.swapaxes(state_conv[l], 0, 1), state_lru[l], vectors, sources,
            sgw_rows, sgb_rows)
        conv_s.append(jnp.swapaxes(cs, 0, 1))
        lru_s.append(hs)
        chunk_s.append(zs)
        xp, cp, hp = _prompt_call(xp, p_prompt[l], vectors, matrices, sg_w[l], sg_b[l].T)
        conv_p.append(jnp.swapaxes(cp, 0, 1))
        lru_p.append(hp)
    return (xp, xs, jnp.stack(conv_p), jnp.stack(lru_p),
            jnp.stack(conv_s), jnp.stack(lru_s), jnp.stack(chunk_s))
```

```python
import functools

import jax
import jax.numpy as jnp
from jax import lax
from jax.experimental import pallas as pl
from jax.experimental.pallas import tpu as pltpu

D_MODEL = 1024
PLE_DIM = 256
SG_HEADS = 4
SG_HEAD_DIM = D_MODEL // SG_HEADS
CHUNK = 128
LRU_BLOCKS = 8
LRU_BLOCK_DIM = D_MODEL // LRU_BLOCKS
CONV_WIDTH = 4
LRU_C = 8.0
EPS = 1e-6
SQRT_FLOOR = 1e-30

SUBLANES = 8
LANES = 128
PROMPT_TILE = 256
PROMPT_LANES = 2
VMEM_LIMIT_BYTES = 56 * 1024 * 1024
STAGE_COLS = 1024
STAGE_SLOTS = 4
RING_SLOTS = 4

_BF16 = jnp.bfloat16
_F32 = jnp.float32
_U32 = jnp.uint32
ROWS_PER_WORD = 2

_VECTOR_NAMES = ("norm_pre", "sg_norm", "conv_w", "conv_b", "lru_ba", "lru_bx", "lru_lambda",
                 "b_merge", "norm_post", "b_ple_gate")
_MATRIX_SHAPES = dict(
    w_in=(D_MODEL, 5 * D_MODEL), w_merge=(D_MODEL, 2 * D_MODEL),
    w_branch_sg=(D_MODEL, D_MODEL), w_branch_lru=(D_MODEL, D_MODEL), w_out=(D_MODEL, D_MODEL),
    w_ple_gate=(D_MODEL, D_MODEL), w_ple=(PLE_DIM, D_MODEL),
    w_gate=(D_MODEL, 2 * LRU_BLOCK_DIM))
_MATRIX_NAMES = tuple(_MATRIX_SHAPES)

COL_U, COL_V, COL_G_SG, COL_X_LRU, COL_G_LRU = range(5)


def _dot(a, b):
    return jnp.dot(a, b, preferred_element_type=_F32)


def _pack_rows(wb):
    return pltpu.bitcast(wb, _U32)


def _dot_packed(a, words):
    return _dot(a, pltpu.bitcast(words, _BF16))


def _packed_rows(first, rows):
    return slice(first // ROWS_PER_WORD, (first + rows) // ROWS_PER_WORD)


def _rms_scale(x):
    var = jnp.mean(x * x, axis=-1, keepdims=True)
    return x * lax.rsqrt(var + EPS)


def _sigmoid(x):
    return 0.5 * jnp.tanh(0.5 * x) + 0.5


def _silu(x):
    return x * _sigmoid(x)


def _in_proj(hb, w, col):
    return _dot_packed(hb, w["w_in"][:, col * D_MODEL:(col + 1) * D_MODEL])


def _merge_pre(hb, w, half):
    return _dot_packed(hb, w["w_merge"][:, half * D_MODEL:(half + 1) * D_MODEL])


def _head_norm(v, sgn):
    zs = []
    for h in range(SG_HEADS):
        sl = slice(h * SG_HEAD_DIM, (h + 1) * SG_HEAD_DIM)
        zs.append(_rms_scale(v[:, sl]) * sgn[:, sl])
    return jnp.concatenate(zs, axis=-1)


def _lru_gates(xcb, wg_ref):
    gates = []
    for n in range(LRU_BLOCKS):
        sl = slice(n * LRU_BLOCK_DIM, (n + 1) * LRU_BLOCK_DIM)
        wg = wg_ref[_packed_rows(n * LRU_BLOCK_DIM, LRU_BLOCK_DIM), :]
        gates.append(_dot_packed(xcb[:, sl], wg))
    r_pre = jnp.concatenate([g[:, :LRU_BLOCK_DIM] for g in gates], axis=-1)
    i_pre = jnp.concatenate([g[:, LRU_BLOCK_DIM:] for g in gates], axis=-1)
    return r_pre, i_pre


def _lru_coeffs(xc, r_pre, i_pre, w):
    r = _sigmoid(r_pre + w["lru_ba"][...])
    i = _sigmoid(i_pre + w["lru_bx"][...])
    log_a = (-LRU_C * jax.nn.softplus(-w["lru_lambda"][...])) * r
    a = jnp.exp(log_a)
    gap = 1.0 - a * a
    mult = gap * lax.rsqrt(jnp.maximum(gap, SQRT_FLOOR))
    return a, mult * (i * xc)


def _merge_gate(pre, w, half):
    return _sigmoid(pre + w["b_merge"][:, half * D_MODEL:(half + 1) * D_MODEL])


def _post_mix(x, hb, y_sg, y_lru, p, w, mm):
    g_a = _merge_gate(mm["merge_a"](hb), w, 0)
    merged = g_a * mm["branch_sg"](y_sg.astype(_BF16))
    g_b = _merge_gate(mm["merge_b"](hb), w, 1)
    merged = merged + g_b * mm["branch_lru"](y_lru.astype(_BF16))
    o = mm["out"](merged.astype(_BF16))
    x1 = x + _rms_scale(o) * w["norm_post"][...]
    embedded = mm["ple"](p.astype(_BF16))
    gate = _sigmoid(mm["ple_gate"](x1.astype(_BF16)) + w["b_ple_gate"][...])
    return x1 + gate * embedded


def _resident_matmuls(w):
    return dict(
        merge_a=lambda lhs: _merge_pre(lhs, w, 0), merge_b=lambda lhs: _merge_pre(lhs, w, 1),
        branch_sg=lambda lhs: _dot_packed(lhs, w["w_branch_sg"][...]),
        branch_lru=lambda lhs: _dot_packed(lhs, w["w_branch_lru"][...]),
        out=lambda lhs: _dot_packed(lhs, w["w_out"][...]),
        ple_gate=lambda lhs: _dot_packed(lhs, w["w_ple_gate"][...]),
        ple=lambda lhs: _dot_packed(lhs, w["w_ple"][...]))


def _run_layout(tile):
    steps = tile // SUBLANES
    return steps, steps + SUBLANES


def _store_runs(buf, value, steps, pitch):
    for l in range(value.shape[1] // LANES):
        for s in range(SUBLANES):
            buf[l, s * pitch:s * pitch + steps, :] = (
                value[s * steps:(s + 1) * steps, l * LANES:(l + 1) * LANES])


def _load_runs(buf, steps, pitch):
    return jnp.concatenate(
        [jnp.concatenate([buf[l, s * pitch:s * pitch + steps, :] for l in range(buf.shape[0])],
                         axis=1) for s in range(SUBLANES)], axis=0)


def _load_step(buf, j, pitch):
    return jnp.concatenate([buf[l, pl.ds(j, SUBLANES, stride=pitch), :]
                            for l in range(buf.shape[0])], axis=1)


def _store_step(buf, j, pitch, slab):
    for l in range(buf.shape[0]):
        buf[l, pl.ds(j, SUBLANES, stride=pitch), :] = slab[:, l * LANES:(l + 1) * LANES]


def _from_previous_run(slab, first_run_rows):
    first = lax.broadcasted_iota(jnp.int32, slab.shape, 0) == 0
    return jnp.where(first, first_run_rows, pltpu.roll(slab, 1, 0))


def _conv_by_step(xs, conv_state, w):
    steps = len(xs)
    last = SUBLANES - 1
    before = [_from_previous_run(xs[steps - k], conv_state[SUBLANES - k:SUBLANES - k + 1, :])
              for k in range(CONV_WIDTH - 1, 0, -1)]
    for k in range(1, CONV_WIDTH):
        conv_state[SUBLANES - k:SUBLANES - k + 1, :] = xs[steps - k][last:, :]
    ext = before + xs
    out = []
    for j in range(steps):
        acc = w["conv_b"][...] + w["conv_w"][0:1, :] * ext[j]
        for k in range(1, CONV_WIDTH):
            acc = acc + w["conv_w"][k:k + 1, :] * ext[j + k]
        out.append(acc)
    return out


def _scan_by_step(a, b, h_prev):
    steps = a.shape[0] // SUBLANES
    slab = lambda v, j: v[j * SUBLANES:(j + 1) * SUBLANES, :]
    h0, decay = [slab(b, 0)], [slab(a, 0)]
    for j in range(1, steps):
        h0.append(slab(a, j) * h0[-1] + slab(b, j))
        decay.append(slab(a, j) * decay[-1])
    end_h, end_decay = h0[-1], decay[-1]
    row = lax.broadcasted_iota(jnp.int32, end_h.shape, 0)
    shift = 1
    while shift < SUBLANES:
        keep = row >= shift
        end_h = end_h + end_decay * jnp.where(keep, pltpu.roll(end_h, shift, 0), 0.0)
        end_decay = end_decay * jnp.where(keep, pltpu.roll(end_decay, shift, 0), 1.0)
        shift *= 2
    ends = end_h + end_decay * h_prev
    entering = _from_previous_run(ends, h_prev)
    hs = [h0[j] + decay[j] * entering for j in range(steps)]
    return hs, ends[SUBLANES - 1:, :]


def _spatial_mix_chunk(z, c, sgw_ref, sgb_ref):
    tri = (lax.broadcasted_iota(jnp.int32, (CHUNK, CHUNK), 0)
           >= lax.broadcasted_iota(jnp.int32, (CHUNK, CHUNK), 1))
    s_heads = []
    for h in range(SG_HEADS):
        wm = jnp.where(tri, sgw_ref[h], 0.0).astype(_BF16)
        zc = z[c * CHUNK:(c + 1) * CHUNK, h * SG_HEAD_DIM:(h + 1) * SG_HEAD_DIM]
        s_heads.append(_dot(wm, zc) + sgb_ref[:, h:h + 1])
    return jnp.concatenate(s_heads, axis=-1)


_MIDDLE_MATMUL_ORDER = (
    ("gate", 0), ("gate", 1), ("wide", COL_U), ("spatial", 0),
    ("gate", 2), ("gate", 3), ("wide", COL_G_SG), ("spatial", 1),
    ("gate", 4), ("gate", 5), ("wide", COL_G_LRU), ("gate", 6), ("gate", 7))


def _tile_layer(x, p, runs_buf, conv_state, h_state, w, sgw_ref, sgb_ref):
    tile = x.shape[0]
    hb = (_rms_scale(x) * w["norm_pre"][...]).astype(_BF16)

    steps, pitch = _run_layout(tile)
    _store_runs(runs_buf, _in_proj(hb, w, COL_X_LRU), steps, pitch)

    v = _in_proj(hb, w, COL_V)
    z = _head_norm(v, w["sg_norm"][...]).astype(_BF16)

    xs = [_load_step(runs_buf, j, pitch) for j in range(steps)]
    xc = jnp.concatenate(_conv_by_step(xs, conv_state, w), axis=0)
    xcb = xc.astype(_BF16)
    gates, wide, s_chunks = {}, {}, {}
    for kind, k in _MIDDLE_MATMUL_ORDER:
        if kind == "wide":
            wide[k] = _in_proj(hb, w, k)
        elif kind == "gate":
            sl = slice(k * LRU_BLOCK_DIM, (k + 1) * LRU_BLOCK_DIM)
            wg = w["w_gate"][_packed_rows(k * LRU_BLOCK_DIM, LRU_BLOCK_DIM), :]
            gates[k] = _dot_packed(xcb[:, sl], wg)
        else:
            s_chunks[k] = _spatial_mix_chunk(z, k, sgw_ref, sgb_ref)
    u, g_sg, g_lru = wide[COL_U], wide[COL_G_SG], wide[COL_G_LRU]
    r_pre = jnp.concatenate([gates[n][:, :LRU_BLOCK_DIM] for n in range(LRU_BLOCKS)], axis=-1)
    i_pre = jnp.concatenate([gates[n][:, LRU_BLOCK_DIM:] for n in range(LRU_BLOCKS)], axis=-1)
    s = jnp.concatenate([s_chunks[c] for c in range(tile // CHUNK)], axis=0)

    y_sg = u * s * _silu(g_sg)

    a, bterm = _lru_coeffs(xc, r_pre, i_pre, w)
    hs, h_last = _scan_by_step(a, bterm, h_state[...])
    h_state[...] = h_last
    for j in range(steps):
        _store_step(runs_buf, j, pitch, hs[j])
    y_lru = _load_runs(runs_buf, steps, pitch) * _silu(g_lru)

    return _post_mix(x, hb, y_sg, y_lru, p, w, _resident_matmuls(w)), h_last


def _prompt_kernel(*refs):
    n_w = len(_VECTOR_NAMES) + len(_MATRIX_NAMES)
    x_ref, p_ref = refs[0], refs[1]
    w = dict(zip(_VECTOR_NAMES + _MATRIX_NAMES, refs[2:2 + n_w]))
    sgw_ref, sgb_ref = refs[2 + n_w], refs[3 + n_w]
    y_ref, conv_out_ref, lru_out_ref = refs[4 + n_w:7 + n_w]
    runs_buf, conv_state, h_state = refs[7 + n_w:]

    t = pl.program_id(1)
    lanes = x_ref.shape[0]

    @pl.when(t == 0)
    def _():
        conv_state[...] = jnp.zeros_like(conv_state)
        h_state[...] = jnp.zeros_like(h_state)

    h_last = []
    for i in range(lanes):
        y, h = _tile_layer(x_ref[i], p_ref[i], runs_buf.at[i], conv_state.at[i], h_state.at[i],
                           w, sgw_ref, sgb_ref)
        y_ref[i] = y
        h_last.append(h)

    @pl.when(t == pl.num_programs(1) - 1)
    def _():
        for i in range(lanes):
            seq = pl.ds(i * pl.num_programs(0) + pl.program_id(0), 1)
            for k in range(CONV_WIDTH - 1):
                row = SUBLANES - (CONV_WIDTH - 1) + k
                conv_out_ref[k, seq, :] = conv_state[i, row:row + 1, :]
            lru_out_ref[seq, :] = h_last[i]


def _resident(shape):
    return pl.BlockSpec(shape, lambda *_: (0,) * len(shape), pipeline_mode=pl.Buffered(1))


def _prompt_call(x, p, vectors, matrices, sgw, sgb_t):
    batch, seq, d = x.shape
    tile = PROMPT_TILE
    lanes = PROMPT_LANES
    grid = (batch // lanes, seq // tile)
    x4 = x.reshape(lanes, batch // lanes, seq, d)
    p4 = p.reshape(lanes, batch // lanes, seq, PLE_DIM)
    in_specs = [
        pl.BlockSpec((lanes, None, tile, d), lambda b, t: (0, b, t, 0)),
        pl.BlockSpec((lanes, None, tile, PLE_DIM), lambda b, t: (0, b, t, 0)),
    ]
    in_specs += [_resident(v.shape) for v in vectors + matrices]
    in_specs += [_resident(sgw.shape), _resident(sgb_t.shape)]
    out_shape = (
        jax.ShapeDtypeStruct((lanes, batch // lanes, seq, d), _F32),
        jax.ShapeDtypeStruct((CONV_WIDTH - 1, batch, d), _F32),
        jax.ShapeDtypeStruct((batch, d), _F32),
    )
    out_specs = (
        pl.BlockSpec((lanes, None, tile, d), lambda b, t: (0, b, t, 0)),
        pl.BlockSpec((CONV_WIDTH - 1, batch, d), lambda b, t: (0, 0, 0)),
        pl.BlockSpec((batch, d), lambda b, t: (0, 0)),
    )
    y, conv_rows, state = pl.pallas_call(
        _prompt_kernel,
        grid=grid,
        in_specs=in_specs,
        out_specs=out_specs,
        out_shape=out_shape,
        scratch_shapes=[
            pltpu.VMEM((lanes, d // LANES, SUBLANES * _run_layout(tile)[1], LANES), _F32),
            pltpu.VMEM((lanes, SUBLANES, d), _F32),
            pltpu.VMEM((lanes, 1, d), _F32),
        ],
        compiler_params=pltpu.CompilerParams(
            dimension_semantics=("arbitrary", "arbitrary"),
            vmem_limit_bytes=VMEM_LIMIT_BYTES),
        name="prompt_layer",
    )(x4, p4, *vectors, *matrices, sgw, sgb_t)
    return y.reshape(batch, seq, d), conv_rows, state


_SOURCE_NAMES = ("w_in", "w_merge", "w_branch_sg", "w_branch_lru", "w_out", "w_ple_gate",
                 "w_ple", "lru_wa", "lru_wx")


def _column_chunks(ref, matrix, first, last):
    return [([(ref, c, min(STAGE_COLS, last - c), 0)], matrix, c)
            for c in range(first, last, STAGE_COLS)]


class _WeightStream:
    def __init__(self, tasks, out, stage, ring, sem_in, sem_out):
        self.tasks, self.out, self.stage, self.ring = tasks, out, stage, ring
        self.sem_in, self.sem_out = sem_in, sem_out
        self.taken = 0
        for i in range(min(STAGE_SLOTS - 1, len(tasks))):
            self._start_in(i)

    def _shape(self, i):
        parts = self.tasks[i][0]
        return parts[0][0].shape[0], sum(part[2] for part in parts)

    def _in_copies(self, i):
        slot = i % STAGE_SLOTS
        rows, _ = self._shape(i)
        return [pltpu.make_async_copy(
            ref.at[:, pl.ds(col, cols)],
            self.stage.at[slot, pl.ds(0, rows), pl.ds(at, cols)],
            self.sem_in.at[slot]) for ref, col, cols, at in self.tasks[i][0]]

    def _start_in(self, i):
        for copy in self._in_copies(i):
            copy.start()

    def _out_copy(self, i):
        slot = i % RING_SLOTS
        rows, cols = self._shape(i)
        _, matrix, first = self.tasks[i]
        return pltpu.make_async_copy(
            self.ring.at[slot, pl.ds(0, rows // ROWS_PER_WORD), pl.ds(0, cols)],
            self.out[matrix].at[:, pl.ds(first, cols)],
            self.sem_out.at[slot])

    def take(self):
        i = self.taken
        self.taken += 1
        if i + STAGE_SLOTS - 1 < len(self.tasks):
            self._start_in(i + STAGE_SLOTS - 1)
        for copy in self._in_copies(i):
            copy.wait()
        if i >= RING_SLOTS:
            self._out_copy(i - RING_SLOTS).wait()
        rows, cols = self._shape(i)
        staged = self.stage[i % STAGE_SLOTS, 0:rows, 0:cols]
        words = rows // ROWS_PER_WORD
        self.ring[i % RING_SLOTS, 0:words, 0:cols] = _pack_rows(staged.astype(_BF16))
        self._out_copy(i).start()
        return self.ring.at[i % RING_SLOTS, pl.ds(0, words), pl.ds(0, cols)]

    def matmul(self, lhs, chunks):
        return jnp.concatenate([_dot_packed(lhs, self.take()[...]) for _ in range(chunks)],
                               axis=-1)

    def finish(self):
        assert self.taken == len(self.tasks)
        for i in range(max(0, len(self.tasks) - RING_SLOTS), len(self.tasks)):
            self._out_copy(i).wait()


def _sample_tasks(src):
    d = D_MODEL
    half = lambda name, h: _column_chunks(src[name], name, h * d, (h + 1) * d)
    gate = [([(src["lru_wa"], 0, LRU_BLOCK_DIM, 0),
              (src["lru_wx"], 0, LRU_BLOCK_DIM, LRU_BLOCK_DIM)], "w_gate", 0)]
    return (_column_chunks(src["w_in"], "w_in", 0, 5 * d) + gate
            + half("w_merge", 0) + half("w_branch_sg", 0) + half("w_merge", 1)
            + half("w_branch_lru", 0) + half("w_out", 0) + half("w_ple", 0)
            + half("w_ple_gate", 0))


def _sample_kernel(*refs, steps):
    n_v, n_s, n_m = len(_VECTOR_NAMES), len(_SOURCE_NAMES), len(_MATRIX_NAMES)
    x_ref, p_ref, conv_in_ref, h0_ref = refs[:4]
    w = dict(zip(_VECTOR_NAMES, refs[4:4 + n_v]))
    sgw_ref, sgb_ref = refs[4 + n_v], refs[5 + n_v]
    src = dict(zip(_SOURCE_NAMES, refs[6 + n_v:6 + n_v + n_s]))
    outs = refs[6 + n_v + n_s:]
    y_ref, conv_out_ref, lru_out_ref, z_out_ref = outs[:4]
    out_packed = dict(zip(_MATRIX_NAMES, outs[4:4 + n_m]))
    stage, ring, sem_in, sem_out = outs[4 + n_m:]
    d = D_MODEL
    nb = x_ref.shape[0]

    stream = _WeightStream(_sample_tasks(src), out_packed, stage, ring, sem_in, sem_out)
    chunks_per_block = d // STAGE_COLS

    def slab(v, t):
        return v[t * nb:(t + 1) * nb]

    x = jnp.concatenate([x_ref[:, t, :] for t in range(steps)], axis=0)
    p = jnp.concatenate([p_ref[:, t, :] for t in range(steps)], axis=0)
    hb = (_rms_scale(x) * w["norm_pre"][...]).astype(_BF16)
    proj = stream.matmul(hb, 5 * chunks_per_block)
    col = lambda c: proj[:, c * d:(c + 1) * d]

    z = _head_norm(col(COL_V), w["sg_norm"][...])
    for t in range(steps):
        z_out_ref[:, t, :] = slab(z, t)
    s_slabs = []
    for t in range(steps):
        acc = jnp.broadcast_to(sgb_ref[t:t + 1, :], (nb, d))
        for u in range(t + 1):
            acc = acc + sgw_ref[t * steps + u:t * steps + u + 1, :] * slab(z, u)
        s_slabs.append(acc)
    s = jnp.concatenate(s_slabs, axis=0)
    y_sg = col(COL_U) * s * _silu(col(COL_G_SG))

    x_lru = col(COL_X_LRU)
    hist = [conv_in_ref[k] for k in range(CONV_WIDTH - 1)]
    hist += [slab(x_lru, t) for t in range(steps)]
    for k in range(CONV_WIDTH - 1):
        conv_out_ref[k] = hist[steps + k]
    xc_slabs = []
    for t in range(steps):
        acc = w["conv_b"][...] + w["conv_w"][0:1, :] * hist[t]
        for k in range(1, CONV_WIDTH):
            acc = acc + w["conv_w"][k:k + 1, :] * hist[t + k]
        xc_slabs.append(acc)
    xc = jnp.concatenate(xc_slabs, axis=0)
    r_pre, i_pre = _lru_gates(xc.astype(_BF16), stream.take())
    a, bterm = _lru_coeffs(xc, r_pre, i_pre, w)
    h = h0_ref[...]
    h_slabs = []
    for t in range(steps):
        h = slab(a, t) * h + slab(bterm, t)
        h_slabs.append(h)
    lru_out_ref[...] = h
    y_lru = jnp.concatenate(h_slabs, axis=0) * _silu(col(COL_G_LRU))

    block = lambda lhs: stream.matmul(lhs, chunks_per_block)
    mm = {name: block for name in
          ("merge_a", "branch_sg", "merge_b", "branch_lru", "out", "ple", "ple_gate")}
    y = _post_mix(x, hb, y_sg, y_lru, p, w, mm)
    for t in range(steps):
        y_ref[:, t, :] = slab(y, t)

    stream.finish()


def _sample_call(x, p, conv_in, h0, vectors, sources, sgw_rows, sgb_rows):
    nb, steps, d = x.shape
    vmem = pl.BlockSpec(memory_space=pltpu.VMEM)
    hbm = pl.BlockSpec(memory_space=pl.ANY)
    matrix_shapes = [_MATRIX_SHAPES[m] for m in _MATRIX_NAMES]
    stage_rows = max(s.shape[0] for s in sources)
    out_shape = (
        jax.ShapeDtypeStruct((nb, steps, d), _F32),
        jax.ShapeDtypeStruct((CONV_WIDTH - 1, nb, d), _F32),
        jax.ShapeDtypeStruct((nb, d), _F32),
        jax.ShapeDtypeStruct((nb, steps, d), _F32),
    ) + tuple(jax.ShapeDtypeStruct((rows // ROWS_PER_WORD, cols), _U32)
              for rows, cols in matrix_shapes)
    outs = pl.pallas_call(
        functools.partial(_sample_kernel, steps=steps),
        in_specs=[vmem] * (6 + len(vectors)) + [hbm] * len(sources),
        out_specs=(vmem,) * 4 + (hbm,) * len(matrix_shapes),
        out_shape=out_shape,
        scratch_shapes=[
            pltpu.VMEM((STAGE_SLOTS, stage_rows, STAGE_COLS), _F32),
            pltpu.VMEM((RING_SLOTS, stage_rows // ROWS_PER_WORD, STAGE_COLS), _U32),
            pltpu.SemaphoreType.DMA((STAGE_SLOTS,)),
            pltpu.SemaphoreType.DMA((RING_SLOTS,)),
        ],
        compiler_params=pltpu.CompilerParams(vmem_limit_bytes=VMEM_LIMIT_BYTES),
        name="sample_layer",
    )(x, p, conv_in, h0, *vectors, sgw_rows, sgb_rows, *sources)
    return outs[:4], list(outs[4:])


def kernel(x_prompt, x_sample, p_prompt, p_sample, state_conv, state_lru, norm_pre, w_in, sg_norm,
           sg_w, sg_b, conv_w, conv_b, lru_wa, lru_ba, lru_wx, lru_bx, lru_lambda, w_branch_sg,
           w_branch_lru, w_merge, b_merge, w_out, norm_post, w_ple, w_ple_gate, b_ple_gate):
    depth = norm_pre.shape[0]
    nb, steps, d = x_sample.shape
    xp, xs = x_prompt, x_sample
    conv_p, lru_p, conv_s, lru_s, chunk_s = [], [], [], [], []
    for l in range(depth):
        row = lambda v: v[l].reshape(1, -1)
        by_name = dict(
            norm_pre=row(norm_pre), sg_norm=row(sg_norm), conv_w=conv_w[l], conv_b=row(conv_b),
            lru_ba=row(lru_ba), lru_bx=row(lru_bx), lru_lambda=row(lru_lambda),
            b_merge=row(b_merge), norm_post=row(norm_post), b_ple_gate=row(b_ple_gate))
        vectors = [by_name[n] for n in _VECTOR_NAMES]
        f32_by_name = dict(
            w_in=w_in[l], w_merge=w_merge[l], w_branch_sg=w_branch_sg[l],
            w_branch_lru=w_branch_lru[l], w_out=w_out[l], w_ple_gate=w_ple_gate[l],
            w_ple=w_ple[l], lru_wa=lru_wa[l].reshape(d, LRU_BLOCK_DIM),
            lru_wx=lru_wx[l].reshape(d, LRU_BLOCK_DIM))
        sources = [f32_by_name[name] for name in _SOURCE_NAMES]
        sgw_rows = jnp.repeat(sg_w[l][:, :steps, :steps].reshape(SG_HEADS, steps * steps).T,
                              SG_HEAD_DIM, axis=1)
        sgb_rows = jnp.repeat(sg_b[l][:, :steps].T, SG_HEAD_DIM, axis=1)
        (xs, cs, hs, zs), matrices = _sample_call(
            xs, p_sample[l], jnp.swapaxes(state_conv[l], 0, 1), state_lru[l], vectors, sources,
            sgw_rows, sgb_rows)
        conv_s.append(jnp.swapaxes(cs, 0, 1))
        lru_s.append(hs)
        chunk_s.append(zs)
        xp, cp, hp = _prompt_call(xp, p_prompt[l], vectors, matrices, sg_w[l], sg_b[l].T)
        conv_p.append(jnp.swapaxes(cp, 0, 1))
        lru_p.append(hp)
    return (xp, xs, jnp.stack(conv_p), jnp.stack(lru_p),
            jnp.stack(conv_s), jnp.stack(lru_s), jnp.stack(chunk_s))
```

```python
import functools

import jax
import jax.numpy as jnp
from jax import lax
from jax.experimental import pallas as pl
from jax.experimental.pallas import tpu as pltpu

D_MODEL = 1024
PLE_DIM = 256
SG_HEADS = 4
SG_HEAD_DIM = D_MODEL // SG_HEADS
CHUNK = 128
LRU_BLOCKS = 8
LRU_BLOCK_DIM = D_MODEL // LRU_BLOCKS
CONV_WIDTH = 4
LRU_C = 8.0
EPS = 1e-6
SQRT_FLOOR = 1e-30

SUBLANES = 8
LANES = 128
PROMPT_TILE = 256
PROMPT_LANES = 4
VMEM_LIMIT_BYTES = 56 * 1024 * 1024
STAGE_COLS = 1024
STAGE_SLOTS = 4
RING_SLOTS = 4

_BF16 = jnp.bfloat16
_F32 = jnp.float32
_U32 = jnp.uint32
ROWS_PER_WORD = 2

_VECTOR_NAMES = ("norm_pre", "sg_norm", "conv_w", "conv_b", "lru_ba", "lru_bx", "lru_lambda",
                 "b_merge", "norm_post", "b_ple_gate")
_MATRIX_SHAPES = dict(
    w_in=(D_MODEL, 5 * D_MODEL), w_merge=(D_MODEL, 2 * D_MODEL),
    w_branch_sg=(D_MODEL, D_MODEL), w_branch_lru=(D_MODEL, D_MODEL), w_out=(D_MODEL, D_MODEL),
    w_ple_gate=(D_MODEL, D_MODEL), w_ple=(PLE_DIM, D_MODEL),
    w_gate=(D_MODEL, 2 * LRU_BLOCK_DIM))
_MATRIX_NAMES = tuple(_MATRIX_SHAPES)

COL_U, COL_V, COL_G_SG, COL_X_LRU, COL_G_LRU = range(5)


def _dot(a, b):
    return jnp.dot(a, b, preferred_element_type=_F32)


def _pack_rows(wb):
    return pltpu.bitcast(wb, _U32)


def _dot_packed(a, words):
    return _dot(a, pltpu.bitcast(words, _BF16))


def _packed_rows(first, rows):
    return slice(first // ROWS_PER_WORD, (first + rows) // ROWS_PER_WORD)


def _rms_scale(x):
    var = jnp.mean(x * x, axis=-1, keepdims=True)
    return x * lax.rsqrt(var + EPS)


def _sigmoid(x):
    return 0.5 * jnp.tanh(0.5 * x) + 0.5


def _silu(x):
    return x * _sigmoid(x)


def _in_proj(hb, w, col):
    return _dot_packed(hb, w["w_in"][:, col * D_MODEL:(col + 1) * D_MODEL])


def _merge_pre(hb, w, half):
    return _dot_packed(hb, w["w_merge"][:, half * D_MODEL:(half + 1) * D_MODEL])


def _head_norm(v, sgn):
    zs = []
    for h in range(SG_HEADS):
        sl = slice(h * SG_HEAD_DIM, (h + 1) * SG_HEAD_DIM)
        zs.append(_rms_scale(v[:, sl]) * sgn[:, sl])
    return jnp.concatenate(zs, axis=-1)


def _lru_gates(xcb, wg_ref):
    gates = []
    for n in range(LRU_BLOCKS):
        sl = slice(n * LRU_BLOCK_DIM, (n + 1) * LRU_BLOCK_DIM)
        wg = wg_ref[_packed_rows(n * LRU_BLOCK_DIM, LRU_BLOCK_DIM), :]
        gates.append(_dot_packed(xcb[:, sl], wg))
    r_pre = jnp.concatenate([g[:, :LRU_BLOCK_DIM] for g in gates], axis=-1)
    i_pre = jnp.concatenate([g[:, LRU_BLOCK_DIM:] for g in gates], axis=-1)
    return r_pre, i_pre


def _lru_coeffs(xc, r_pre, i_pre, w):
    r = _sigmoid(r_pre + w["lru_ba"][...])
    i = _sigmoid(i_pre + w["lru_bx"][...])
    log_a = (-LRU_C * jax.nn.softplus(-w["lru_lambda"][...])) * r
    a = jnp.exp(log_a)
    gap = 1.0 - a * a
    mult = gap * lax.rsqrt(jnp.maximum(gap, SQRT_FLOOR))
    return a, mult * (i * xc)


def _merge_gate(pre, w, half):
    return _sigmoid(pre + w["b_merge"][:, half * D_MODEL:(half + 1) * D_MODEL])


def _post_mix(x, hb, y_sg, y_lru, p, w, mm):
    g_a = _merge_gate(mm["merge_a"](hb), w, 0)
    merged = g_a * mm["branch_sg"](y_sg.astype(_BF16))
    g_b = _merge_gate(mm["merge_b"](hb), w, 1)
    merged = merged + g_b * mm["branch_lru"](y_lru.astype(_BF16))
    o = mm["out"](merged.astype(_BF16))
    x1 = x + _rms_scale(o) * w["norm_post"][...]
    embedded = mm["ple"](p.astype(_BF16))
    gate = _sigmoid(mm["ple_gate"](x1.astype(_BF16)) + w["b_ple_gate"][...])
    return x1 + gate * embedded


def _resident_matmuls(w):
    return dict(
        merge_a=lambda lhs: _merge_pre(lhs, w, 0), merge_b=lambda lhs: _merge_pre(lhs, w, 1),
        branch_sg=lambda lhs: _dot_packed(lhs, w["w_branch_sg"][...]),
        branch_lru=lambda lhs: _dot_packed(lhs, w["w_branch_lru"][...]),
        out=lambda lhs: _dot_packed(lhs, w["w_out"][...]),
        ple_gate=lambda lhs: _dot_packed(lhs, w["w_ple_gate"][...]),
        ple=lambda lhs: _dot_packed(lhs, w["w_ple"][...]))


def _run_layout(tile):
    steps = tile // SUBLANES
    return steps, steps + SUBLANES


def _store_runs(buf, value, steps, pitch):
    for l in range(value.shape[1] // LANES):
        for s in range(SUBLANES):
            buf[l, s * pitch:s * pitch + steps, :] = (
                value[s * steps:(s + 1) * steps, l * LANES:(l + 1) * LANES])


def _load_runs(buf, steps, pitch):
    return jnp.concatenate(
        [jnp.concatenate([buf[l, s * pitch:s * pitch + steps, :] for l in range(buf.shape[0])],
                         axis=1) for s in range(SUBLANES)], axis=0)


def _load_step(buf, j, pitch):
    return jnp.concatenate([buf[l, pl.ds(j, SUBLANES, stride=pitch), :]
                            for l in range(buf.shape[0])], axis=1)


def _store_step(buf, j, pitch, slab):
    for l in range(buf.shape[0]):
        buf[l, pl.ds(j, SUBLANES, stride=pitch), :] = slab[:, l * LANES:(l + 1) * LANES]


def _from_previous_run(slab, first_run_rows):
    first = lax.broadcasted_iota(jnp.int32, slab.shape, 0) == 0
    return jnp.where(first, first_run_rows, pltpu.roll(slab, 1, 0))


def _conv_by_step(xs, conv_state, w):
    steps = len(xs)
    last = SUBLANES - 1
    before = [_from_previous_run(xs[steps - k], conv_state[SUBLANES - k:SUBLANES - k + 1, :])
              for k in range(CONV_WIDTH - 1, 0, -1)]
    for k in range(1, CONV_WIDTH):
        conv_state[SUBLANES - k:SUBLANES - k + 1, :] = xs[steps - k][last:, :]
    ext = before + xs
    out = []
    for j in range(steps):
        acc = w["conv_b"][...] + w["conv_w"][0:1, :] * ext[j]
        for k in range(1, CONV_WIDTH):
            acc = acc + w["conv_w"][k:k + 1, :] * ext[j + k]
        out.append(acc)
    return out


def _scan_by_step(a, b, h_prev):
    steps = a.shape[0] // SUBLANES
    slab = lambda v, j: v[j * SUBLANES:(j + 1) * SUBLANES, :]
    h0, decay = [slab(b, 0)], [slab(a, 0)]
    for j in range(1, steps):
        h0.append(slab(a, j) * h0[-1] + slab(b, j))
        decay.append(slab(a, j) * decay[-1])
    end_h, end_decay = h0[-1], decay[-1]
    row = lax.broadcasted_iota(jnp.int32, end_h.shape, 0)
    shift = 1
    while shift < SUBLANES:
        keep = row >= shift
        end_h = end_h + end_decay * jnp.where(keep, pltpu.roll(end_h, shift, 0), 0.0)
        end_decay = end_decay * jnp.where(keep, pltpu.roll(end_decay, shift, 0), 1.0)
        shift *= 2
    ends = end_h + end_decay * h_prev
    entering = _from_previous_run(ends, h_prev)
    hs = [h0[j] + decay[j] * entering for j in range(steps)]
    return hs, ends[SUBLANES - 1:, :]


def _spatial_mix_chunk(z, c, sgw_ref, sgb_ref):
    tri = (lax.broadcasted_iota(jnp.int32, (CHUNK, CHUNK), 0)
           >= lax.broadcasted_iota(jnp.int32, (CHUNK, CHUNK), 1))
    s_heads = []
    for h in range(SG_HEADS):
        wm = jnp.where(tri, sgw_ref[h], 0.0).astype(_BF16)
        zc = z[c * CHUNK:(c + 1) * CHUNK, h * SG_HEAD_DIM:(h + 1) * SG_HEAD_DIM]
        s_heads.append(_dot(wm, zc) + sgb_ref[:, h:h + 1])
    return jnp.concatenate(s_heads, axis=-1)


_MIDDLE_MATMUL_ORDER = (
    ("gate", 0), ("gate", 1), ("wide", COL_U), ("spatial", 0),
    ("gate", 2), ("gate", 3), ("wide", COL_G_SG), ("spatial", 1),
    ("gate", 4), ("gate", 5), ("wide", COL_G_LRU), ("gate", 6), ("gate", 7))


def _tile_layer(x, p, runs_buf, conv_state, h_state, w, sgw_ref, sgb_ref):
    tile = x.shape[0]
    hb = (_rms_scale(x) * w["norm_pre"][...]).astype(_BF16)

    steps, pitch = _run_layout(tile)
    _store_runs(runs_buf, _in_proj(hb, w, COL_X_LRU), steps, pitch)

    v = _in_proj(hb, w, COL_V)
    z = _head_norm(v, w["sg_norm"][...]).astype(_BF16)

    xs = [_load_step(runs_buf, j, pitch) for j in range(steps)]
    xc = jnp.concatenate(_conv_by_step(xs, conv_state, w), axis=0)
    xcb = xc.astype(_BF16)
    gates, wide, s_chunks = {}, {}, {}
    for kind, k in _MIDDLE_MATMUL_ORDER:
        if kind == "wide":
            wide[k] = _in_proj(hb, w, k)
        elif kind == "gate":
            sl = slice(k * LRU_BLOCK_DIM, (k + 1) * LRU_BLOCK_DIM)
            wg = w["w_gate"][_packed_rows(k * LRU_BLOCK_DIM, LRU_BLOCK_DIM), :]
            gates[k] = _dot_packed(xcb[:, sl], wg)
        else:
            s_chunks[k] = _spatial_mix_chunk(z, k, sgw_ref, sgb_ref)
    u, g_sg, g_lru = wide[COL_U], wide[COL_G_SG], wide[COL_G_LRU]
    r_pre = jnp.concatenate([gates[n][:, :LRU_BLOCK_DIM] for n in range(LRU_BLOCKS)], axis=-1)
    i_pre = jnp.concatenate([gates[n][:, LRU_BLOCK_DIM:] for n in range(LRU_BLOCKS)], axis=-1)
    s = jnp.concatenate([s_chunks[c] for c in range(tile // CHUNK)], axis=0)

    y_sg = u * s * _silu(g_sg)

    a, bterm = _lru_coeffs(xc, r_pre, i_pre, w)
    hs, h_last = _scan_by_step(a, bterm, h_state[...])
    h_state[...] = h_last
    for j in range(steps):
        _store_step(runs_buf, j, pitch, hs[j])
    y_lru = _load_runs(runs_buf, steps, pitch) * _silu(g_lru)

    return _post_mix(x, hb, y_sg, y_lru, p, w, _resident_matmuls(w)), h_last


def _prompt_kernel(*refs):
    n_w = len(_VECTOR_NAMES) + len(_MATRIX_NAMES)
    x_ref, p_ref = refs[0], refs[1]
    w = dict(zip(_VECTOR_NAMES + _MATRIX_NAMES, refs[2:2 + n_w]))
    sgw_ref, sgb_ref = refs[2 + n_w], refs[3 + n_w]
    y_ref, conv_out_ref, lru_out_ref = refs[4 + n_w:7 + n_w]
    runs_buf, conv_state, h_state = refs[7 + n_w:]

    t = pl.program_id(1)
    lanes = x_ref.shape[0]

    @pl.when(t == 0)
    def _():
        conv_state[...] = jnp.zeros_like(conv_state)
        h_state[...] = jnp.zeros_like(h_state)

    h_last = []
    for i in range(lanes):
        y, h = _tile_layer(x_ref[i], p_ref[i], runs_buf.at[i], conv_state.at[i], h_state.at[i],
                           w, sgw_ref, sgb_ref)
        y_ref[i] = y
        h_last.append(h)

    @pl.when(t == pl.num_programs(1) - 1)
    def _():
        for i in range(lanes):
            seq = pl.ds(i * pl.num_programs(0) + pl.program_id(0), 1)
            for k in range(CONV_WIDTH - 1):
                row = SUBLANES - (CONV_WIDTH - 1) + k
                conv_out_ref[k, seq, :] = conv_state[i, row:row + 1, :]
            lru_out_ref[seq, :] = h_last[i]


def _resident(shape):
    return pl.BlockSpec(shape, lambda *_: (0,) * len(shape), pipeline_mode=pl.Buffered(1))


def _prompt_call(x, p, vectors, matrices, sgw, sgb_t):
    batch, seq, d = x.shape
    tile = PROMPT_TILE
    lanes = PROMPT_LANES
    grid = (batch // lanes, seq // tile)
    x4 = x.reshape(lanes, batch // lanes, seq, d)
    p4 = p.reshape(lanes, batch // lanes, seq, PLE_DIM)
    in_specs = [
        pl.BlockSpec((lanes, None, tile, d), lambda b, t: (0, b, t, 0)),
        pl.BlockSpec((lanes, None, tile, PLE_DIM), lambda b, t: (0, b, t, 0)),
    ]
    in_specs += [_resident(v.shape) for v in vectors + matrices]
    in_specs += [_resident(sgw.shape), _resident(sgb_t.shape)]
    out_shape = (
        jax.ShapeDtypeStruct((lanes, batch // lanes, seq, d), _F32),
        jax.ShapeDtypeStruct((CONV_WIDTH - 1, batch, d), _F32),
        jax.ShapeDtypeStruct((batch, d), _F32),
    )
    out_specs = (
        pl.BlockSpec((lanes, None, tile, d), lambda b, t: (0, b, t, 0)),
        pl.BlockSpec((CONV_WIDTH - 1, batch, d), lambda b, t: (0, 0, 0)),
        pl.BlockSpec((batch, d), lambda b, t: (0, 0)),
    )
    y, conv_rows, state = pl.pallas_call(
        _prompt_kernel,
        grid=grid,
        in_specs=in_specs,
        out_specs=out_specs,
        out_shape=out_shape,
        scratch_shapes=[
            pltpu.VMEM((lanes, d // LANES, SUBLANES * _run_layout(tile)[1], LANES), _F32),
            pltpu.VMEM((lanes, SUBLANES, d), _F32),
            pltpu.VMEM((lanes, 1, d), _F32),
        ],
        compiler_params=pltpu.CompilerParams(
            dimension_semantics=("arbitrary", "arbitrary"),
            vmem_limit_bytes=VMEM_LIMIT_BYTES),
        name="prompt_layer",
    )(x4, p4, *vectors, *matrices, sgw, sgb_t)
    return y.reshape(batch, seq, d), conv_rows, state


_SOURCE_NAMES = ("w_in", "w_merge", "w_branch_sg", "w_branch_lru", "w_out", "w_ple_gate",
                 "w_ple", "lru_wa", "lru_wx")


def _column_chunks(ref, matrix, first, last):
    return [([(ref, c, min(STAGE_COLS, last - c), 0)], matrix, c)
            for c in range(first, last, STAGE_COLS)]


class _WeightStream:
    def __init__(self, tasks, out, stage, ring, sem_in, sem_out):
        self.tasks, self.out, self.stage, self.ring = tasks, out, stage, ring
        self.sem_in, self.sem_out = sem_in, sem_out
        self.taken = 0
        for i in range(min(STAGE_SLOTS - 1, len(tasks))):
            self._start_in(i)

    def _shape(self, i):
        parts = self.tasks[i][0]
        return parts[0][0].shape[0], sum(part[2] for part in parts)

    def _in_copies(self, i):
        slot = i % STAGE_SLOTS
        rows, _ = self._shape(i)
        return [pltpu.make_async_copy(
            ref.at[:, pl.ds(col, cols)],
            self.stage.at[slot, pl.ds(0, rows), pl.ds(at, cols)],
            self.sem_in.at[slot]) for ref, col, cols, at in self.tasks[i][0]]

    def _start_in(self, i):
        for copy in self._in_copies(i):
            copy.start()

    def _out_copy(self, i):
        slot = i % RING_SLOTS
        rows, cols = self._shape(i)
        _, matrix, first = self.tasks[i]
        return pltpu.make_async_copy(
            self.ring.at[slot, pl.ds(0, rows // ROWS_PER_WORD), pl.ds(0, cols)],
            self.out[matrix].at[:, pl.ds(first, cols)],
            self.sem_out.at[slot])

    def take(self):
        i = self.taken
        self.taken += 1
        if i + STAGE_SLOTS - 1 < len(self.tasks):
            self._start_in(i + STAGE_SLOTS - 1)
        for copy in self._in_copies(i):
            copy.wait()
        if i >= RING_SLOTS:
            self._out_copy(i - RING_SLOTS).wait()
        rows, cols = self._shape(i)
        staged = self.stage[i % STAGE_SLOTS, 0:rows, 0:cols]
        words = rows // ROWS_PER_WORD
        self.ring[i % RING_SLOTS, 0:words, 0:cols] = _pack_rows(staged.astype(_BF16))
        self._out_copy(i).start()
        return self.ring.at[i % RING_SLOTS, pl.ds(0, words), pl.ds(0, cols)]

    def matmul(self, lhs, chunks):
        return jnp.concatenate([_dot_packed(lhs, self.take()[...]) for _ in range(chunks)],
                               axis=-1)

    def finish(self):
        assert self.taken == len(self.tasks)
        for i in range(max(0, len(self.tasks) - RING_SLOTS), len(self.tasks)):
            self._out_copy(i).wait()


def _sample_tasks(src):
    d = D_MODEL
    half = lambda name, h: _column_chunks(src[name], name, h * d, (h + 1) * d)
    gate = [([(src["lru_wa"], 0, LRU_BLOCK_DIM, 0),
              (src["lru_wx"], 0, LRU_BLOCK_DIM, LRU_BLOCK_DIM)], "w_gate", 0)]
    return (_column_chunks(src["w_in"], "w_in", 0, 5 * d) + gate
            + half("w_merge", 0) + half("w_branch_sg", 0) + half("w_merge", 1)
            + half("w_branch_lru", 0) + half("w_out", 0) + half("w_ple", 0)
            + half("w_ple_gate", 0))


def _sample_kernel(*refs, steps):
    n_v, n_s, n_m = len(_VECTOR_NAMES), len(_SOURCE_NAMES), len(_MATRIX_NAMES)
    x_ref, p_ref, conv_in_ref, h0_ref = refs[:4]
    w = dict(zip(_VECTOR_NAMES, refs[4:4 + n_v]))
    sgw_ref, sgb_ref = refs[4 + n_v], refs[5 + n_v]
    src = dict(zip(_SOURCE_NAMES, refs[6 + n_v:6 + n_v + n_s]))
    outs = refs[6 + n_v + n_s:]
    y_ref, conv_out_ref, lru_out_ref, z_out_ref = outs[:4]
    out_packed = dict(zip(_MATRIX_NAMES, outs[4:4 + n_m]))
    stage, ring, sem_in, sem_out = outs[4 + n_m:]
    d = D_MODEL
    nb = x_ref.shape[0]

    stream = _WeightStream(_sample_tasks(src), out_packed, stage, ring, sem_in, sem_out)
    chunks_per_block = d // STAGE_COLS

    def slab(v, t):
        return v[t * nb:(t + 1) * nb]

    x = jnp.concatenate([x_ref[:, t, :] for t in range(steps)], axis=0)
    p = jnp.concatenate([p_ref[:, t, :] for t in range(steps)], axis=0)
    hb = (_rms_scale(x) * w["norm_pre"][...]).astype(_BF16)
    proj = stream.matmul(hb, 5 * chunks_per_block)
    col = lambda c: proj[:, c * d:(c + 1) * d]

    z = _head_norm(col(COL_V), w["sg_norm"][...])
    for t in range(steps):
        z_out_ref[:, t, :] = slab(z, t)
    s_slabs = []
    for t in range(steps):
        acc = jnp.broadcast_to(sgb_ref[t:t + 1, :], (nb, d))
        for u in range(t + 1):
            acc = acc + sgw_ref[t * steps + u:t * steps + u + 1, :] * slab(z, u)
        s_slabs.append(acc)
    s = jnp.concatenate(s_slabs, axis=0)
    y_sg = col(COL_U) * s * _silu(col(COL_G_SG))

    x_lru = col(COL_X_LRU)
    hist = [conv_in_ref[k] for k in range(CONV_WIDTH - 1)]
    hist += [slab(x_lru, t) for t in range(steps)]
    for k in range(CONV_WIDTH - 1):
        conv_out_ref[k] = hist[steps + k]
    xc_slabs = []
    for t in range(steps):
        acc = w["conv_b"][...] + w["conv_w"][0:1, :] * hist[t]
        for k in range(1, CONV_WIDTH):
            acc = acc + w["conv_w"][k:k + 1, :] * hist[t + k]
        xc_slabs.append(acc)
    xc = jnp.concatenate(xc_slabs, axis=0)
    r_pre, i_pre = _lru_gates(xc.astype(_BF16), stream.take())
    a, bterm = _lru_coeffs(xc, r_pre, i_pre, w)
    h = h0_ref[...]
    h_slabs = []
    for t in range(steps):
        h = slab(a, t) * h + slab(bterm, t)
        h_slabs.append(h)
    lru_out_ref[...] = h
    y_lru = jnp.concatenate(h_slabs, axis=0) * _silu(col(COL_G_LRU))

    block = lambda lhs: stream.matmul(lhs, chunks_per_block)
    mm = {name: block for name in
          ("merge_a", "branch_sg", "merge_b", "branch_lru", "out", "ple", "ple_gate")}
    y = _post_mix(x, hb, y_sg, y_lru, p, w, mm)
    for t in range(steps):
        y_ref[:, t, :] = slab(y, t)

    stream.finish()


def _sample_call(x, p, conv_in, h0, vectors, sources, sgw_rows, sgb_rows):
    nb, steps, d = x.shape
    vmem = pl.BlockSpec(memory_space=pltpu.VMEM)
    hbm = pl.BlockSpec(memory_space=pl.ANY)
    matrix_shapes = [_MATRIX_SHAPES[m] for m in _MATRIX_NAMES]
    stage_rows = max(s.shape[0] for s in sources)
    out_shape = (
        jax.ShapeDtypeStruct((nb, steps, d), _F32),
        jax.ShapeDtypeStruct((CONV_WIDTH - 1, nb, d), _F32),
        jax.ShapeDtypeStruct((nb, d), _F32),
        jax.ShapeDtypeStruct((nb, steps, d), _F32),
    ) + tuple(jax.ShapeDtypeStruct((rows // ROWS_PER_WORD, cols), _U32)
              for rows, cols in matrix_shapes)
    outs = pl.pallas_call(
        functools.partial(_sample_kernel, steps=steps),
        in_specs=[vmem] * (6 + len(vectors)) + [hbm] * len(sources),
        out_specs=(vmem,) * 4 + (hbm,) * len(matrix_shapes),
        out_shape=out_shape,
        scratch_shapes=[
            pltpu.VMEM((STAGE_SLOTS, stage_rows, STAGE_COLS), _F32),
            pltpu.VMEM((RING_SLOTS, stage_rows // ROWS_PER_WORD, STAGE_COLS), _U32),
            pltpu.SemaphoreType.DMA((STAGE_SLOTS,)),
            pltpu.SemaphoreType.DMA((RING_SLOTS,)),
        ],
        compiler_params=pltpu.CompilerParams(vmem_limit_bytes=VMEM_LIMIT_BYTES),
        name="sample_layer",
    )(x, p, conv_in, h0, *vectors, sgw_rows, sgb_rows, *sources)
    return outs[:4], list(outs[4:])


def kernel(x_prompt, x_sample, p_prompt, p_sample, state_conv, state_lru, norm_pre, w_in, sg_norm,
           sg_w, sg_b, conv_w, conv_b, lru_wa, lru_ba, lru_wx, lru_bx, lru_lambda, w_branch_sg,
           w_branch_lru, w_merge, b_merge, w_out, norm_post, w_ple, w_ple_gate, b_ple_gate):
    depth = norm_pre.shape[0]
    nb, steps, d = x_sample.shape
    xp, xs = x_prompt, x_sample
    conv_p, lru_p, conv_s, lru_s, chunk_s = [], [], [], [], []
    for l in range(depth):
        row = lambda v: v[l].reshape(1, -1)
        by_name = dict(
            norm_pre=row(norm_pre), sg_norm=row(sg_norm), conv_w=conv_w[l], conv_b=row(conv_b),
            lru_ba=row(lru_ba), lru_bx=row(lru_bx), lru_lambda=row(lru_lambda),
            b_merge=row(b_merge), norm_post=row(norm_post), b_ple_gate=row(b_ple_gate))
        vectors = [by_name[n] for n in _VECTOR_NAMES]
        f32_by_name = dict(
            w_in=w_in[l], w_merge=w_merge[l], w_branch_sg=w_branch_sg[l],
            w_branch_lru=w_branch_lru[l], w_out=w_out[l], w_ple_gate=w_ple_gate[l],
            w_ple=w_ple[l], lru_wa=lru_wa[l].reshape(d, LRU_BLOCK_DIM),
            lru_wx=lru_wx[l].reshape(d, LRU_BLOCK_DIM))
        sources = [f32_by_name[name] for name in _SOURCE_NAMES]
        sgw_rows = jnp.repeat(sg_w[l][:, :steps, :steps].reshape(SG_HEADS, steps * steps).T,
                              SG_HEAD_DIM, axis=1)
        sgb_rows = jnp.repeat(sg_b[l][:, :steps].T, SG_HEAD_DIM, axis=1)
        (xs, cs, hs, zs), matrices = _sample_call(
            xs, p_sample[l], jnp.swapaxes(state_conv[l], 0, 1), state_lru[l], vectors, sources,
            sgw_rows, sgb_rows)
        conv_s.append(jnp.swapaxes(cs, 0, 1))
        lru_s.append(hs)
        chunk_s.append(zs)
        xp, cp, hp = _prompt_call(xp, p_prompt[l], vectors, matrices, sg_w[l], sg_b[l].T)
        conv_p.append(jnp.swapaxes(cp, 0, 1))
        lru_p.append(hp)
    return (xp, xs, jnp.stack(conv_p), jnp.stack(lru_p),
            jnp.stack(conv_s), jnp.stack(lru_s), jnp.stack(chunk_s))
```

```python
import functools

import jax
import jax.numpy as jnp
from jax import lax
from jax.experimental import pallas as pl
from jax.experimental.pallas import tpu as pltpu

D_MODEL = 1024
PLE_DIM = 256
SG_HEADS = 4
SG_HEAD_DIM = D_MODEL // SG_HEADS
CHUNK = 128
LRU_BLOCKS = 8
LRU_BLOCK_DIM = D_MODEL // LRU_BLOCKS
CONV_WIDTH = 4
LRU_C = 8.0
EPS = 1e-6
SQRT_FLOOR = 1e-30

SUBLANES = 8
LANES = 128
PROMPT_TILE = 256
PROMPT_LANES = 2
VMEM_LIMIT_BYTES = 56 * 1024 * 1024
STAGE_COLS = 1024
STAGE_SLOTS = 4
RING_SLOTS = 4

_BF16 = jnp.bfloat16
_F32 = jnp.float32
_U32 = jnp.uint32
ROWS_PER_WORD = 2

_VECTOR_NAMES = ("norm_pre", "sg_norm", "conv_w", "conv_b", "lru_ba", "lru_bx", "lru_lambda",
                 "b_merge", "norm_post", "b_ple_gate")
_MATRIX_SHAPES = dict(
    w_in=(D_MODEL, 5 * D_MODEL), w_merge=(D_MODEL, 2 * D_MODEL),
    w_branch_sg=(D_MODEL, D_MODEL), w_branch_lru=(D_MODEL, D_MODEL), w_out=(D_MODEL, D_MODEL),
    w_ple_gate=(D_MODEL, D_MODEL), w_ple=(PLE_DIM, D_MODEL),
    w_gate=(D_MODEL, 2 * LRU_BLOCK_DIM))
_MATRIX_NAMES = tuple(_MATRIX_SHAPES)

COL_U, COL_V, COL_G_SG, COL_X_LRU, COL_G_LRU = range(5)


def _dot(a, b):
    return jnp.dot(a, b, preferred_element_type=_F32)


def _pack_rows(wb):
    return pltpu.bitcast(wb, _U32)


def _dot_packed(a, words):
    return _dot(a, pltpu.bitcast(words, _BF16))


def _packed_rows(first, rows):
    return slice(first // ROWS_PER_WORD, (first + rows) // ROWS_PER_WORD)


def _rms_scale(x):
    var = jnp.mean(x * x, axis=-1, keepdims=True)
    return x * lax.rsqrt(var + EPS)


def _sigmoid(x):
    return 0.5 * jnp.tanh(0.5 * x) + 0.5


def _silu(x):
    return x * _sigmoid(x)


def _in_proj(hb, w, col):
    return _dot_packed(hb, w["w_in"][:, col * D_MODEL:(col + 1) * D_MODEL])


def _merge_pre(hb, w, half):
    return _dot_packed(hb, w["w_merge"][:, half * D_MODEL:(half + 1) * D_MODEL])


def _head_norm(v, sgn):
    zs = []
    for h in range(SG_HEADS):
        sl = slice(h * SG_HEAD_DIM, (h + 1) * SG_HEAD_DIM)
        zs.append(_rms_scale(v[:, sl]) * sgn[:, sl])
    return jnp.concatenate(zs, axis=-1)


def _lru_gates(xcb, wg_ref):
    gates = []
    for n in range(LRU_BLOCKS):
        sl = slice(n * LRU_BLOCK_DIM, (n + 1) * LRU_BLOCK_DIM)
        wg = wg_ref[_packed_rows(n * LRU_BLOCK_DIM, LRU_BLOCK_DIM), :]
        gates.append(_dot_packed(xcb[:, sl], wg))
    r_pre = jnp.concatenate([g[:, :LRU_BLOCK_DIM] for g in gates], axis=-1)
    i_pre = jnp.concatenate([g[:, LRU_BLOCK_DIM:] for g in gates], axis=-1)
    return r_pre, i_pre


def _lru_coeffs(xc, r_pre, i_pre, w):
    r = _sigmoid(r_pre + w["lru_ba"][...])
    i = _sigmoid(i_pre + w["lru_bx"][...])
    log_a = (-LRU_C * jax.nn.softplus(-w["lru_lambda"][...])) * r
    a = jnp.exp(log_a)
    gap = 1.0 - a * a
    mult = gap * lax.rsqrt(jnp.maximum(gap, SQRT_FLOOR))
    return a, mult * (i * xc)


def _merge_gate(pre, w, half):
    return _sigmoid(pre + w["b_merge"][:, half * D_MODEL:(half + 1) * D_MODEL])


def _post_mix(x, hb, y_sg, y_lru, p, w, mm):
    g_a = _merge_gate(mm["merge_a"](hb), w, 0)
    merged = g_a * mm["branch_sg"](y_sg.astype(_BF16))
    g_b = _merge_gate(mm["merge_b"](hb), w, 1)
    merged = merged + g_b * mm["branch_lru"](y_lru.astype(_BF16))
    o = mm["out"](merged.astype(_BF16))
    x1 = x + _rms_scale(o) * w["norm_post"][...]
    embedded = mm["ple"](p.astype(_BF16))
    gate = _sigmoid(mm["ple_gate"](x1.astype(_BF16)) + w["b_ple_gate"][...])
    return x1 + gate * embedded


def _resident_matmuls(w):
    return dict(
        merge_a=lambda lhs: _merge_pre(lhs, w, 0), merge_b=lambda lhs: _merge_pre(lhs, w, 1),
        branch_sg=lambda lhs: _dot_packed(lhs, w["w_branch_sg"][...]),
        branch_lru=lambda lhs: _dot_packed(lhs, w["w_branch_lru"][...]),
        out=lambda lhs: _dot_packed(lhs, w["w_out"][...]),
        ple_gate=lambda lhs: _dot_packed(lhs, w["w_ple_gate"][...]),
        ple=lambda lhs: _dot_packed(lhs, w["w_ple"][...]))


def _run_layout(tile):
    steps = tile // SUBLANES
    return steps, steps + SUBLANES


def _store_runs(buf, value, steps, pitch):
    for l in range(value.shape[1] // LANES):
        for s in range(SUBLANES):
            buf[l, s * pitch:s * pitch + steps, :] = (
                value[s * steps:(s + 1) * steps, l * LANES:(l + 1) * LANES])


def _load_runs(buf, steps, pitch):
    return jnp.concatenate(
        [jnp.concatenate([buf[l, s * pitch:s * pitch + steps, :] for l in range(buf.shape[0])],
                         axis=1) for s in range(SUBLANES)], axis=0)


def _load_step(buf, j, pitch):
    return jnp.concatenate([buf[l, pl.ds(j, SUBLANES, stride=pitch), :]
                            for l in range(buf.shape[0])], axis=1)


def _store_step(buf, j, pitch, slab):
    for l in range(buf.shape[0]):
        buf[l, pl.ds(j, SUBLANES, stride=pitch), :] = slab[:, l * LANES:(l + 1) * LANES]


def _from_previous_run(slab, first_run_rows):
    first = lax.broadcasted_iota(jnp.int32, slab.shape, 0) == 0
    return jnp.where(first, first_run_rows, pltpu.roll(slab, 1, 0))


def _conv_by_step(xs, conv_state, w):
    steps = len(xs)
    last = SUBLANES - 1
    before = [_from_previous_run(xs[steps - k], conv_state[SUBLANES - k:SUBLANES - k + 1, :])
              for k in range(CONV_WIDTH - 1, 0, -1)]
    for k in range(1, CONV_WIDTH):
        conv_state[SUBLANES - k:SUBLANES - k + 1, :] = xs[steps - k][last:, :]
    ext = before + xs
    out = []
    for j in range(steps):
        acc = w["conv_b"][...] + w["conv_w"][0:1, :] * ext[j]
        for k in range(1, CONV_WIDTH):
            acc = acc + w["conv_w"][k:k + 1, :] * ext[j + k]
        out.append(acc)
    return out


def _scan_by_step(a, b, h_prev):
    steps = a.shape[0] // SUBLANES
    slab = lambda v, j: v[j * SUBLANES:(j + 1) * SUBLANES, :]
    h0, decay = [slab(b, 0)], [slab(a, 0)]
    for j in range(1, steps):
        h0.append(slab(a, j) * h0[-1] + slab(b, j))
        decay.append(slab(a, j) * decay[-1])
    end_h, end_decay = h0[-1], decay[-1]
    row = lax.broadcasted_iota(jnp.int32, end_h.shape, 0)
    shift = 1
    while shift < SUBLANES:
        keep = row >= shift
        end_h = end_h + end_decay * jnp.where(keep, pltpu.roll(end_h, shift, 0), 0.0)
        end_decay = end_decay * jnp.where(keep, pltpu.roll(end_decay, shift, 0), 1.0)
        shift *= 2
    ends = end_h + end_decay * h_prev
    entering = _from_previous_run(ends, h_prev)
    hs = [h0[j] + decay[j] * entering for j in range(steps)]
    return hs, ends[SUBLANES - 1:, :]


def _spatial_mix_chunk(z, c, sgw_ref, sgb_ref):
    tri = (lax.broadcasted_iota(jnp.int32, (CHUNK, CHUNK), 0)
           >= lax.broadcasted_iota(jnp.int32, (CHUNK, CHUNK), 1))
    s_heads = []
    for h in range(SG_HEADS):
        wm = jnp.where(tri, sgw_ref[h], 0.0).astype(_BF16)
        zc = z[c * CHUNK:(c + 1) * CHUNK, h * SG_HEAD_DIM:(h + 1) * SG_HEAD_DIM]
        s_heads.append(_dot(wm, zc) + sgb_ref[:, h:h + 1])
    return jnp.concatenate(s_heads, axis=-1)


_MIDDLE_MATMUL_ORDER = (
    ("gate", 0), ("gate", 1), ("wide", COL_U), ("spatial", 0),
    ("gate", 2), ("gate", 3), ("wide", COL_G_SG), ("spatial", 1),
    ("gate", 4), ("gate", 5), ("wide", COL_G_LRU), ("gate", 6), ("gate", 7))


def _tile_layer(x, p, runs_buf, conv_state, h_state, gate_bufs, w, sgw_ref, sgb_ref):
    tile = x.shape[0]
    hb = (_rms_scale(x) * w["norm_pre"][...]).astype(_BF16)

    steps, pitch = _run_layout(tile)
    _store_runs(runs_buf, _in_proj(hb, w, COL_X_LRU), steps, pitch)

    v = _in_proj(hb, w, COL_V)
    z = _head_norm(v, w["sg_norm"][...]).astype(_BF16)

    xs = [_load_step(runs_buf, j, pitch) for j in range(steps)]
    xc = jnp.concatenate(_conv_by_step(xs, conv_state, w), axis=0)
    xcb = xc.astype(_BF16)
    wide, s_chunks = {}, {}
    for kind, k in _MIDDLE_MATMUL_ORDER:
        if kind == "wide":
            wide[k] = _in_proj(hb, w, k)
        elif kind == "gate":
            sl = slice(k * LRU_BLOCK_DIM, (k + 1) * LRU_BLOCK_DIM)
            wg = w["w_gate"][_packed_rows(k * LRU_BLOCK_DIM, LRU_BLOCK_DIM), :]
            at = (k % 2) * 2 * LRU_BLOCK_DIM
            gate_bufs[k // 2][:, at:at + 2 * LRU_BLOCK_DIM] = _dot_packed(xcb[:, sl], wg)
        else:
            s_chunks[k] = _spatial_mix_chunk(z, k, sgw_ref, sgb_ref)
    u, g_sg, g_lru = wide[COL_U], wide[COL_G_SG], wide[COL_G_LRU]
    s = jnp.concatenate([s_chunks[c] for c in range(tile // CHUNK)], axis=0)

    rows = pl.ds(pl.multiple_of(jnp.minimum(pl.program_id(1), 0), SUBLANES), tile)

    def parked(n, half):
        at = ((n % 2) * 2 + half) * LRU_BLOCK_DIM
        return gate_bufs[n // 2][rows, at:at + LRU_BLOCK_DIM]

    r_pre = jnp.concatenate([parked(n, 0) for n in range(LRU_BLOCKS)], axis=-1)
    i_pre = jnp.concatenate([parked(n, 1) for n in range(LRU_BLOCKS)], axis=-1)

    y_sg = u * s * _silu(g_sg)

    a, bterm = _lru_coeffs(xc, r_pre, i_pre, w)
    hs, h_last = _scan_by_step(a, bterm, h_state[...])
    h_state[...] = h_last
    for j in range(steps):
        _store_step(runs_buf, j, pitch, hs[j])
    y_lru = _load_runs(runs_buf, steps, pitch) * _silu(g_lru)

    return _post_mix(x, hb, y_sg, y_lru, p, w, _resident_matmuls(w)), h_last


def _prompt_kernel(*refs):
    n_w = len(_VECTOR_NAMES) + len(_MATRIX_NAMES)
    x_ref, p_ref = refs[0], refs[1]
    w = dict(zip(_VECTOR_NAMES + _MATRIX_NAMES, refs[2:2 + n_w]))
    sgw_ref, sgb_ref = refs[2 + n_w], refs[3 + n_w]
    y_ref, conv_out_ref, lru_out_ref = refs[4 + n_w:7 + n_w]
    runs_buf, conv_state, h_state = refs[7 + n_w:10 + n_w]
    gate_bufs = refs[10 + n_w:]

    t = pl.program_id(1)
    lanes = x_ref.shape[0]

    @pl.when(t == 0)
    def _():
        conv_state[...] = jnp.zeros_like(conv_state)
        h_state[...] = jnp.zeros_like(h_state)

    h_last = []
    for i in range(lanes):
        y, h = _tile_layer(x_ref[i], p_ref[i], runs_buf.at[i], conv_state.at[i], h_state.at[i],
                           [g.at[i] for g in gate_bufs], w, sgw_ref, sgb_ref)
        y_ref[i] = y
        h_last.append(h)

    @pl.when(t == pl.num_programs(1) - 1)
    def _():
        for i in range(lanes):
            seq = pl.ds(i * pl.num_programs(0) + pl.program_id(0), 1)
            for k in range(CONV_WIDTH - 1):
                row = SUBLANES - (CONV_WIDTH - 1) + k
                conv_out_ref[k, seq, :] = conv_state[i, row:row + 1, :]
            lru_out_ref[seq, :] = h_last[i]


def _resident(shape):
    return pl.BlockSpec(shape, lambda *_: (0,) * len(shape), pipeline_mode=pl.Buffered(1))


def _prompt_call(x, p, vectors, matrices, sgw, sgb_t):
    batch, seq, d = x.shape
    tile = PROMPT_TILE
    lanes = PROMPT_LANES
    grid = (batch // lanes, seq // tile)
    x4 = x.reshape(lanes, batch // lanes, seq, d)
    p4 = p.reshape(lanes, batch // lanes, seq, PLE_DIM)
    in_specs = [
        pl.BlockSpec((lanes, None, tile, d), lambda b, t: (0, b, t, 0)),
        pl.BlockSpec((lanes, None, tile, PLE_DIM), lambda b, t: (0, b, t, 0)),
    ]
    in_specs += [_resident(v.shape) for v in vectors + matrices]
    in_specs += [_resident(sgw.shape), _resident(sgb_t.shape)]
    out_shape = (
        jax.ShapeDtypeStruct((lanes, batch // lanes, seq, d), _F32),
        jax.ShapeDtypeStruct((CONV_WIDTH - 1, batch, d), _F32),
        jax.ShapeDtypeStruct((batch, d), _F32),
    )
    out_specs = (
        pl.BlockSpec((lanes, None, tile, d), lambda b, t: (0, b, t, 0)),
        pl.BlockSpec((CONV_WIDTH - 1, batch, d), lambda b, t: (0, 0, 0)),
        pl.BlockSpec((batch, d), lambda b, t: (0, 0)),
    )
    y, conv_rows, state = pl.pallas_call(
        _prompt_kernel,
        grid=grid,
        in_specs=in_specs,
        out_specs=out_specs,
        out_shape=out_shape,
        scratch_shapes=[
            pltpu.VMEM((lanes, d // LANES, SUBLANES * _run_layout(tile)[1], LANES), _F32),
            pltpu.VMEM((lanes, SUBLANES, d), _F32),
            pltpu.VMEM((lanes, 1, d), _F32),
        ] + [pltpu.VMEM((lanes, tile, 4 * LRU_BLOCK_DIM), _F32)
             for _ in range(LRU_BLOCKS // 2)],
        compiler_params=pltpu.CompilerParams(
            dimension_semantics=("arbitrary", "arbitrary"),
            vmem_limit_bytes=VMEM_LIMIT_BYTES),
        name="prompt_layer",
    )(x4, p4, *vectors, *matrices, sgw, sgb_t)
    return y.reshape(batch, seq, d), conv_rows, state


_SOURCE_NAMES = ("w_in", "w_merge", "w_branch_sg", "w_branch_lru", "w_out", "w_ple_gate",
                 "w_ple", "lru_wa", "lru_wx")


def _column_chunks(ref, matrix, first, last):
    return [([(ref, c, min(STAGE_COLS, last - c), 0)], matrix, c)
            for c in range(first, last, STAGE_COLS)]


class _WeightStream:
    def __init__(self, tasks, out, stage, ring, sem_in, sem_out):
        self.tasks, self.out, self.stage, self.ring = tasks, out, stage, ring
        self.sem_in, self.sem_out = sem_in, sem_out
        self.taken = 0
        for i in range(min(STAGE_SLOTS - 1, len(tasks))):
            self._start_in(i)

    def _shape(self, i):
        parts = self.tasks[i][0]
        return parts[0][0].shape[0], sum(part[2] for part in parts)

    def _in_copies(self, i):
        slot = i % STAGE_SLOTS
        rows, _ = self._shape(i)
        return [pltpu.make_async_copy(
            ref.at[:, pl.ds(col, cols)],
            self.stage.at[slot, pl.ds(0, rows), pl.ds(at, cols)],
            self.sem_in.at[slot]) for ref, col, cols, at in self.tasks[i][0]]

    def _start_in(self, i):
        for copy in self._in_copies(i):
            copy.start()

    def _out_copy(self, i):
        slot = i % RING_SLOTS
        rows, cols = self._shape(i)
        _, matrix, first = self.tasks[i]
        return pltpu.make_async_copy(
            self.ring.at[slot, pl.ds(0, rows // ROWS_PER_WORD), pl.ds(0, cols)],
            self.out[matrix].at[:, pl.ds(first, cols)],
            self.sem_out.at[slot])

    def take(self):
        i = self.taken
        self.taken += 1
        if i + STAGE_SLOTS - 1 < len(self.tasks):
            self._start_in(i + STAGE_SLOTS - 1)
        for copy in self._in_copies(i):
            copy.wait()
        if i >= RING_SLOTS:
            self._out_copy(i - RING_SLOTS).wait()
        rows, cols = self._shape(i)
        staged = self.stage[i % STAGE_SLOTS, 0:rows, 0:cols]
        words = rows // ROWS_PER_WORD
        self.ring[i % RING_SLOTS, 0:words, 0:cols] = _pack_rows(staged.astype(_BF16))
        self._out_copy(i).start()
        return self.ring.at[i % RING_SLOTS, pl.ds(0, words), pl.ds(0, cols)]

    def matmul(self, lhs, chunks):
        return jnp.concatenate([_dot_packed(lhs, self.take()[...]) for _ in range(chunks)],
                               axis=-1)

    def finish(self):
        assert self.taken == len(self.tasks)
        for i in range(max(0, len(self.tasks) - RING_SLOTS), len(self.tasks)):
            self._out_copy(i).wait()


def _sample_tasks(src):
    d = D_MODEL
    half = lambda name, h: _column_chunks(src[name], name, h * d, (h + 1) * d)
    gate = [([(src["lru_wa"], 0, LRU_BLOCK_DIM, 0),
              (src["lru_wx"], 0, LRU_BLOCK_DIM, LRU_BLOCK_DIM)], "w_gate", 0)]
    return (_column_chunks(src["w_in"], "w_in", 0, 5 * d) + gate
            + half("w_merge", 0) + half("w_branch_sg", 0) + half("w_merge", 1)
            + half("w_branch_lru", 0) + half("w_out", 0) + half("w_ple", 0)
            + half("w_ple_gate", 0))


def _sample_kernel(*refs, steps):
    n_v, n_s, n_m = len(_VECTOR_NAMES), len(_SOURCE_NAMES), len(_MATRIX_NAMES)
    x_ref, p_ref, conv_in_ref, h0_ref = refs[:4]
    w = dict(zip(_VECTOR_NAMES, refs[4:4 + n_v]))
    sgw_ref, sgb_ref = refs[4 + n_v], refs[5 + n_v]
    src = dict(zip(_SOURCE_NAMES, refs[6 + n_v:6 + n_v + n_s]))
    outs = refs[6 + n_v + n_s:]
    y_ref, conv_out_ref, lru_out_ref, z_out_ref = outs[:4]
    out_packed = dict(zip(_MATRIX_NAMES, outs[4:4 + n_m]))
    stage, ring, sem_in, sem_out = outs[4 + n_m:]
    d = D_MODEL
    nb = x_ref.shape[0]

    stream = _WeightStream(_sample_tasks(src), out_packed, stage, ring, sem_in, sem_out)
    chunks_per_block = d // STAGE_COLS

    def slab(v, t):
        return v[t * nb:(t + 1) * nb]

    x = jnp.concatenate([x_ref[:, t, :] for t in range(steps)], axis=0)
    p = jnp.concatenate([p_ref[:, t, :] for t in range(steps)], axis=0)
    hb = (_rms_scale(x) * w["norm_pre"][...]).astype(_BF16)
    proj = stream.matmul(hb, 5 * chunks_per_block)
    col = lambda c: proj[:, c * d:(c + 1) * d]

    z = _head_norm(col(COL_V), w["sg_norm"][...])
    for t in range(steps):
        z_out_ref[:, t, :] = slab(z, t)
    s_slabs = []
    for t in range(steps):
        acc = jnp.broadcast_to(sgb_ref[t:t + 1, :], (nb, d))
        for u in range(t + 1):
            acc = acc + sgw_ref[t * steps + u:t * steps + u + 1, :] * slab(z, u)
        s_slabs.append(acc)
    s = jnp.concatenate(s_slabs, axis=0)
    y_sg = col(COL_U) * s * _silu(col(COL_G_SG))

    x_lru = col(COL_X_LRU)
    hist = [conv_in_ref[k] for k in range(CONV_WIDTH - 1)]
    hist += [slab(x_lru, t) for t in range(steps)]
    for k in range(CONV_WIDTH - 1):
        conv_out_ref[k] = hist[steps + k]
    xc_slabs = []
    for t in range(steps):
        acc = w["conv_b"][...] + w["conv_w"][0:1, :] * hist[t]
        for k in range(1, CONV_WIDTH):
            acc = acc + w["conv_w"][k:k + 1, :] * hist[t + k]
        xc_slabs.append(acc)
    xc = jnp.concatenate(xc_slabs, axis=0)
    r_pre, i_pre = _lru_gates(xc.astype(_BF16), stream.take())
    a, bterm = _lru_coeffs(xc, r_pre, i_pre, w)
    h = h0_ref[...]
    h_slabs = []
    for t in range(steps):
        h = slab(a, t) * h + slab(bterm, t)
        h_slabs.append(h)
    lru_out_ref[...] = h
    y_lru = jnp.concatenate(h_slabs, axis=0) * _silu(col(COL_G_LRU))

    block = lambda lhs: stream.matmul(lhs, chunks_per_block)
    mm = {name: block for name in
          ("merge_a", "branch_sg", "merge_b", "branch_lru", "out", "ple", "ple_gate")}
    y = _post_mix(x, hb, y_sg, y_lru, p, w, mm)
    for t in range(steps):
        y_ref[:, t, :] = slab(y, t)

    stream.finish()


def _sample_call(x, p, conv_in, h0, vectors, sources, sgw_rows, sgb_rows):
    nb, steps, d = x.shape
    vmem = pl.BlockSpec(memory_space=pltpu.VMEM)
    hbm = pl.BlockSpec(memory_space=pl.ANY)
    matrix_shapes = [_MATRIX_SHAPES[m] for m in _MATRIX_NAMES]
    stage_rows = max(s.shape[0] for s in sources)
    out_shape = (
        jax.ShapeDtypeStruct((nb, steps, d), _F32),
        jax.ShapeDtypeStruct((CONV_WIDTH - 1, nb, d), _F32),
        jax.ShapeDtypeStruct((nb, d), _F32),
        jax.ShapeDtypeStruct((nb, steps, d), _F32),
    ) + tuple(jax.ShapeDtypeStruct((rows // ROWS_PER_WORD, cols), _U32)
              for rows, cols in matrix_shapes)
    outs = pl.pallas_call(
        functools.partial(_sample_kernel, steps=steps),
        in_specs=[vmem] * (6 + len(vectors)) + [hbm] * len(sources),
        out_specs=(vmem,) * 4 + (hbm,) * len(matrix_shapes),
        out_shape=out_shape,
        scratch_shapes=[
            pltpu.VMEM((STAGE_SLOTS, stage_rows, STAGE_COLS), _F32),
            pltpu.VMEM((RING_SLOTS, stage_rows // ROWS_PER_WORD, STAGE_COLS), _U32),
            pltpu.SemaphoreType.DMA((STAGE_SLOTS,)),
            pltpu.SemaphoreType.DMA((RING_SLOTS,)),
        ],
        compiler_params=pltpu.CompilerParams(vmem_limit_bytes=VMEM_LIMIT_BYTES),
        name="sample_layer",
    )(x, p, conv_in, h0, *vectors, sgw_rows, sgb_rows, *sources)
    return outs[:4], list(outs[4:])


def kernel(x_prompt, x_sample, p_prompt, p_sample, state_conv, state_lru, norm_pre, w_in, sg_norm,
           sg_w, sg_b, conv_w, conv_b, lru_wa, lru_ba, lru_wx, lru_bx, lru_lambda, w_branch_sg,
           w_branch_lru, w_merge, b_merge, w_out, norm_post, w_ple, w_ple_gate, b_ple_gate):
    depth = norm_pre.shape[0]
    nb, steps, d = x_sample.shape
    xp, xs = x_prompt, x_sample
    conv_p, lru_p, conv_s, lru_s, chunk_s = [], [], [], [], []
    for l in range(depth):
        row = lambda v: v[l].reshape(1, -1)
        by_name = dict(
            norm_pre=row(norm_pre), sg_norm=row(sg_norm), conv_w=conv_w[l], conv_b=row(conv_b),
            lru_ba=row(lru_ba), lru_bx=row(lru_bx), lru_lambda=row(lru_lambda),
            b_merge=row(b_merge), norm_post=row(norm_post), b_ple_gate=row(b_ple_gate))
        vectors = [by_name[n] for n in _VECTOR_NAMES]
        f32_by_name = dict(
            w_in=w_in[l], w_merge=w_merge[l], w_branch_sg=w_branch_sg[l],
            w_branch_lru=w_branch_lru[l], w_out=w_out[l], w_ple_gate=w_ple_gate[l],
            w_ple=w_ple[l], lru_wa=lru_wa[l].reshape(d, LRU_BLOCK_DIM),
            lru_wx=lru_wx[l].reshape(d, LRU_BLOCK_DIM))
        sources = [f32_by_name[name] for name in _SOURCE_NAMES]
        sgw_rows = jnp.repeat(sg_w[l][:, :steps, :steps].reshape(SG_HEADS, steps * steps).T,
                              SG_HEAD_DIM, axis=1)
        sgb_rows = jnp.repeat(sg_b[l][:, :steps].T, SG_HEAD_DIM, axis=1)
        (xs, cs, hs, zs), matrices = _sample_call(
            xs, p_sample[l], jnp.swapaxes(state_conv[l], 0, 1), state_lru[l], vectors, sources,
            sgw_rows, sgb_rows)
        conv_s.append(jnp.swapaxes(cs, 0, 1))
        lru_s.append(hs)
        chunk_s.append(zs)
        xp, cp, hp = _prompt_call(xp, p_prompt[l], vectors, matrices, sg_w[l], sg_b[l].T)
        conv_p.append(jnp.swapaxes(cp, 0, 1))
        lru_p.append(hp)
    return (xp, xs, jnp.stack(conv_p), jnp.stack(lru_p),
            jnp.stack(conv_s), jnp.stack(lru_s), jnp.stack(chunk_s))
```

```python
import functools

import jax
import jax.numpy as jnp
from jax import lax
from jax.experimental import pallas as pl
from jax.experimental.pallas import tpu as pltpu

D_MODEL = 1024
PLE_DIM = 256
SG_HEADS = 4
SG_HEAD_DIM = D_MODEL // SG_HEADS
CHUNK = 128
LRU_BLOCKS = 8
LRU_BLOCK_DIM = D_MODEL // LRU_BLOCKS
CONV_WIDTH = 4
LRU_C = 8.0
EPS = 1e-6
SQRT_FLOOR = 1e-30

SUBLANES = 8
LANES = 128
PROMPT_TILE = 256
PROMPT_LANES = 2
VMEM_LIMIT_BYTES = 56 * 1024 * 1024
STAGE_COLS = 1024
STAGE_SLOTS = 4
RING_SLOTS = 4

_BF16 = jnp.bfloat16
_F32 = jnp.float32
_U32 = jnp.uint32
ROWS_PER_WORD = 2

_VECTOR_NAMES = ("norm_pre", "sg_norm", "conv_w", "conv_b", "lru_ba", "lru_bx", "lru_lambda",
                 "b_merge", "norm_post", "b_ple_gate")
_MATRIX_SHAPES = dict(
    w_in=(D_MODEL, 5 * D_MODEL), w_merge=(D_MODEL, 2 * D_MODEL),
    w_branch_sg=(D_MODEL, D_MODEL), w_branch_lru=(D_MODEL, D_MODEL), w_out=(D_MODEL, D_MODEL),
    w_ple_gate=(D_MODEL, D_MODEL), w_ple=(PLE_DIM, D_MODEL),
    w_gate=(D_MODEL, 2 * LRU_BLOCK_DIM))
_MATRIX_NAMES = tuple(_MATRIX_SHAPES)

COL_U, COL_V, COL_G_SG, COL_X_LRU, COL_G_LRU = range(5)


def _dot(a, b):
    return jnp.dot(a, b, preferred_element_type=_F32)


def _pack_rows(wb):
    return pltpu.bitcast(wb, _U32)


def _dot_packed(a, words):
    return _dot(a, pltpu.bitcast(words, _BF16))


def _packed_rows(first, rows):
    return slice(first // ROWS_PER_WORD, (first + rows) // ROWS_PER_WORD)


def _rms_scale(x):
    var = jnp.mean(x * x, axis=-1, keepdims=True)
    return x * lax.rsqrt(var + EPS)


def _sigmoid(x):
    return 0.5 * jnp.tanh(0.5 * x) + 0.5


def _silu(x):
    return x * _sigmoid(x)


def _in_proj(hb, w, col):
    return _dot_packed(hb, w["w_in"][:, col * D_MODEL:(col + 1) * D_MODEL])


def _merge_pre(hb, w, half):
    return _dot_packed(hb, w["w_merge"][:, half * D_MODEL:(half + 1) * D_MODEL])


def _head_norm(v, sgn):
    zs = []
    for h in range(SG_HEADS):
        sl = slice(h * SG_HEAD_DIM, (h + 1) * SG_HEAD_DIM)
        zs.append(_rms_scale(v[:, sl]) * sgn[:, sl])
    return jnp.concatenate(zs, axis=-1)


def _lru_gates(xcb, wg_ref):
    gates = []
    for n in range(LRU_BLOCKS):
        sl = slice(n * LRU_BLOCK_DIM, (n + 1) * LRU_BLOCK_DIM)
        wg = wg_ref[_packed_rows(n * LRU_BLOCK_DIM, LRU_BLOCK_DIM), :]
        gates.append(_dot_packed(xcb[:, sl], wg))
    r_pre = jnp.concatenate([g[:, :LRU_BLOCK_DIM] for g in gates], axis=-1)
    i_pre = jnp.concatenate([g[:, LRU_BLOCK_DIM:] for g in gates], axis=-1)
    return r_pre, i_pre


def _lru_coeffs(xc, r_pre, i_pre, w):
    r = _sigmoid(r_pre + w["lru_ba"][...])
    i = _sigmoid(i_pre + w["lru_bx"][...])
    log_a = (-LRU_C * jax.nn.softplus(-w["lru_lambda"][...])) * r
    a = jnp.exp(log_a)
    gap = 1.0 - a * a
    mult = gap * lax.rsqrt(jnp.maximum(gap, SQRT_FLOOR))
    return a, mult * (i * xc)


def _merge_gate(pre, w, half):
    return _sigmoid(pre + w["b_merge"][:, half * D_MODEL:(half + 1) * D_MODEL])


def _post_mix(x, hb, y_sg, y_lru, p, w, mm):
    g_a = _merge_gate(mm["merge_a"](hb), w, 0)
    merged = g_a * mm["branch_sg"](y_sg.astype(_BF16))
    g_b = _merge_gate(mm["merge_b"](hb), w, 1)
    merged = merged + g_b * mm["branch_lru"](y_lru.astype(_BF16))
    o = mm["out"](merged.astype(_BF16))
    x1 = x + _rms_scale(o) * w["norm_post"][...]
    embedded = mm["ple"](p.astype(_BF16))
    gate = _sigmoid(mm["ple_gate"](x1.astype(_BF16)) + w["b_ple_gate"][...])
    return x1 + gate * embedded


def _resident_matmuls(w):
    return dict(
        merge_a=lambda lhs: _merge_pre(lhs, w, 0), merge_b=lambda lhs: _merge_pre(lhs, w, 1),
        branch_sg=lambda lhs: _dot_packed(lhs, w["w_branch_sg"][...]),
        branch_lru=lambda lhs: _dot_packed(lhs, w["w_branch_lru"][...]),
        out=lambda lhs: _dot_packed(lhs, w["w_out"][...]),
        ple_gate=lambda lhs: _dot_packed(lhs, w["w_ple_gate"][...]),
        ple=lambda lhs: _dot_packed(lhs, w["w_ple"][...]))


def _run_layout(tile):
    steps = tile // SUBLANES
    return steps, steps + SUBLANES


def _store_runs(buf, value, steps, pitch):
    for l in range(value.shape[1] // LANES):
        for s in range(SUBLANES):
            buf[l, s * pitch:s * pitch + steps, :] = (
                value[s * steps:(s + 1) * steps, l * LANES:(l + 1) * LANES])


def _load_runs(buf, steps, pitch):
    return jnp.concatenate(
        [jnp.concatenate([buf[l, s * pitch:s * pitch + steps, :] for l in range(buf.shape[0])],
                         axis=1) for s in range(SUBLANES)], axis=0)


def _load_step(buf, j, pitch):
    return jnp.concatenate([buf[l, pl.ds(j, SUBLANES, stride=pitch), :]
                            for l in range(buf.shape[0])], axis=1)


def _store_step(buf, j, pitch, slab):
    for l in range(buf.shape[0]):
        buf[l, pl.ds(j, SUBLANES, stride=pitch), :] = slab[:, l * LANES:(l + 1) * LANES]


def _from_previous_run(slab, first_run_rows):
    first = lax.broadcasted_iota(jnp.int32, slab.shape, 0) == 0
    return jnp.where(first, first_run_rows, pltpu.roll(slab, 1, 0))


def _conv_by_step(xs, conv_state, w):
    steps = len(xs)
    last = SUBLANES - 1
    before = [_from_previous_run(xs[steps - k], conv_state[SUBLANES - k:SUBLANES - k + 1, :])
              for k in range(CONV_WIDTH - 1, 0, -1)]
    for k in range(1, CONV_WIDTH):
        conv_state[SUBLANES - k:SUBLANES - k + 1, :] = xs[steps - k][last:, :]
    ext = before + xs
    out = []
    for j in range(steps):
        acc = w["conv_b"][...] + w["conv_w"][0:1, :] * ext[j]
        for k in range(1, CONV_WIDTH):
            acc = acc + w["conv_w"][k:k + 1, :] * ext[j + k]
        out.append(acc)
    return out


def _scan_by_step(a, b, h_prev):
    steps = a.shape[0] // SUBLANES
    slab = lambda v, j: v[j * SUBLANES:(j + 1) * SUBLANES, :]
    h0, decay = [slab(b, 0)], [slab(a, 0)]
    for j in range(1, steps):
        h0.append(slab(a, j) * h0[-1] + slab(b, j))
        decay.append(slab(a, j) * decay[-1])
    end_h, end_decay = h0[-1], decay[-1]
    row = lax.broadcasted_iota(jnp.int32, end_h.shape, 0)
    shift = 1
    while shift < SUBLANES:
        keep = row >= shift
        end_h = end_h + end_decay * jnp.where(keep, pltpu.roll(end_h, shift, 0), 0.0)
        end_decay = end_decay * jnp.where(keep, pltpu.roll(end_decay, shift, 0), 1.0)
        shift *= 2
    ends = end_h + end_decay * h_prev
    entering = _from_previous_run(ends, h_prev)
    hs = [h0[j] + decay[j] * entering for j in range(steps)]
    return hs, ends[SUBLANES - 1:, :]


def _spatial_mix_chunk(z, c, sgw_ref, sgb_ref):
    tri = (lax.broadcasted_iota(jnp.int32, (CHUNK, CHUNK), 0)
           >= lax.broadcasted_iota(jnp.int32, (CHUNK, CHUNK), 1))
    s_heads = []
    for h in range(SG_HEADS):
        wm = jnp.where(tri, sgw_ref[h], 0.0).astype(_BF16)
        zc = z[c * CHUNK:(c + 1) * CHUNK, h * SG_HEAD_DIM:(h + 1) * SG_HEAD_DIM]
        s_heads.append(_dot(wm, zc) + sgb_ref[:, h:h + 1])
    return jnp.concatenate(s_heads, axis=-1)


_MIDDLE_MATMUL_ORDER = (
    ("gate", 0), ("gate", 1), ("gate", 2), ("gate", 3), ("wide", COL_U), ("spatial", 0),
    ("gate", 4), ("gate", 5), ("gate", 6), ("gate", 7), ("wide", COL_G_SG), ("spatial", 1),
    ("wide", COL_G_LRU))


def _tile_layer(x, p, runs_buf, conv_state, h_state, gate_bufs, w, sgw_ref, sgb_ref):
    tile = x.shape[0]
    hb = (_rms_scale(x) * w["norm_pre"][...]).astype(_BF16)

    steps, pitch = _run_layout(tile)
    _store_runs(runs_buf, _in_proj(hb, w, COL_X_LRU), steps, pitch)

    v = _in_proj(hb, w, COL_V)
    z = _head_norm(v, w["sg_norm"][...]).astype(_BF16)

    xs = [_load_step(runs_buf, j, pitch) for j in range(steps)]
    xc = jnp.concatenate(_conv_by_step(xs, conv_state, w), axis=0)
    xcb = xc.astype(_BF16)
    wide, s_chunks = {}, {}
    for kind, k in _MIDDLE_MATMUL_ORDER:
        if kind == "wide":
            wide[k] = _in_proj(hb, w, k)
        elif kind == "gate":
            sl = slice(k * LRU_BLOCK_DIM, (k + 1) * LRU_BLOCK_DIM)
            wg = w["w_gate"][_packed_rows(k * LRU_BLOCK_DIM, LRU_BLOCK_DIM), :]
            at = (k % 2) * 2 * LRU_BLOCK_DIM
            gate_bufs[k // 2][:, at:at + 2 * LRU_BLOCK_DIM] = _dot_packed(xcb[:, sl], wg)
        else:
            s_chunks[k] = _spatial_mix_chunk(z, k, sgw_ref, sgb_ref)
    u, g_sg, g_lru = wide[COL_U], wide[COL_G_SG], wide[COL_G_LRU]
    s = jnp.concatenate([s_chunks[c] for c in range(tile // CHUNK)], axis=0)

    rows = pl.ds(pl.multiple_of(jnp.minimum(pl.program_id(1), 0), SUBLANES), tile)

    def parked(n, half):
        at = ((n % 2) * 2 + half) * LRU_BLOCK_DIM
        return gate_bufs[n // 2][rows, at:at + LRU_BLOCK_DIM]

    r_pre = jnp.concatenate([parked(n, 0) for n in range(LRU_BLOCKS)], axis=-1)
    i_pre = jnp.concatenate([parked(n, 1) for n in range(LRU_BLOCKS)], axis=-1)

    y_sg = u * s * _silu(g_sg)

    a, bterm = _lru_coeffs(xc, r_pre, i_pre, w)
    hs, h_last = _scan_by_step(a, bterm, h_state[...])
    h_state[...] = h_last
    for j in range(steps):
        _store_step(runs_buf, j, pitch, hs[j])
    y_lru = _load_runs(runs_buf, steps, pitch) * _silu(g_lru)

    return _post_mix(x, hb, y_sg, y_lru, p, w, _resident_matmuls(w)), h_last


def _prompt_kernel(*refs):
    n_w = len(_VECTOR_NAMES) + len(_MATRIX_NAMES)
    x_ref, p_ref = refs[0], refs[1]
    w = dict(zip(_VECTOR_NAMES + _MATRIX_NAMES, refs[2:2 + n_w]))
    sgw_ref, sgb_ref = refs[2 + n_w], refs[3 + n_w]
    y_ref, conv_out_ref, lru_out_ref = refs[4 + n_w:7 + n_w]
    runs_buf, conv_state, h_state = refs[7 + n_w:10 + n_w]
    gate_bufs = refs[10 + n_w:]

    t = pl.program_id(1)
    lanes = x_ref.shape[0]

    @pl.when(t == 0)
    def _():
        conv_state[...] = jnp.zeros_like(conv_state)
        h_state[...] = jnp.zeros_like(h_state)

    h_last = []
    for i in range(lanes):
        y, h = _tile_layer(x_ref[i], p_ref[i], runs_buf.at[i], conv_state.at[i], h_state.at[i],
                           [g.at[i] for g in gate_bufs], w, sgw_ref, sgb_ref)
        y_ref[i] = y
        h_last.append(h)

    @pl.when(t == pl.num_programs(1) - 1)
    def _():
        for i in range(lanes):
            seq = pl.ds(i * pl.num_programs(0) + pl.program_id(0), 1)
            for k in range(CONV_WIDTH - 1):
                row = SUBLANES - (CONV_WIDTH - 1) + k
                conv_out_ref[k, seq, :] = conv_state[i, row:row + 1, :]
            lru_out_ref[seq, :] = h_last[i]


def _resident(shape):
    return pl.BlockSpec(shape, lambda *_: (0,) * len(shape), pipeline_mode=pl.Buffered(1))


def _prompt_call(x, p, vectors, matrices, sgw, sgb_t):
    batch, seq, d = x.shape
    tile = PROMPT_TILE
    lanes = PROMPT_LANES
    grid = (batch // lanes, seq // tile)
    x4 = x.reshape(lanes, batch // lanes, seq, d)
    p4 = p.reshape(lanes, batch // lanes, seq, PLE_DIM)
    in_specs = [
        pl.BlockSpec((lanes, None, tile, d), lambda b, t: (0, b, t, 0)),
        pl.BlockSpec((lanes, None, tile, PLE_DIM), lambda b, t: (0, b, t, 0)),
    ]
    in_specs += [_resident(v.shape) for v in vectors + matrices]
    in_specs += [_resident(sgw.shape), _resident(sgb_t.shape)]
    out_shape = (
        jax.ShapeDtypeStruct((lanes, batch // lanes, seq, d), _F32),
        jax.ShapeDtypeStruct((CONV_WIDTH - 1, batch, d), _F32),
        jax.ShapeDtypeStruct((batch, d), _F32),
    )
    out_specs = (
        pl.BlockSpec((lanes, None, tile, d), lambda b, t: (0, b, t, 0)),
        pl.BlockSpec((CONV_WIDTH - 1, batch, d), lambda b, t: (0, 0, 0)),
        pl.BlockSpec((batch, d), lambda b, t: (0, 0)),
    )
    y, conv_rows, state = pl.pallas_call(
        _prompt_kernel,
        grid=grid,
        in_specs=in_specs,
        out_specs=out_specs,
        out_shape=out_shape,
        scratch_shapes=[
            pltpu.VMEM((lanes, d // LANES, SUBLANES * _run_layout(tile)[1], LANES), _F32),
            pltpu.VMEM((lanes, SUBLANES, d), _F32),
            pltpu.VMEM((lanes, 1, d), _F32),
        ] + [pltpu.VMEM((lanes, tile, 4 * LRU_BLOCK_DIM), _F32)
             for _ in range(LRU_BLOCKS // 2)],
        compiler_params=pltpu.CompilerParams(
            dimension_semantics=("arbitrary", "arbitrary"),
            vmem_limit_bytes=VMEM_LIMIT_BYTES),
        name="prompt_layer",
    )(x4, p4, *vectors, *matrices, sgw, sgb_t)
    return y.reshape(batch, seq, d), conv_rows, state


_SOURCE_NAMES = ("w_in", "w_merge", "w_branch_sg", "w_branch_lru", "w_out", "w_ple_gate",
                 "w_ple", "lru_wa", "lru_wx")


def _column_chunks(ref, matrix, first, last):
    return [([(ref, c, min(STAGE_COLS, last - c), 0)], matrix, c)
            for c in range(first, last, STAGE_COLS)]


class _WeightStream:
    def __init__(self, tasks, out, stage, ring, sem_in, sem_out):
        self.tasks, self.out, self.stage, self.ring = tasks, out, stage, ring
        self.sem_in, self.sem_out = sem_in, sem_out
        self.taken = 0
        for i in range(min(STAGE_SLOTS - 1, len(tasks))):
            self._start_in(i)

    def _shape(self, i):
        parts = self.tasks[i][0]
        return parts[0][0].shape[0], sum(part[2] for part in parts)

    def _in_copies(self, i):
        slot = i % STAGE_SLOTS
        rows, _ = self._shape(i)
        return [pltpu.make_async_copy(
            ref.at[:, pl.ds(col, cols)],
            self.stage.at[slot, pl.ds(0, rows), pl.ds(at, cols)],
            self.sem_in.at[slot]) for ref, col, cols, at in self.tasks[i][0]]

    def _start_in(self, i):
        for copy in self._in_copies(i):
            copy.start()

    def _out_copy(self, i):
        slot = i % RING_SLOTS
        rows, cols = self._shape(i)
        _, matrix, first = self.tasks[i]
        return pltpu.make_async_copy(
            self.ring.at[slot, pl.ds(0, rows // ROWS_PER_WORD), pl.ds(0, cols)],
            self.out[matrix].at[:, pl.ds(first, cols)],
            self.sem_out.at[slot])

    def take(self):
        i = self.taken
        self.taken += 1
        if i + STAGE_SLOTS - 1 < len(self.tasks):
            self._start_in(i + STAGE_SLOTS - 1)
        for copy in self._in_copies(i):
            copy.wait()
        if i >= RING_SLOTS:
            self._out_copy(i - RING_SLOTS).wait()
        rows, cols = self._shape(i)
        staged = self.stage[i % STAGE_SLOTS, 0:rows, 0:cols]
        words = rows // ROWS_PER_WORD
        self.ring[i % RING_SLOTS, 0:words, 0:cols] = _pack_rows(staged.astype(_BF16))
        self._out_copy(i).start()
        return self.ring.at[i % RING_SLOTS, pl.ds(0, words), pl.ds(0, cols)]

    def matmul(self, lhs, chunks):
        return jnp.concatenate([_dot_packed(lhs, self.take()[...]) for _ in range(chunks)],
                               axis=-1)

    def finish(self):
        assert self.taken == len(self.tasks)
        for i in range(max(0, len(self.tasks) - RING_SLOTS), len(self.tasks)):
            self._out_copy(i).wait()


def _sample_tasks(src):
    d = D_MODEL
    half = lambda name, h: _column_chunks(src[name], name, h * d, (h + 1) * d)
    gate = [([(src["lru_wa"], 0, LRU_BLOCK_DIM, 0),
              (src["lru_wx"], 0, LRU_BLOCK_DIM, LRU_BLOCK_DIM)], "w_gate", 0)]
    return (_column_chunks(src["w_in"], "w_in", 0, 5 * d) + gate
            + half("w_merge", 0) + half("w_branch_sg", 0) + half("w_merge", 1)
            + half("w_branch_lru", 0) + half("w_out", 0) + half("w_ple", 0)
            + half("w_ple_gate", 0))


def _sample_kernel(*refs, steps):
    n_v, n_s, n_m = len(_VECTOR_NAMES), len(_SOURCE_NAMES), len(_MATRIX_NAMES)
    x_ref, p_ref, conv_in_ref, h0_ref = refs[:4]
    w = dict(zip(_VECTOR_NAMES, refs[4:4 + n_v]))
    sgw_ref, sgb_ref = refs[4 + n_v], refs[5 + n_v]
    src = dict(zip(_SOURCE_NAMES, refs[6 + n_v:6 + n_v + n_s]))
    outs = refs[6 + n_v + n_s:]
    y_ref, conv_out_ref, lru_out_ref, z_out_ref = outs[:4]
    out_packed = dict(zip(_MATRIX_NAMES, outs[4:4 + n_m]))
    stage, ring, sem_in, sem_out = outs[4 + n_m:]
    d = D_MODEL
    nb = x_ref.shape[0]

    stream = _WeightStream(_sample_tasks(src), out_packed, stage, ring, sem_in, sem_out)
    chunks_per_block = d // STAGE_COLS

    def slab(v, t):
        return v[t * nb:(t + 1) * nb]

    x = jnp.concatenate([x_ref[:, t, :] for t in range(steps)], axis=0)
    p = jnp.concatenate([p_ref[:, t, :] for t in range(steps)], axis=0)
    hb = (_rms_scale(x) * w["norm_pre"][...]).astype(_BF16)
    proj = stream.matmul(hb, 5 * chunks_per_block)
    col = lambda c: proj[:, c * d:(c + 1) * d]

    z = _head_norm(col(COL_V), w["sg_norm"][...])
    for t in range(steps):
        z_out_ref[:, t, :] = slab(z, t)
    s_slabs = []
    for t in range(steps):
        acc = jnp.broadcast_to(sgb_ref[t:t + 1, :], (nb, d))
        for u in range(t + 1):
            acc = acc + sgw_ref[t * steps + u:t * steps + u + 1, :] * slab(z, u)
        s_slabs.append(acc)
    s = jnp.concatenate(s_slabs, axis=0)
    y_sg = col(COL_U) * s * _silu(col(COL_G_SG))

    x_lru = col(COL_X_LRU)
    hist = [conv_in_ref[k] for k in range(CONV_WIDTH - 1)]
    hist += [slab(x_lru, t) for t in range(steps)]
    for k in range(CONV_WIDTH - 1):
        conv_out_ref[k] = hist[steps + k]
    xc_slabs = []
    for t in range(steps):
        acc = w["conv_b"][...] + w["conv_w"][0:1, :] * hist[t]
        for k in range(1, CONV_WIDTH):
            acc = acc + w["conv_w"][k:k + 1, :] * hist[t + k]
        xc_slabs.append(acc)
    xc = jnp.concatenate(xc_slabs, axis=0)
    r_pre, i_pre = _lru_gates(xc.astype(_BF16), stream.take())
    a, bterm = _lru_coeffs(xc, r_pre, i_pre, w)
    h = h0_ref[...]
    h_slabs = []
    for t in range(steps):
        h = slab(a, t) * h + slab(bterm, t)
        h_slabs.append(h)
    lru_out_ref[...] = h
    y_lru = jnp.concatenate(h_slabs, axis=0) * _silu(col(COL_G_LRU))

    block = lambda lhs: stream.matmul(lhs, chunks_per_block)
    mm = {name: block for name in
          ("merge_a", "branch_sg", "merge_b", "branch_lru", "out", "ple", "ple_gate")}
    y = _post_mix(x, hb, y_sg, y_lru, p, w, mm)
    for t in range(steps):
        y_ref[:, t, :] = slab(y, t)

    stream.finish()


def _sample_call(x, p, conv_in, h0, vectors, sources, sgw_rows, sgb_rows):
    nb, steps, d = x.shape
    vmem = pl.BlockSpec(memory_space=pltpu.VMEM)
    hbm = pl.BlockSpec(memory_space=pl.ANY)
    matrix_shapes = [_MATRIX_SHAPES[m] for m in _MATRIX_NAMES]
    stage_rows = max(s.shape[0] for s in sources)
    out_shape = (
        jax.ShapeDtypeStruct((nb, steps, d), _F32),
        jax.ShapeDtypeStruct((CONV_WIDTH - 1, nb, d), _F32),
        jax.ShapeDtypeStruct((nb, d), _F32),
        jax.ShapeDtypeStruct((nb, steps, d), _F32),
    ) + tuple(jax.ShapeDtypeStruct((rows // ROWS_PER_WORD, cols), _U32)
              for rows, cols in matrix_shapes)
    outs = pl.pallas_call(
        functools.partial(_sample_kernel, steps=steps),
        in_specs=[vmem] * (6 + len(vectors)) + [hbm] * len(sources),
        out_specs=(vmem,) * 4 + (hbm,) * len(matrix_shapes),
        out_shape=out_shape,
        scratch_shapes=[
            pltpu.VMEM((STAGE_SLOTS, stage_rows, STAGE_COLS), _F32),
            pltpu.VMEM((RING_SLOTS, stage_rows // ROWS_PER_WORD, STAGE_COLS), _U32),
            pltpu.SemaphoreType.DMA((STAGE_SLOTS,)),
            pltpu.SemaphoreType.DMA((RING_SLOTS,)),
        ],
        compiler_params=pltpu.CompilerParams(vmem_limit_bytes=VMEM_LIMIT_BYTES),
        name="sample_layer",
    )(x, p, conv_in, h0, *vectors, sgw_rows, sgb_rows, *sources)
    return outs[:4], list(outs[4:])


def kernel(x_prompt, x_sample, p_prompt, p_sample, state_conv, state_lru, norm_pre, w_in, sg_norm,
           sg_w, sg_b, conv_w, conv_b, lru_wa, lru_ba, lru_wx, lru_bx, lru_lambda, w_branch_sg,
           w_branch_lru, w_merge, b_merge, w_out, norm_post, w_ple, w_ple_gate, b_ple_gate):
    depth = norm_pre.shape[0]
    nb, steps, d = x_sample.shape
    xp, xs = x_prompt, x_sample
    conv_p, lru_p, conv_s, lru_s, chunk_s = [], [], [], [], []
    for l in range(depth):
        row = lambda v: v[l].reshape(1, -1)
        by_name = dict(
            norm_pre=row(norm_pre), sg_norm=row(sg_norm), conv_w=conv_w[l], conv_b=row(conv_b),
            lru_ba=row(lru_ba), lru_bx=row(lru_bx), lru_lambda=row(lru_lambda),
            b_merge=row(b_merge), norm_post=row(norm_post), b_ple_gate=row(b_ple_gate))
        vectors = [by_name[n] for n in _VECTOR_NAMES]
        f32_by_name = dict(
            w_in=w_in[l], w_merge=w_merge[l], w_branch_sg=w_branch_sg[l],
            w_branch_lru=w_branch_lru[l], w_out=w_out[l], w_ple_gate=w_ple_gate[l],
            w_ple=w_ple[l], lru_wa=lru_wa[l].reshape(d, LRU_BLOCK_DIM),
            lru_wx=lru_wx[l].reshape(d, LRU_BLOCK_DIM))
        sources = [f32_by_name[name] for name in _SOURCE_NAMES]
        sgw_rows = jnp.repeat(sg_w[l][:, :steps, :steps].reshape(SG_HEADS, steps * steps).T,
                              SG_HEAD_DIM, axis=1)
        sgb_rows = jnp.repeat(sg_b[l][:, :steps].T, SG_HEAD_DIM, axis=1)
        (xs, cs, hs, zs), matrices = _sample_call(
            xs, p_sample[l], jnp.swapaxes(state_conv[l], 0, 1), state_lru[l], vectors, sources,
            sgw_rows, sgb_rows)
        conv_s.append(jnp.swapaxes(cs, 0, 1))
        lru_s.append(hs)
        chunk_s.append(zs)
        xp, cp, hp = _prompt_call(xp, p_prompt[l], vectors, matrices, sg_w[l], sg_b[l].T)
        conv_p.append(jnp.swapaxes(cp, 0, 1))
        lru_p.append(hp)
    return (xp, xs, jnp.stack(conv_p), jnp.stack(lru_p),
            jnp.stack(conv_s), jnp.stack(lru_s), jnp.stack(chunk_s))
```

```python
import functools

import jax
import jax.numpy as jnp
from jax import lax
from jax.experimental import pallas as pl
from jax.experimental.pallas import tpu as pltpu

D_MODEL = 1024
PLE_DIM = 256
SG_HEADS = 4
SG_HEAD_DIM = D_MODEL // SG_HEADS
CHUNK = 128
LRU_BLOCKS = 8
LRU_BLOCK_DIM = D_MODEL // LRU_BLOCKS
CONV_WIDTH = 4
LRU_C = 8.0
EPS = 1e-6
SQRT_FLOOR = 1e-30

SUBLANES = 8
LANES = 128
PROMPT_TILE = 256
LRU_HALVES = 2
PROMPT_LANES = 2
VMEM_LIMIT_BYTES = 56 * 1024 * 1024
STAGE_COLS = 1024
STAGE_SLOTS = 4
RING_SLOTS = 4

_BF16 = jnp.bfloat16
_F32 = jnp.float32
_U32 = jnp.uint32
ROWS_PER_WORD = 2

_VECTOR_NAMES = ("norm_pre", "sg_norm", "conv_w", "conv_b", "lru_ba", "lru_bx", "lru_lambda",
                 "b_merge", "norm_post", "b_ple_gate")
_MATRIX_SHAPES = dict(
    w_in=(D_MODEL, 5 * D_MODEL), w_merge=(D_MODEL, 2 * D_MODEL),
    w_branch_sg=(D_MODEL, D_MODEL), w_branch_lru=(D_MODEL, D_MODEL), w_out=(D_MODEL, D_MODEL),
    w_ple_gate=(D_MODEL, D_MODEL), w_ple=(PLE_DIM, D_MODEL),
    w_gate=(D_MODEL, 2 * LRU_BLOCK_DIM))
_MATRIX_NAMES = tuple(_MATRIX_SHAPES)

COL_U, COL_V, COL_G_SG, COL_X_LRU, COL_G_LRU = range(5)


def _dot(a, b):
    return jnp.dot(a, b, preferred_element_type=_F32)


def _pack_rows(wb):
    return pltpu.bitcast(wb, _U32)


def _dot_packed(a, words):
    return _dot(a, pltpu.bitcast(words, _BF16))


def _packed_rows(first, rows):
    return slice(first // ROWS_PER_WORD, (first + rows) // ROWS_PER_WORD)


def _rms_scale(x):
    var = jnp.mean(x * x, axis=-1, keepdims=True)
    return x * lax.rsqrt(var + EPS)


def _sigmoid(x):
    return 0.5 * jnp.tanh(0.5 * x) + 0.5


def _silu(x):
    return x * _sigmoid(x)


def _in_proj(hb, w, col):
    return _dot_packed(hb, w["w_in"][:, col * D_MODEL:(col + 1) * D_MODEL])


def _merge_pre(hb, w, half):
    return _dot_packed(hb, w["w_merge"][:, half * D_MODEL:(half + 1) * D_MODEL])


def _head_norm(v, sgn):
    zs = []
    for h in range(SG_HEADS):
        sl = slice(h * SG_HEAD_DIM, (h + 1) * SG_HEAD_DIM)
        zs.append(_rms_scale(v[:, sl]) * sgn[:, sl])
    return jnp.concatenate(zs, axis=-1)


def _lru_gates(xcb, wg_ref):
    gates = []
    for n in range(LRU_BLOCKS):
        sl = slice(n * LRU_BLOCK_DIM, (n + 1) * LRU_BLOCK_DIM)
        wg = wg_ref[_packed_rows(n * LRU_BLOCK_DIM, LRU_BLOCK_DIM), :]
        gates.append(_dot_packed(xcb[:, sl], wg))
    r_pre = jnp.concatenate([g[:, :LRU_BLOCK_DIM] for g in gates], axis=-1)
    i_pre = jnp.concatenate([g[:, LRU_BLOCK_DIM:] for g in gates], axis=-1)
    return r_pre, i_pre


def _lru_coeffs(xc, r_pre, i_pre, w):
    r = _sigmoid(r_pre + w["lru_ba"][...])
    i = _sigmoid(i_pre + w["lru_bx"][...])
    log_a = (-LRU_C * jax.nn.softplus(-w["lru_lambda"][...])) * r
    a = jnp.exp(log_a)
    gap = 1.0 - a * a
    mult = gap * lax.rsqrt(jnp.maximum(gap, SQRT_FLOOR))
    return a, mult * (i * xc)


def _merge_gate(pre, w, half):
    return _sigmoid(pre + w["b_merge"][:, half * D_MODEL:(half + 1) * D_MODEL])


def _post_mix(x, hb, y_sg, y_lru, p, w, mm):
    g_a = _merge_gate(mm["merge_a"](hb), w, 0)
    merged = g_a * mm["branch_sg"](y_sg.astype(_BF16))
    g_b = _merge_gate(mm["merge_b"](hb), w, 1)
    merged = merged + g_b * mm["branch_lru"](y_lru.astype(_BF16))
    o = mm["out"](merged.astype(_BF16))
    x1 = x + _rms_scale(o) * w["norm_post"][...]
    embedded = mm["ple"](p.astype(_BF16))
    gate = _sigmoid(mm["ple_gate"](x1.astype(_BF16)) + w["b_ple_gate"][...])
    return x1 + gate * embedded


def _resident_matmuls(w):
    return dict(
        merge_a=lambda lhs: _merge_pre(lhs, w, 0), merge_b=lambda lhs: _merge_pre(lhs, w, 1),
        branch_sg=lambda lhs: _dot_packed(lhs, w["w_branch_sg"][...]),
        branch_lru=lambda lhs: _dot_packed(lhs, w["w_branch_lru"][...]),
        out=lambda lhs: _dot_packed(lhs, w["w_out"][...]),
        ple_gate=lambda lhs: _dot_packed(lhs, w["w_ple_gate"][...]),
        ple=lambda lhs: _dot_packed(lhs, w["w_ple"][...]))


def _run_layout(tile):
    steps = tile // SUBLANES
    return steps, steps + SUBLANES


def _store_runs(buf, value, steps, pitch):
    for l in range(value.shape[1] // LANES):
        for s in range(SUBLANES):
            buf[l, s * pitch:s * pitch + steps, :] = (
                value[s * steps:(s + 1) * steps, l * LANES:(l + 1) * LANES])


def _load_runs(buf, steps, pitch):
    return jnp.concatenate(
        [jnp.concatenate([buf[l, s * pitch:s * pitch + steps, :] for l in range(buf.shape[0])],
                         axis=1) for s in range(SUBLANES)], axis=0)


def _load_step(buf, j, pitch):
    return jnp.concatenate([buf[l, pl.ds(j, SUBLANES, stride=pitch), :]
                            for l in range(buf.shape[0])], axis=1)


def _store_step(buf, j, pitch, slab):
    for l in range(buf.shape[0]):
        buf[l, pl.ds(j, SUBLANES, stride=pitch), :] = slab[:, l * LANES:(l + 1) * LANES]


def _from_previous_run(slab, first_run_rows):
    first = lax.broadcasted_iota(jnp.int32, slab.shape, 0) == 0
    return jnp.where(first, first_run_rows, pltpu.roll(slab, 1, 0))


def _conv_by_step(xs, conv_state, w):
    steps = len(xs)
    last = SUBLANES - 1
    before = [_from_previous_run(xs[steps - k], conv_state[SUBLANES - k:SUBLANES - k + 1, :])
              for k in range(CONV_WIDTH - 1, 0, -1)]
    for k in range(1, CONV_WIDTH):
        conv_state[SUBLANES - k:SUBLANES - k + 1, :] = xs[steps - k][last:, :]
    ext = before + xs
    out = []
    for j in range(steps):
        acc = w["conv_b"][...] + w["conv_w"][0:1, :] * ext[j]
        for k in range(1, CONV_WIDTH):
            acc = acc + w["conv_w"][k:k + 1, :] * ext[j + k]
        out.append(acc)
    return out


def _scan_by_step(a, b, h_prev):
    steps = a.shape[0] // SUBLANES
    slab = lambda v, j: v[j * SUBLANES:(j + 1) * SUBLANES, :]
    h0, decay = [slab(b, 0)], [slab(a, 0)]
    for j in range(1, steps):
        h0.append(slab(a, j) * h0[-1] + slab(b, j))
        decay.append(slab(a, j) * decay[-1])
    end_h, end_decay = h0[-1], decay[-1]
    row = lax.broadcasted_iota(jnp.int32, end_h.shape, 0)
    shift = 1
    while shift < SUBLANES:
        keep = row >= shift
        end_h = end_h + end_decay * jnp.where(keep, pltpu.roll(end_h, shift, 0), 0.0)
        end_decay = end_decay * jnp.where(keep, pltpu.roll(end_decay, shift, 0), 1.0)
        shift *= 2
    ends = end_h + end_decay * h_prev
    entering = _from_previous_run(ends, h_prev)
    hs = [h0[j] + decay[j] * entering for j in range(steps)]
    return hs, ends[SUBLANES - 1:, :]


def _spatial_mix_chunk(z, c, sgw_ref, sgb_ref):
    tri = (lax.broadcasted_iota(jnp.int32, (CHUNK, CHUNK), 0)
           >= lax.broadcasted_iota(jnp.int32, (CHUNK, CHUNK), 1))
    s_heads = []
    for h in range(SG_HEADS):
        wm = jnp.where(tri, sgw_ref[h], 0.0).astype(_BF16)
        zc = z[c * CHUNK:(c + 1) * CHUNK, h * SG_HEAD_DIM:(h + 1) * SG_HEAD_DIM]
        s_heads.append(_dot(wm, zc) + sgb_ref[:, h:h + 1])
    return jnp.concatenate(s_heads, axis=-1)


_MIDDLE_MATMUL_ORDER = (
    ("gate", 0), ("gate", 1), ("gate", 2), ("gate", 3), ("wide", COL_U), ("spatial", 0),
    ("gate", 4), ("gate", 5), ("gate", 6), ("gate", 7), ("wide", COL_G_SG), ("spatial", 1),
    ("wide", COL_G_LRU))


def _tile_layer(x, p, runs_buf, conv_state, h_state, gate_bufs, w, sgw_ref, sgb_ref):
    tile = x.shape[0]
    hb = (_rms_scale(x) * w["norm_pre"][...]).astype(_BF16)

    steps, pitch = _run_layout(tile)
    _store_runs(runs_buf, _in_proj(hb, w, COL_X_LRU), steps, pitch)

    v = _in_proj(hb, w, COL_V)
    z = _head_norm(v, w["sg_norm"][...]).astype(_BF16)

    group = D_MODEL // LRU_HALVES
    views = []
    for g in range(LRU_HALVES):
        cols = pl.ds(g * group, group)
        views.append(dict(
            runs=runs_buf.at[g * group // LANES:(g + 1) * group // LANES],
            conv_state=conv_state.at[:, cols], h_state=h_state.at[:, cols],
            w={n: w[n].at[:, cols] for n in ("conv_w", "conv_b", "lru_ba", "lru_bx", "lru_lambda")}))
    xc, xcb = [], []
    for vw in views:
        xs = [_load_step(vw["runs"], j, pitch) for j in range(steps)]
        xc.append(jnp.concatenate(_conv_by_step(xs, vw["conv_state"], vw["w"]), axis=0))
        xcb.append(xc[-1].astype(_BF16))
    blocks_per_group = LRU_BLOCKS // LRU_HALVES
    wide, s_chunks = {}, {}
    for kind, k in _MIDDLE_MATMUL_ORDER:
        if kind == "wide":
            wide[k] = _in_proj(hb, w, k)
        elif kind == "gate":
            local = (k % blocks_per_group) * LRU_BLOCK_DIM
            lhs = xcb[k // blocks_per_group][:, local:local + LRU_BLOCK_DIM]
            wg = w["w_gate"][_packed_rows(k * LRU_BLOCK_DIM, LRU_BLOCK_DIM), :]
            at = (k % 2) * 2 * LRU_BLOCK_DIM
            gate_bufs[k // 2][:, at:at + 2 * LRU_BLOCK_DIM] = _dot_packed(lhs, wg)
        else:
            s_chunks[k] = _spatial_mix_chunk(z, k, sgw_ref, sgb_ref)
    u, g_sg, g_lru = wide[COL_U], wide[COL_G_SG], wide[COL_G_LRU]
    s = jnp.concatenate([s_chunks[c] for c in range(tile // CHUNK)], axis=0)

    rows = pl.ds(pl.multiple_of(jnp.minimum(pl.program_id(1), 0), SUBLANES), tile)

    def parked(n, half):
        at = ((n % 2) * 2 + half) * LRU_BLOCK_DIM
        return gate_bufs[n // 2][rows, at:at + LRU_BLOCK_DIM]

    y_sg = u * s * _silu(g_sg)

    h_last = []
    for g, vw in enumerate(views):
        blocks = range(g * blocks_per_group, (g + 1) * blocks_per_group)
        r_pre = jnp.concatenate([parked(n, 0) for n in blocks], axis=-1)
        i_pre = jnp.concatenate([parked(n, 1) for n in blocks], axis=-1)
        a, bterm = _lru_coeffs(xc[g], r_pre, i_pre, vw["w"])
        hs, h_end = _scan_by_step(a, bterm, vw["h_state"][...])
        vw["h_state"][...] = h_end
        for j in range(steps):
            _store_step(vw["runs"], j, pitch, hs[j])
        h_last.append(h_end)
    h_last = jnp.concatenate(h_last, axis=-1)
    y_lru = _load_runs(runs_buf, steps, pitch) * _silu(g_lru)

    return _post_mix(x, hb, y_sg, y_lru, p, w, _resident_matmuls(w)), h_last


def _prompt_kernel(*refs):
    n_w = len(_VECTOR_NAMES) + len(_MATRIX_NAMES)
    x_ref, p_ref = refs[0], refs[1]
    w = dict(zip(_VECTOR_NAMES + _MATRIX_NAMES, refs[2:2 + n_w]))
    sgw_ref, sgb_ref = refs[2 + n_w], refs[3 + n_w]
    y_ref, conv_out_ref, lru_out_ref = refs[4 + n_w:7 + n_w]
    runs_buf, conv_state, h_state = refs[7 + n_w:10 + n_w]
    gate_bufs = refs[10 + n_w:]

    t = pl.program_id(1)
    lanes = x_ref.shape[0]

    @pl.when(t == 0)
    def _():
        conv_state[...] = jnp.zeros_like(conv_state)
        h_state[...] = jnp.zeros_like(h_state)

    h_last = []
    for i in range(lanes):
        y, h = _tile_layer(x_ref[i], p_ref[i], runs_buf.at[i], conv_state.at[i], h_state.at[i],
                           [g.at[i] for g in gate_bufs], w, sgw_ref, sgb_ref)
        y_ref[i] = y
        h_last.append(h)

    @pl.when(t == pl.num_programs(1) - 1)
    def _():
        for i in range(lanes):
            seq = pl.ds(i * pl.num_programs(0) + pl.program_id(0), 1)
            for k in range(CONV_WIDTH - 1):
                row = SUBLANES - (CONV_WIDTH - 1) + k
                conv_out_ref[k, seq, :] = conv_state[i, row:row + 1, :]
            lru_out_ref[seq, :] = h_last[i]


def _resident(shape):
    return pl.BlockSpec(shape, lambda *_: (0,) * len(shape), pipeline_mode=pl.Buffered(1))


def _prompt_call(x, p, vectors, matrices, sgw, sgb_t):
    batch, seq, d = x.shape
    tile = PROMPT_TILE
    lanes = PROMPT_LANES
    grid = (batch // lanes, seq // tile)
    x4 = x.reshape(lanes, batch // lanes, seq, d)
    p4 = p.reshape(lanes, batch // lanes, seq, PLE_DIM)
    in_specs = [
        pl.BlockSpec((lanes, None, tile, d), lambda b, t: (0, b, t, 0)),
        pl.BlockSpec((lanes, None, tile, PLE_DIM), lambda b, t: (0, b, t, 0)),
    ]
    in_specs += [_resident(v.shape) for v in vectors + matrices]
    in_specs += [_resident(sgw.shape), _resident(sgb_t.shape)]
    out_shape = (
        jax.ShapeDtypeStruct((lanes, batch // lanes, seq, d), _F32),
        jax.ShapeDtypeStruct((CONV_WIDTH - 1, batch, d), _F32),
        jax.ShapeDtypeStruct((batch, d), _F32),
    )
    out_specs = (
        pl.BlockSpec((lanes, None, tile, d), lambda b, t: (0, b, t, 0)),
        pl.BlockSpec((CONV_WIDTH - 1, batch, d), lambda b, t: (0, 0, 0)),
        pl.BlockSpec((batch, d), lambda b, t: (0, 0)),
    )
    y, conv_rows, state = pl.pallas_call(
        _prompt_kernel,
        grid=grid,
        in_specs=in_specs,
        out_specs=out_specs,
        out_shape=out_shape,
        scratch_shapes=[
            pltpu.VMEM((lanes, d // LANES, SUBLANES * _run_layout(tile)[1], LANES), _F32),
            pltpu.VMEM((lanes, SUBLANES, d), _F32),
            pltpu.VMEM((lanes, 1, d), _F32),
        ] + [pltpu.VMEM((lanes, tile, 4 * LRU_BLOCK_DIM), _F32)
             for _ in range(LRU_BLOCKS // 2)],
        compiler_params=pltpu.CompilerParams(
            dimension_semantics=("arbitrary", "arbitrary"),
            vmem_limit_bytes=VMEM_LIMIT_BYTES),
        name="prompt_layer",
    )(x4, p4, *vectors, *matrices, sgw, sgb_t)
    return y.reshape(batch, seq, d), conv_rows, state


_SOURCE_NAMES = ("w_in", "w_merge", "w_branch_sg", "w_branch_lru", "w_out", "w_ple_gate",
                 "w_ple", "lru_wa", "lru_wx")


def _column_chunks(ref, matrix, first, last):
    return [([(ref, c, min(STAGE_COLS, last - c), 0)], matrix, c)
            for c in range(first, last, STAGE_COLS)]


class _WeightStream:
    def __init__(self, tasks, out, stage, ring, sem_in, sem_out):
        self.tasks, self.out, self.stage, self.ring = tasks, out, stage, ring
        self.sem_in, self.sem_out = sem_in, sem_out
        self.taken = 0
        for i in range(min(STAGE_SLOTS - 1, len(tasks))):
            self._start_in(i)

    def _shape(self, i):
        parts = self.tasks[i][0]
        return parts[0][0].shape[0], sum(part[2] for part in parts)

    def _in_copies(self, i):
        slot = i % STAGE_SLOTS
        rows, _ = self._shape(i)
        return [pltpu.make_async_copy(
            ref.at[:, pl.ds(col, cols)],
            self.stage.at[slot, pl.ds(0, rows), pl.ds(at, cols)],
            self.sem_in.at[slot]) for ref, col, cols, at in self.tasks[i][0]]

    def _start_in(self, i):
        for copy in self._in_copies(i):
            copy.start()

    def _out_copy(self, i):
        slot = i % RING_SLOTS
        rows, cols = self._shape(i)
        _, matrix, first = self.tasks[i]
        return pltpu.make_async_copy(
            self.ring.at[slot, pl.ds(0, rows // ROWS_PER_WORD), pl.ds(0, cols)],
            self.out[matrix].at[:, pl.ds(first, cols)],
            self.sem_out.at[slot])

    def take(self):
        i = self.taken
        self.taken += 1
        if i + STAGE_SLOTS - 1 < len(self.tasks):
            self._start_in(i + STAGE_SLOTS - 1)
        for copy in self._in_copies(i):
            copy.wait()
        if i >= RING_SLOTS:
            self._out_copy(i - RING_SLOTS).wait()
        rows, cols = self._shape(i)
        staged = self.stage[i % STAGE_SLOTS, 0:rows, 0:cols]
        words = rows // ROWS_PER_WORD
        self.ring[i % RING_SLOTS, 0:words, 0:cols] = _pack_rows(staged.astype(_BF16))
        self._out_copy(i).start()
        return self.ring.at[i % RING_SLOTS, pl.ds(0, words), pl.ds(0, cols)]

    def matmul(self, lhs, chunks):
        return jnp.concatenate([_dot_packed(lhs, self.take()[...]) for _ in range(chunks)],
                               axis=-1)

    def finish(self):
        assert self.taken == len(self.tasks)
        for i in range(max(0, len(self.tasks) - RING_SLOTS), len(self.tasks)):
            self._out_copy(i).wait()


def _sample_tasks(src):
    d = D_MODEL
    half = lambda name, h: _column_chunks(src[name], name, h * d, (h + 1) * d)
    gate = [([(src["lru_wa"], 0, LRU_BLOCK_DIM, 0),
              (src["lru_wx"], 0, LRU_BLOCK_DIM, LRU_BLOCK_DIM)], "w_gate", 0)]
    return (_column_chunks(src["w_in"], "w_in", 0, 5 * d) + gate
            + half("w_merge", 0) + half("w_branch_sg", 0) + half("w_merge", 1)
            + half("w_branch_lru", 0) + half("w_out", 0) + half("w_ple", 0)
            + half("w_ple_gate", 0))


def _sample_kernel(*refs, steps):
    n_v, n_s, n_m = len(_VECTOR_NAMES), len(_SOURCE_NAMES), len(_MATRIX_NAMES)
    x_ref, p_ref, conv_in_ref, h0_ref = refs[:4]
    w = dict(zip(_VECTOR_NAMES, refs[4:4 + n_v]))
    sgw_ref, sgb_ref = refs[4 + n_v], refs[5 + n_v]
    src = dict(zip(_SOURCE_NAMES, refs[6 + n_v:6 + n_v + n_s]))
    outs = refs[6 + n_v + n_s:]
    y_ref, conv_out_ref, lru_out_ref, z_out_ref = outs[:4]
    out_packed = dict(zip(_MATRIX_NAMES, outs[4:4 + n_m]))
    stage, ring, sem_in, sem_out = outs[4 + n_m:]
    d = D_MODEL
    nb = x_ref.shape[0]

    stream = _WeightStream(_sample_tasks(src), out_packed, stage, ring, sem_in, sem_out)
    chunks_per_block = d // STAGE_COLS

    def slab(v, t):
        return v[t * nb:(t + 1) * nb]

    x = jnp.concatenate([x_ref[:, t, :] for t in range(steps)], axis=0)
    p = jnp.concatenate([p_ref[:, t, :] for t in range(steps)], axis=0)
    hb = (_rms_scale(x) * w["norm_pre"][...]).astype(_BF16)
    proj = stream.matmul(hb, 5 * chunks_per_block)
    col = lambda c: proj[:, c * d:(c + 1) * d]

    z = _head_norm(col(COL_V), w["sg_norm"][...])
    for t in range(steps):
        z_out_ref[:, t, :] = slab(z, t)
    s_slabs = []
    for t in range(steps):
        acc = jnp.broadcast_to(sgb_ref[t:t + 1, :], (nb, d))
        for u in range(t + 1):
            acc = acc + sgw_ref[t * steps + u:t * steps + u + 1, :] * slab(z, u)
        s_slabs.append(acc)
    s = jnp.concatenate(s_slabs, axis=0)
    y_sg = col(COL_U) * s * _silu(col(COL_G_SG))

    x_lru = col(COL_X_LRU)
    hist = [conv_in_ref[k] for k in range(CONV_WIDTH - 1)]
    hist += [slab(x_lru, t) for t in range(steps)]
    for k in range(CONV_WIDTH - 1):
        conv_out_ref[k] = hist[steps + k]
    xc_slabs = []
    for t in range(steps):
        acc = w["conv_b"][...] + w["conv_w"][0:1, :] * hist[t]
        for k in range(1, CONV_WIDTH):
            acc = acc + w["conv_w"][k:k + 1, :] * hist[t + k]
        xc_slabs.append(acc)
    xc = jnp.concatenate(xc_slabs, axis=0)
    r_pre, i_pre = _lru_gates(xc.astype(_BF16), stream.take())
    a, bterm = _lru_coeffs(xc, r_pre, i_pre, w)
    h = h0_ref[...]
    h_slabs = []
    for t in range(steps):
        h = slab(a, t) * h + slab(bterm, t)
        h_slabs.append(h)
    lru_out_ref[...] = h
    y_lru = jnp.concatenate(h_slabs, axis=0) * _silu(col(COL_G_LRU))

    block = lambda lhs: stream.matmul(lhs, chunks_per_block)
    mm = {name: block for name in
          ("merge_a", "branch_sg", "merge_b", "branch_lru", "out", "ple", "ple_gate")}
    y = _post_mix(x, hb, y_sg, y_lru, p, w, mm)
    for t in range(steps):
        y_ref[:, t, :] = slab(y, t)

    stream.finish()


def _sample_call(x, p, conv_in, h0, vectors, sources, sgw_rows, sgb_rows):
    nb, steps, d = x.shape
    vmem = pl.BlockSpec(memory_space=pltpu.VMEM)
    hbm = pl.BlockSpec(memory_space=pl.ANY)
    matrix_shapes = [_MATRIX_SHAPES[m] for m in _MATRIX_NAMES]
    stage_rows = max(s.shape[0] for s in sources)
    out_shape = (
        jax.ShapeDtypeStruct((nb, steps, d), _F32),
        jax.ShapeDtypeStruct((CONV_WIDTH - 1, nb, d), _F32),
        jax.ShapeDtypeStruct((nb, d), _F32),
        jax.ShapeDtypeStruct((nb, steps, d), _F32),
    ) + tuple(jax.ShapeDtypeStruct((rows // ROWS_PER_WORD, cols), _U32)
              for rows, cols in matrix_shapes)
    outs = pl.pallas_call(
        functools.partial(_sample_kernel, steps=steps),
        in_specs=[vmem] * (6 + len(vectors)) + [hbm] * len(sources),
        out_specs=(vmem,) * 4 + (hbm,) * len(matrix_shapes),
        out_shape=out_shape,
        scratch_shapes=[
            pltpu.VMEM((STAGE_SLOTS, stage_rows, STAGE_COLS), _F32),
            pltpu.VMEM((RING_SLOTS, stage_rows // ROWS_PER_WORD, STAGE_COLS), _U32),
            pltpu.SemaphoreType.DMA((STAGE_SLOTS,)),
            pltpu.SemaphoreType.DMA((RING_SLOTS,)),
        ],
        compiler_params=pltpu.CompilerParams(vmem_limit_bytes=VMEM_LIMIT_BYTES),
        name="sample_layer",
    )(x, p, conv_in, h0, *vectors, sgw_rows, sgb_rows, *sources)
    return outs[:4], list(outs[4:])


def kernel(x_prompt, x_sample, p_prompt, p_sample, state_conv, state_lru, norm_pre, w_in, sg_norm,
           sg_w, sg_b, conv_w, conv_b, lru_wa, lru_ba, lru_wx, lru_bx, lru_lambda, w_branch_sg,
           w_branch_lru, w_merge, b_merge, w_out, norm_post, w_ple, w_ple_gate, b_ple_gate):
    depth = norm_pre.shape[0]
    nb, steps, d = x_sample.shape
    xp, xs = x_prompt, x_sample
    conv_p, lru_p, conv_s, lru_s, chunk_s = [], [], [], [], []
    for l in range(depth):
        row = lambda v: v[l].reshape(1, -1)
        by_name = dict(
            norm_pre=row(norm_pre), sg_norm=row(sg_norm), conv_w=conv_w[l], conv_b=row(conv_b),
            lru_ba=row(lru_ba), lru_bx=row(lru_bx), lru_lambda=row(lru_lambda),
            b_merge=row(b_merge), norm_post=row(norm_post), b_ple_gate=row(b_ple_gate))
        vectors = [by_name[n] for n in _VECTOR_NAMES]
        f32_by_name = dict(
            w_in=w_in[l], w_merge=w_merge[l], w_branch_sg=w_branch_sg[l],
            w_branch_lru=w_branch_lru[l], w_out=w_out[l], w_ple_gate=w_ple_gate[l],
            w_ple=w_ple[l], lru_wa=lru_wa[l].reshape(d, LRU_BLOCK_DIM),
            lru_wx=lru_wx[l].reshape(d, LRU_BLOCK_DIM))
        sources = [f32_by_name[name] for name in _SOURCE_NAMES]
        sgw_rows = jnp.repeat(sg_w[l][:, :steps, :steps].reshape(SG_HEADS, steps * steps).T,
                              SG_HEAD_DIM, axis=1)
        sgb_rows = jnp.repeat(sg_b[l][:, :steps].T, SG_HEAD_DIM, axis=1)
        (xs, cs, hs, zs), matrices = _sample_call(
            xs, p_sample[l], jnp.swapaxes(state_conv[l], 0, 1), state_lru[l], vectors, sources,
            sgw_rows, sgb_rows)
        conv_s.append(jnp.swapaxes(cs, 0, 1))
        lru_s.append(hs)
        chunk_s.append(zs)
        xp, cp, hp = _prompt_call(xp, p_prompt[l], vectors, matrices, sg_w[l], sg_b[l].T)
        conv_p.append(jnp.swapaxes(cp, 0, 1))
        lru_p.append(hp)
    return (xp, xs, jnp.stack(conv_p), jnp.stack(lru_p),
            jnp.stack(conv_s), jnp.stack(lru_s), jnp.stack(chunk_s))
```
